```python
import math
import jax
import jax.numpy as jnp
from jax import lax
import numpy as np

D_MODEL = 1024
BATCH = 2
SEQ = 8192
DEPTH = 2
DEC_BATCH = 32
DEC_SEQ = 8
PAST_LEN = 16384
PAGE_SIZE = 128

N_BRANCH = 4
BRANCH_W = D_MODEL // 4
DSA_HEADS = 4
DSA_HEAD_DIM = BRANCH_W // DSA_HEADS
IDX_HEADS = 4
IDX_DIM = 64
IDX_TOPK_MAX = 256
DIFF_HEADS = 4
DIFF_HEAD_DIM = BRANCH_W // (2 * DIFF_HEADS)
DIFF_V_DIM = 2 * DIFF_HEAD_DIM
SSD_HEADS = 4
SSD_HEAD_DIM = BRANCH_W // SSD_HEADS
SSD_STATE = 64
SSD_GROUPS = 2
SSD_CONV = 4
SSD_CHUNK = 128
SSD_CONV_DIM = BRANCH_W + 2 * SSD_GROUPS * SSD_STATE
POOL_WINDOWS = (2, 4, 8, 16)
POOL_GROUP = BRANCH_W // 4
POOL_STATE = 16 - 1
Q_BLOCK = 128
ROPE_THETA = 10000.0
EPS = 1e-6

IN_SIZES = (BRANCH_W, BRANCH_W, BRANCH_W, BRANCH_W,
            IDX_HEADS * IDX_DIM, IDX_DIM, IDX_HEADS,
            BRANCH_W, BRANCH_W, DIFF_HEADS * DIFF_V_DIM, BRANCH_W,
            BRANCH_W, SSD_CONV_DIM, SSD_HEADS,
            BRANCH_W, BRANCH_W,
            N_BRANCH * D_MODEL)
IN_TOTAL = sum(IN_SIZES)

kernel_name = 'hybrid_dsa_diff_ssd_pool_decode_step'


def rms_norm(x, g):
    xf = x.astype(jnp.float32)
    y = xf * lax.rsqrt(jnp.mean(xf * xf, axis=-1, keepdims=True) + EPS)
    return (y * g.astype(jnp.float32)).astype(x.dtype)


def rope(x, pos):
    half = x.shape[-1] // 2
    inv = ROPE_THETA ** (-jnp.arange(half, dtype=jnp.float32) / half)
    ang = pos.astype(jnp.float32)[:, None] * inv[None, :]
    cos = jnp.cos(ang)[None, :, None, :]
    sin = jnp.sin(ang)[None, :, None, :]
    xf = x.astype(jnp.float32)
    x1, x2 = xf[..., :half], xf[..., half:]
    return jnp.concatenate([x1 * cos - x2 * sin, x2 * cos + x1 * sin], axis=-1).astype(x.dtype)


def project(x, pos, norm_g, w_in, dsa_qk_g, diff_qk_g):
    b, t, _ = x.shape
    h = rms_norm(x, norm_g)
    z = jnp.einsum('btd,de->bte', h, w_in)
    offs = [int(v) for v in np.cumsum(IN_SIZES)[:-1]]
    (dq, dk, dv, dg, iq, ik, iw, fq, fk, fv, fg, sz, sxbc, sdt, pu, pg, mg) = jnp.split(z, offs, axis=-1)
    def heads(a, n):
        return a.reshape(b, t, n, -1)
    return dict(
        dq=rope(rms_norm(heads(dq, DSA_HEADS), dsa_qk_g[0]), pos),
        dk=rope(rms_norm(heads(dk, DSA_HEADS), dsa_qk_g[1]), pos),
        dv=heads(dv, DSA_HEADS), dg=dg,
        iq=rope(heads(iq, IDX_HEADS), pos),
        ik=rope(ik[:, :, None, :], pos)[:, :, 0],
        iw=iw * IDX_HEADS ** -0.5,
        fq=rope(rms_norm(heads(fq, 2 * DIFF_HEADS), diff_qk_g[0]), pos),
        fk=rope(rms_norm(heads(fk, 2 * DIFF_HEADS), diff_qk_g[1]), pos),
        fv=heads(fv, DIFF_HEADS), fg=fg,
        sz=sz, sxbc=sxbc, sdt=sdt, pu=pu, pg=pg,
        mg=mg.reshape(b, t, N_BRANCH, D_MODEL))


def to_blocks(a, qb):
    b, s = a.shape[:2]
    return jnp.moveaxis(a.reshape(b, s // qb, qb, *a.shape[2:]), 1, 0)


def from_blocks(a):
    a = jnp.moveaxis(a, 0, 1)
    return a.reshape(a.shape[0], -1, *a.shape[3:])


def index_scores(iq, iw, ik, q_pos, k_pos):
    s = jnp.einsum('bqhd,bkd->bqhk', iq, ik, preferred_element_type=jnp.float32) * IDX_DIM ** -0.5
    s = jnp.einsum('bqh,bqhk->bqk', iw.astype(jnp.float32), jax.nn.relu(s))
    return jnp.where(k_pos[None, None, :] <= q_pos[None, :, None], s, -jnp.inf)


def topk_attend(q, kg, vg, valid):
    s = jnp.einsum('bqhd,bqkhd->bqhk', q, kg, preferred_element_type=jnp.float32) * DSA_HEAD_DIM ** -0.5
    p = jax.nn.softmax(jnp.where(valid[:, :, None, :], s, -jnp.inf), axis=-1)
    return jnp.einsum('bqhk,bqkhd->bqhd', p, vg.astype(jnp.float32)).astype(q.dtype)


def diff_attend(q, k, v, q_pos, k_pos, lam, lam_init, subln_g):
    b, tq = q.shape[:2]
    s = jnp.einsum('bqhd,bkhd->bhqk', q, k, preferred_element_type=jnp.float32) * DIFF_HEAD_DIM ** -0.5
    s = jnp.where((k_pos[None, :] <= q_pos[:, None])[None, None], s, -jnp.inf)
    p = jax.nn.softmax(s, axis=-1).reshape(b, DIFF_HEADS, 2, tq, -1)
    a = p[:, :, 0] - lam * p[:, :, 1]
    o = jnp.einsum('bhqk,bkhd->bqhd', a, v.astype(jnp.float32))
    o = rms_norm(o, subln_g) * (1.0 - lam_init)
    return o.reshape(b, tq, BRANCH_W).astype(q.dtype)


def dsa_prompt(p, pos):
    b, s = p['dq'].shape[:2]
    k_top = min(IDX_TOPK_MAX, s // 4)
    qb = min(Q_BLOCK, s)
    take = jax.vmap(lambda rows, idx: rows[idx])
    def block(args):
        q, iq, iw, qpos = args
        _, idx = lax.top_k(index_scores(iq, iw, p['ik'], qpos, pos), k_top)
        valid = idx <= qpos[None, :, None]
        return topk_attend(q, take(p['dk'], idx), take(p['dv'], idx), valid)
    o = lax.map(block, (to_blocks(p['dq'], qb), to_blocks(p['iq'], qb), to_blocks(p['iw'], qb),
                        pos.reshape(-1, qb)))
    return from_blocks(o).reshape(b, s, BRANCH_W)


def dsa_sample(p, pos, k_pos, cache_k, cache_v, cache_ik, layer, page_table):
    bs, t = p['dq'].shape[:2]
    past = page_table.shape[1] * PAGE_SIZE
    ik_all = jnp.concatenate([cache_ik[layer, page_table].reshape(bs, past, IDX_DIM), p['ik']], axis=1)
    k_top = min(IDX_TOPK_MAX, (past + t) // 4)
    _, idx = lax.top_k(index_scores(p['iq'], p['iw'], ik_all, pos, k_pos), k_top)
    valid = idx <= pos[None, :, None]
    in_past = (idx < past)[..., None, None]
    pidx = jnp.minimum(idx, past - 1)
    phys = jax.vmap(lambda row, pg: row[pg])(page_table, pidx // PAGE_SIZE)
    off = pidx % PAGE_SIZE
    nidx = jnp.clip(idx - past, 0, t - 1)
    take = jax.vmap(lambda rows, i: rows[i])
    kg = jnp.where(in_past, cache_k[layer, phys, off], take(p['dk'], nidx))
    vg = jnp.where(in_past, cache_v[layer, phys, off], take(p['dv'], nidx))
    return topk_attend(p['dq'], kg, vg, valid).reshape(bs, t, BRANCH_W)


def diff_prompt(p, pos, lam, lam_init, subln_g):
    s = p['fq'].shape[1]
    qb = min(Q_BLOCK, s)
    def block(args):
        q, qpos = args
        return diff_attend(q, p['fk'], p['fv'], qpos, pos, lam, lam_init, subln_g)
    return from_blocks(lax.map(block, (to_blocks(p['fq'], qb), pos.reshape(-1, qb))))


def diff_sample(p, pos, k_pos, cache_k, cache_v, layer, page_table, lam, lam_init, subln_g):
    bs = p['fq'].shape[0]
    past = page_table.shape[1] * PAGE_SIZE
    k_all = jnp.concatenate([cache_k[layer, page_table].reshape(bs, past, 2 * DIFF_HEADS, DIFF_HEAD_DIM),
                             p['fk']], axis=1)
    v_all = jnp.concatenate([cache_v[layer, page_table].reshape(bs, past, DIFF_HEADS, DIFF_V_DIM),
                             p['fv']], axis=1)
    return diff_attend(p['fq'], k_all, v_all, pos, k_pos, lam, lam_init, subln_g)


def ssd_scan(xdt, adt, b_in, c_in, h0):
    bsz, L, H, P = xdt.shape
    Q = math.gcd(L, SSD_CHUNK)
    nc = L // Q
    rep = H // SSD_GROUPS
    Bh = jnp.repeat(b_in, rep, axis=2).reshape(bsz, nc, Q, H, -1)
    Ch = jnp.repeat(c_in, rep, axis=2).reshape(bsz, nc, Q, H, -1)
    X = xdt.reshape(bsz, nc, Q, H, P)
    A_cs = jnp.cumsum(adt.reshape(bsz, nc, Q, H).transpose(0, 3, 1, 2), axis=-1)
    causal = jnp.tril(jnp.ones((Q, Q), bool))
    Lmat = jnp.exp(jnp.where(causal, A_cs[..., :, None] - A_cs[..., None, :], -jnp.inf))
    CB = jnp.einsum('bclhn,bcshn->bhcls', Ch, Bh)
    y_diag = jnp.einsum('bhcls,bcshp->bclhp', CB * Lmat, X)
    decay_in = jnp.exp(A_cs[..., -1:] - A_cs)
    states = jnp.einsum('bclhn,bhcl,bclhp->bchpn', Bh, decay_in, X)
    chunk_decay = jnp.exp(A_cs[..., -1])
    def step(h, inp):
        st, dec = inp
        return h * dec[..., None, None] + st, h
    h_final, h_prev = lax.scan(step, h0, (jnp.moveaxis(states, 1, 0), jnp.moveaxis(chunk_decay, 2, 0)))
    h_prev = jnp.moveaxis(h_prev, 0, 1)
    y_off = jnp.einsum('bclhn,bchpn,bhcl->bclhp', Ch, h_prev, jnp.exp(A_cs))
    return (y_diag + y_off).reshape(bsz, L, H, P), h_final


def ssd_mix(xbc, z, dt_raw, conv_prev, ssm_prev, conv_w, conv_b, dt_bias, a_log, d_skip, norm_g):
    bsz, L, _ = xbc.shape
    ext = jnp.concatenate([conv_prev.astype(xbc.dtype), xbc], axis=1)
    conv = conv_b.astype(jnp.float32)
    for k in range(SSD_CONV):
        conv = conv + ext[:, k:k + L].astype(jnp.float32) * conv_w[k].astype(jnp.float32)
    u = jax.nn.silu(conv)
    xs, bb, cc = jnp.split(u, [BRANCH_W, BRANCH_W + SSD_GROUPS * SSD_STATE], axis=-1)
    xs = xs.reshape(bsz, L, SSD_HEADS, SSD_HEAD_DIM)
    bb = bb.reshape(bsz, L, SSD_GROUPS, SSD_STATE)
    cc = cc.reshape(bsz, L, SSD_GROUPS, SSD_STATE)
    dt = jax.nn.softplus(dt_raw.astype(jnp.float32) + dt_bias.astype(jnp.float32))
    a = -jnp.exp(a_log.astype(jnp.float32))
    y, h = ssd_scan(xs * dt[..., None], dt * a, bb, cc, ssm_prev.astype(jnp.float32))
    y = y + xs * d_skip.astype(jnp.float32)[:, None]
    g = (y.reshape(bsz, L, BRANCH_W) * jax.nn.silu(z.astype(jnp.float32))).reshape(bsz, L, SSD_GROUPS, -1)
    y = rms_norm(g, norm_g.reshape(SSD_GROUPS, -1)).reshape(bsz, L, BRANCH_W)
    return y.astype(z.dtype), h, ext[:, L:]


def pool_mix(u, prev, start_pos, w_grp, scale):
    bsz, L, _ = u.shape
    ext = jnp.concatenate([prev.astype(u.dtype), u], axis=1)
    ef = ext.astype(jnp.float32)
    cs = jnp.concatenate([jnp.zeros((bsz, 1, BRANCH_W), jnp.float32), jnp.cumsum(ef, axis=1)], axis=1)
    n_avail = (start_pos + jnp.arange(L) + 1).astype(jnp.float32)
    means = []
    for g, w in enumerate(POOL_WINDOWS):
        ch = slice(g * POOL_GROUP, (g + 1) * POOL_GROUP)
        win = cs[:, POOL_STATE + 1:POOL_STATE + 1 + L, ch] - cs[:, POOL_STATE + 1 - w:POOL_STATE + 1 - w + L, ch]
        means.append(win / jnp.minimum(float(w), n_avail)[None, :, None])
    d = (jnp.concatenate(means, axis=-1) - ef[:, POOL_STATE:]).reshape(bsz, L, len(POOL_WINDOWS), POOL_GROUP)
    d = jnp.einsum('btgc,gce->btge', d, w_grp.astype(jnp.float32)).reshape(bsz, L, BRANCH_W)
    return (d * scale.astype(jnp.float32)).astype(u.dtype), ext[:, L:]


def merge(x, p, dsa_o, diff_o, ssd_o, pool_o, w_branch, w_out):
    silu = jax.nn.silu
    br = jnp.stack([dsa_o * silu(p['dg']), diff_o * silu(p['fg']), ssd_o, pool_o * silu(p['pg'])], axis=2)
    up = jnp.einsum('btnc,ncd->btnd', br, w_branch)
    m = jnp.sum(jax.nn.sigmoid(p['mg']) * up, axis=2)
    return x + jnp.einsum('btd,de->bte', m.astype(x.dtype), w_out)


def setup_inputs(seed: int = 0) -> dict:
    key = jax.random.key(seed)
    ks = jax.random.split(key, 32)
    n_pages = PAST_LEN // PAGE_SIZE
    n_used = DEC_BATCH * n_pages
    n_pool = n_used + n_used // 4
    def nrm(k, shape, scale=1.0):
        return scale * jax.random.normal(k, shape, jnp.float32)
    def gain(k, shape):
        return 1.0 + 0.02 * jax.random.normal(k, shape, jnp.float32)
    page_table = jax.random.permutation(ks[10], n_pool)[:n_used].reshape(DEC_BATCH, n_pages).astype(jnp.int32)
    dt0 = jnp.exp(jax.random.uniform(ks[20], (DEPTH, SSD_HEADS), jnp.float32, math.log(1e-3), math.log(1e-1)))
    return {
        'x_prompt': nrm(ks[0], (BATCH, SEQ, D_MODEL)),
        'x_sample': nrm(ks[1], (DEC_BATCH, DEC_SEQ, D_MODEL)),
        'cache_dsa_k': nrm(ks[2], (DEPTH, n_pool, PAGE_SIZE, DSA_HEADS, DSA_HEAD_DIM)),
        'cache_dsa_v': nrm(ks[3], (DEPTH, n_pool, PAGE_SIZE, DSA_HEADS, DSA_HEAD_DIM)),
        'cache_idx_k': nrm(ks[4], (DEPTH, n_pool, PAGE_SIZE, IDX_DIM)),
        'cache_diff_k': nrm(ks[5], (DEPTH, n_pool, PAGE_SIZE, 2 * DIFF_HEADS, DIFF_HEAD_DIM)),
        'cache_diff_v': nrm(ks[6], (DEPTH, n_pool, PAGE_SIZE, DIFF_HEADS, DIFF_V_DIM)),
        'state_ssm': nrm(ks[7], (DEPTH, DEC_BATCH, SSD_HEADS, SSD_HEAD_DIM, SSD_STATE), 0.1),
        'state_conv': nrm(ks[8], (DEPTH, DEC_BATCH, SSD_CONV - 1, SSD_CONV_DIM)),
        'state_pool': nrm(ks[9], (DEPTH, DEC_BATCH, POOL_STATE, BRANCH_W)),
        'page_table': page_table,
        'norm_g': gain(ks[11], (DEPTH, D_MODEL)),
        'w_in': nrm(ks[12], (DEPTH, D_MODEL, IN_TOTAL), D_MODEL ** -0.5),
        'dsa_qk_g': gain(ks[13], (DEPTH, 2, DSA_HEAD_DIM)),
        'diff_qk_g': gain(ks[14], (DEPTH, 2, DIFF_HEAD_DIM)),
        'diff_lam': nrm(ks[15], (DEPTH, 4, DIFF_HEAD_DIM), 0.1),
        'diff_subln': gain(ks[16], (DEPTH, DIFF_V_DIM)),
        'ssd_conv_w': nrm(ks[17], (DEPTH, SSD_CONV, SSD_CONV_DIM), SSD_CONV ** -0.5),
        'ssd_conv_b': nrm(ks[18], (DEPTH, SSD_CONV_DIM), 0.02),
        'ssd_dt_bias': dt0 + jnp.log(-jnp.expm1(-dt0)),
        'ssd_a_log': jnp.log(jax.random.uniform(ks[21], (DEPTH, SSD_HEADS), jnp.float32, 1.0, 16.0)),
        'ssd_d': gain(ks[22], (DEPTH, SSD_HEADS)),
        'ssd_norm': gain(ks[23], (DEPTH, BRANCH_W)),
        'pool_w': nrm(ks[24], (DEPTH, len(POOL_WINDOWS), POOL_GROUP, POOL_GROUP), POOL_GROUP ** -0.5),
        'pool_scale': 1.0 + 0.1 * jax.random.normal(ks[25], (DEPTH, BRANCH_W), jnp.float32),
        'w_branch': nrm(ks[26], (DEPTH, N_BRANCH, BRANCH_W, D_MODEL), BRANCH_W ** -0.5),
        'w_out': nrm(ks[27], (DEPTH, D_MODEL, D_MODEL), D_MODEL ** -0.5),
    }


def reference(x_prompt, x_sample, cache_dsa_k, cache_dsa_v, cache_idx_k, cache_diff_k, cache_diff_v,
              state_ssm, state_conv, state_pool, page_table,
              norm_g, w_in, dsa_qk_g, diff_qk_g, diff_lam, diff_subln,
              ssd_conv_w, ssd_conv_b, ssd_dt_bias, ssd_a_log, ssd_d, ssd_norm,
              pool_w, pool_scale, w_branch, w_out):
    bp, S, _ = x_prompt.shape
    bs, T, _ = x_sample.shape
    past = page_table.shape[1] * PAGE_SIZE
    pos_p = jnp.arange(S, dtype=jnp.int32)
    pos_s = past + jnp.arange(T, dtype=jnp.int32)
    k_pos_s = jnp.arange(past + T, dtype=jnp.int32)
    names = ('dk', 'dv', 'ik', 'fk', 'fv', 'ssm', 'conv', 'pool')
    acc_p = {n: [] for n in names}
    acc_s = {n: [] for n in names}
    xp, xs = x_prompt, x_sample
    for l in range(DEPTH):
        lam_init = 0.8 - 0.6 * math.exp(-0.3 * l)
        lq1, lk1, lq2, lk2 = diff_lam[l].astype(jnp.float32)
        lam = jnp.exp(jnp.sum(lq1 * lk1)) - jnp.exp(jnp.sum(lq2 * lk2)) + lam_init
        ssd_args = (ssd_conv_w[l], ssd_conv_b[l], ssd_dt_bias[l], ssd_a_log[l], ssd_d[l], ssd_norm[l])

        pp = project(xp, pos_p, norm_g[l], w_in[l], dsa_qk_g[l], diff_qk_g[l])
        dsa_o = dsa_prompt(pp, pos_p)
        diff_o = diff_prompt(pp, pos_p, lam, lam_init, diff_subln[l])
        ssd_o, ssm_new, conv_new = ssd_mix(
            pp['sxbc'], pp['sz'], pp['sdt'],
            jnp.zeros((bp, SSD_CONV - 1, SSD_CONV_DIM), xp.dtype),
            jnp.zeros((bp, SSD_HEADS, SSD_HEAD_DIM, SSD_STATE), jnp.float32), *ssd_args)
        pool_o, pool_new = pool_mix(pp['pu'], jnp.zeros((bp, POOL_STATE, BRANCH_W), xp.dtype), 0,
                                    pool_w[l], pool_scale[l])
        xp = merge(xp, pp, dsa_o, diff_o, ssd_o, pool_o, w_branch[l], w_out[l])
        for n, v in zip(names, (pp['dk'], pp['dv'], pp['ik'], pp['fk'], pp['fv'], ssm_new, conv_new, pool_new)):
            acc_p[n].append(v)

        ps = project(xs, pos_s, norm_g[l], w_in[l], dsa_qk_g[l], diff_qk_g[l])
        dsa_o = dsa_sample(ps, pos_s, k_pos_s, cache_dsa_k, cache_dsa_v, cache_idx_k, l, page_table)
        diff_o = diff_sample(ps, pos_s, k_pos_s, cache_diff_k, cache_diff_v, l, page_table,
                             lam, lam_init, diff_subln[l])
        ssd_o, ssm_new, conv_new = ssd_mix(ps['sxbc'], ps['sz'], ps['sdt'], state_conv[l], state_ssm[l], *ssd_args)
        pool_o, pool_new = pool_mix(ps['pu'], state_pool[l], past, pool_w[l], pool_scale[l])
        xs = merge(xs, ps, dsa_o, diff_o, ssd_o, pool_o, w_branch[l], w_out[l])
        for n, v in zip(names, (ps['dk'], ps['dv'], ps['ik'], ps['fk'], ps['fv'], ssm_new, conv_new, pool_new)):
            acc_s[n].append(v)

    dsa_k_p = jnp.stack(acc_p['dk'])
    dsa_v_p = jnp.stack(acc_p['dv'])
    idx_k_p = jnp.stack(acc_p['ik'])
    diff_k_p = jnp.stack(acc_p['fk'])
    diff_v_p = jnp.stack(acc_p['fv'])
    ssm_p = jnp.stack(acc_p['ssm'])
    conv_p = jnp.stack(acc_p['conv'])
    pool_p = jnp.stack(acc_p['pool'])
    dsa_k_s = jnp.stack(acc_s['dk'])
    dsa_v_s = jnp.stack(acc_s['dv'])
    idx_k_s = jnp.stack(acc_s['ik'])
    diff_k_s = jnp.stack(acc_s['fk'])
    diff_v_s = jnp.stack(acc_s['fv'])
    ssm_s = jnp.stack(acc_s['ssm'])
    conv_s = jnp.stack(acc_s['conv'])
    pool_s = jnp.stack(acc_s['pool'])
    return (xp, xs, dsa_k_p, dsa_v_p, idx_k_p, diff_k_p, diff_v_p, ssm_p, conv_p, pool_p,
            dsa_k_s, dsa_v_s, idx_k_s, diff_k_s, diff_v_s, ssm_s, conv_s, pool_s)
```

```python
import functools
import math

import jax
import jax.numpy as jnp
from jax import lax
from jax.experimental import pallas as pl
from jax.experimental.pallas import tpu as pltpu

F32 = jnp.float32
BF16 = jnp.bfloat16
I32 = jnp.int32

BRANCH_W = 256
DSA_HEADS = 4
DSA_HEAD_DIM = 64
IDX_HEADS = 4
IDX_DIM = 64
IDX_TOPK_MAX = 256
DIFF_HEADS = 4
DIFF_HEAD_DIM = 32
DIFF_V_DIM = 64
SSD_HEADS = 4
SSD_HEAD_DIM = 64
SSD_STATE = 64
SSD_GROUPS = 2
SSD_CONV = 4
SSD_CHUNK = 128
SSD_CONV_DIM = 512
POOL_WINDOWS = (2, 4, 8, 16)
POOL_STATE = 15
PAGE_SIZE = 128
ROPE_THETA = 10000.0
EPS = 1e-6
N_BRANCH = 4

LANES = 128
SUBLANES = 8
VMEM_LIMIT = 56 * 1024 * 1024

NEG = -1e30
INT_MIN = -(2 ** 31)
HIGHEST = lax.Precision.HIGHEST

_SLABS = (("dq", 256), ("dk", 256), ("dv", 256), ("iq", 256), ("fq", 256), ("fk", 256), ("fv", 256),
          ("gates", 1024), ("sxbc", 512), ("pu", 256), ("misc", 128), ("dts", 128))
_OFF = {}
_o = 0
for _n, _w in _SLABS:
    _OFF[_n] = (_o, _o + _w)
    _o += _w
W_PACKED = _o


def _cparams(n_axes, vmem=VMEM_LIMIT):
    return pltpu.CompilerParams(dimension_semantics=("arbitrary",) * n_axes, vmem_limit_bytes=vmem)


def _nk_dot(a, b):
    return lax.dot_general(a, b, (((1,), (1,)), ((), ())), preferred_element_type=F32)


def _silu(x):
    return x * (1.0 / (1.0 + jnp.exp(-x)))


def _sigmoid(x):
    return 1.0 / (1.0 + jnp.exp(-x))


def _group_of_lane(shape, log2_width):
    return lax.broadcasted_iota(I32, shape, len(shape) - 1) >> log2_width


def _proj_kernel(x_ref, ng_ref, w_ref, t64_ref, t32_ref, tmisc_ref, gains_ref,
                 dq_ref, dk_ref, dkb_ref, dv_ref, dvb_ref, iq_ref, misc_ref, miscb_ref,
                 fq_ref, fk_ref, fkb_ref, fv_ref, fvb_ref, gates_ref, sxbc_ref, pu_ref, dts_ref):
    x = x_ref[...]
    h = x * lax.rsqrt(jnp.mean(x * x, axis=-1, keepdims=True) + EPS) * ng_ref[...]
    hb = h.astype(BF16)

    def mm(name):
        a, b = _OFF[name]
        return jnp.dot(hb, w_ref[:, a:b], preferred_element_type=F32)

    def head_norm(z, g, log2_hd):
        n = z.shape[1]
        r = lax.broadcasted_iota(I32, (n, n), 0) >> log2_hd
        c = lax.broadcasted_iota(I32, (n, n), 1) >> log2_hd
        bd = jnp.where(r == c, 1.0 / (1 << log2_hd), 0.0).astype(F32)
        ms = jnp.dot(z * z, bd, preferred_element_type=F32, precision=HIGHEST)
        return z * lax.rsqrt(ms + EPS) * g

    def rope(z, tab_ref, half):
        n = z.shape[1]
        cos = tab_ref[:, :n]
        sin = tab_ref[:, n:]
        lane = lax.broadcasted_iota(I32, z.shape, 1)
        first = (lane & (2 * half - 1)) < half
        partner = jnp.where(first, pltpu.roll(z, n - half, 1), pltpu.roll(z, half, 1))
        return z * cos + partner * sin

    dq = rope(head_norm(mm("dq"), gains_ref[0:1, :], 6), t64_ref, 32)
    dq_ref[...] = dq.astype(BF16)
    dk = rope(head_norm(mm("dk"), gains_ref[1:2, :], 6), t64_ref, 32)
    dk_ref[...] = dk
    dkb_ref[...] = dk.astype(BF16)
    dv = mm("dv")
    dv_ref[...] = dv
    dvb_ref[...] = dv.astype(BF16)
    iq_ref[...] = rope(mm("iq"), t64_ref, 32).astype(BF16)
    misc = rope(mm("misc"), tmisc_ref, 32)
    misc_ref[...] = misc
    miscb_ref[...] = misc.astype(BF16)
    fq_ref[...] = rope(head_norm(mm("fq"), gains_ref[2:3, :], 5), t32_ref, 16).astype(BF16)
    fk = rope(head_norm(mm("fk"), gains_ref[3:4, :], 5), t32_ref, 16)
    fk_ref[...] = fk
    fkb_ref[...] = fk.astype(BF16)
    fv = mm("fv")
    fv_ref[...] = fv
    fvb_ref[...] = fv.astype(BF16)
    for j in range(4):
        a = _OFF["gates"][0] + 256 * j
        gates_ref[:, 256 * j:256 * (j + 1)] = jnp.dot(hb, w_ref[:, a:a + 256], preferred_element_type=F32)
    for j in range(2):
        a = _OFF["sxbc"][0] + 256 * j
        sxbc_ref[:, 256 * j:256 * (j + 1)] = jnp.dot(hb, w_ref[:, a:a + 256], preferred_element_type=F32)
    pu_ref[...] = mm("pu")
    dts_ref[...] = mm("dts")


_PROJ_OUTS = (("dq", 256, BF16), ("dk", 256, F32), ("dkb", 256, BF16), ("dv", 256, F32), ("dvb", 256, BF16),
              ("iq", 256, BF16), ("misc", 128, F32), ("miscb", 128, BF16), ("fq", 256, BF16), ("fk", 256, F32),
              ("fkb", 256, BF16), ("fv", 256, F32), ("fvb", 256, BF16), ("gates", 1024, F32),
              ("sxbc", 512, F32), ("pu", 256, F32), ("dts", 128, F32))


def _proj(x, ng, wp, t64, t32, tmisc, gains, tm, tab_blocks):
    n, d = x.shape
    row = lambda i: (i, 0)
    tab = lambda i: (i % tab_blocks, 0)
    const = lambda i: (0, 0)
    outs = _PROJ_OUTS
    res = pl.pallas_call(
        _proj_kernel,
        grid=(n // tm,),
        in_specs=[pl.BlockSpec((tm, d), row), pl.BlockSpec((1, d), const), pl.BlockSpec(wp.shape, const),
                  pl.BlockSpec((tm, 512), tab), pl.BlockSpec((tm, 512), tab), pl.BlockSpec((tm, 256), tab),
                  pl.BlockSpec((4, 256), const)],
        out_specs=[pl.BlockSpec((tm, w), row) for _, w, _ in outs],
        out_shape=[jax.ShapeDtypeStruct((n, w), dt) for _, w, dt in outs],
        compiler_params=_cparams(1),
        name="proj",
    )(x, ng, wp, t64, t32, tmisc, gains)
    return {name: r for (name, _, _), r in zip(outs, res)}


def _sortable(x):
    bits = pltpu.bitcast(x, I32)
    return bits ^ ((bits >> 31) & 0x7FFFFFFF)


def _online_update(s, m_ref, l_ref, alpha_ref, p_ref, r0, r1):
    m_old = m_ref[r0:r1, :]
    m_new = jnp.maximum(m_old, jnp.max(s, axis=1, keepdims=True))
    alpha = jnp.exp(m_old - m_new)
    p = jnp.exp(s - m_new)
    l_ref[r0:r1, :] = alpha * l_ref[r0:r1, :] + jnp.sum(p, axis=1, keepdims=True)
    m_ref[r0:r1, :] = m_new
    alpha_ref[r0:r1, :] = alpha
    p_ref[r0:r1, :] = p.astype(BF16)


def _diff_lambda(lam_ref, lam_init):
    lp = lam_ref[...]
    s1 = jnp.sum(lp[0:1, :] * lp[1:2, :], axis=1, keepdims=True)
    s2 = jnp.sum(lp[2:3, :] * lp[3:4, :], axis=1, keepdims=True)
    return jnp.exp(s1) - jnp.exp(s2) + lam_init


def _diff_finish(acc_ref, l_ref, rows, lam, lam_init, gsub):
    grp = _group_of_lane((rows, BRANCH_W), 6)
    out = jnp.zeros((rows, BRANCH_W), F32)
    for j in range(DIFF_HEADS):
        r1, r2 = 2 * j * rows, (2 * j + 1) * rows
        a1 = acc_ref[r1:r1 + rows, :] / l_ref[r1:r1 + rows, :]
        a2 = acc_ref[r2:r2 + rows, :] / l_ref[r2:r2 + rows, :]
        o = a1 - lam * a2
        ms = jnp.sum(jnp.where(grp == j, o * o, 0.0), axis=1, keepdims=True) * (1.0 / DIFF_V_DIM)
        out = out + jnp.where(grp == j, o * lax.rsqrt(ms + EPS), 0.0)
    return out * gsub * (1.0 - lam_init)


_DSA_C = 512


def _dsa_prompt_kernel(iq_ref, misc_ref, dq_ref, ik_ref, dk_ref, dv_ref, o_ref,
                       key_scr, qi_scr, qa_scr, jst_scr, m_scr, l_scr, alpha_scr, acc_scr, p_scr, *, k_top):
    tq = iq_ref.shape[0]
    seq = ik_ref.shape[0]
    c_sz = _DSA_C
    i = pl.program_id(1)
    nch = ((i + 1) * tq + c_sz - 1) // c_sz
    qpos = i * tq + lax.broadcasted_iota(I32, (tq, 1), 0)
    lane_c = lax.broadcasted_iota(I32, (tq, c_sz), 1)

    iq = iq_ref[...]
    dq = dq_ref[...]
    grp = _group_of_lane((tq, BRANCH_W), 6)
    for h in range(IDX_HEADS):
        qi_scr[h * tq:(h + 1) * tq, :] = iq[:, IDX_DIM * h:IDX_DIM * (h + 1)]
        qa_scr[h * tq:(h + 1) * tq, :] = jnp.where(grp == h, dq, jnp.zeros_like(dq))
    wts = misc_ref[:, 64:64 + IDX_HEADS]

    def p1(c, carry):
        start = pl.multiple_of(c * c_sz, c_sz)
        kc = ik_ref[pl.ds(start, c_sz), :][:, :IDX_DIM]
        s = _nk_dot(qi_scr[...], kc)
        acc = None
        for h in range(IDX_HEADS):
            t = wts[:, h:h + 1] * jnp.maximum(s[h * tq:(h + 1) * tq, :] * (IDX_DIM ** -0.5), 0.0)
            acc = t if acc is None else acc + t
        acc = jnp.where(acc == 0.0, 0.0, acc)
        key = jnp.where(c * c_sz + lane_c <= qpos, _sortable(acc), INT_MIN)
        key_scr[c] = key
        return carry

    lax.fori_loop(0, nch, p1, 0)

    def count(pred):
        def body(c, cnt):
            m = pred(key_scr[c], c)
            part = m[:, 0:LANES]
            for q in range(1, c_sz // LANES):
                part = part + m[:, q * LANES:(q + 1) * LANES]
            return cnt + part
        cnt = lax.fori_loop(0, nch, body, jnp.zeros((tq, LANES), F32))
        return jnp.sum(cnt, axis=1, keepdims=True)

    kf = float(k_top)
    c0 = count(lambda k, c: jnp.where(k >= 0, 1.0, 0.0))
    cand0 = jnp.where(c0 >= kf, 0, INT_MIN).astype(I32)

    def bit_body(b, cand):
        trial = cand | jnp.left_shift(jnp.int32(1), 30 - b)
        cnt = count(lambda k, c: jnp.where(k >= trial, 1.0, 0.0))
        return jnp.where(cnt >= kf, trial, cand)

    thr = lax.fori_loop(0, 31, bit_body, cand0)
    cnt_gt = count(lambda k, c: jnp.where(k > thr, 1.0, 0.0))
    cnt_ge = count(lambda k, c: jnp.where(k >= thr, 1.0, 0.0))
    need = jnp.where(cnt_ge > kf, jnp.where(thr > INT_MIN, 1.0, 0.0), 0.0)
    rem = kf - cnt_gt
    big = jnp.int32(2 ** 30)
    jst_scr[...] = jnp.full((tq, 1), big, I32)

    @pl.when(jnp.max(need) > 0.0)
    def _():
        nbits = max(1, (seq - 1).bit_length())

        def jb(b, pos):
            trial = pos | jnp.left_shift(jnp.int32(1), nbits - 1 - b)
            cnt = count(lambda k, c: jnp.where(k == thr, jnp.where(c * c_sz + lane_c < trial, 1.0, 0.0), 0.0))
            return jnp.where(cnt < rem, trial, pos)

        pos = lax.fori_loop(0, nbits, jb, jnp.zeros((tq, 1), I32))
        jst_scr[...] = jnp.where(need > 0.0, pos, big)

    rows = DSA_HEADS * tq
    m_scr[...] = jnp.full((rows, 1), NEG, F32)
    l_scr[...] = jnp.zeros((rows, 1), F32)
    acc_scr[...] = jnp.zeros((rows, BRANCH_W), F32)
    jst = jst_scr[...]

    def p3(c, carry):
        start = pl.multiple_of(c * c_sz, c_sz)
        k = key_scr[c]
        kpos = c * c_sz + lane_c
        sel = jnp.where(k > thr, 0.0, jnp.where(k == thr, jnp.where(kpos <= jst, 0.0, NEG), NEG))
        bias = jnp.where(kpos <= qpos, sel, NEG)
        s = _nk_dot(qa_scr[...], dk_ref[pl.ds(start, c_sz), :])
        for h in range(DSA_HEADS):
            sh = s[h * tq:(h + 1) * tq, :] * (DSA_HEAD_DIM ** -0.5) + bias
            _online_update(sh, m_scr, l_scr, alpha_scr, p_scr, h * tq, (h + 1) * tq)
        pv = jnp.dot(p_scr[...], dv_ref[pl.ds(start, c_sz), :], preferred_element_type=F32)
        acc_scr[...] = alpha_scr[...] * acc_scr[...] + pv
        return carry

    lax.fori_loop(0, nch, p3, 0)
    out = jnp.zeros((tq, BRANCH_W), F32)
    for h in range(DSA_HEADS):
        out = out + jnp.where(grp == h, acc_scr[h * tq:(h + 1) * tq, :] / l_scr[h * tq:(h + 1) * tq, :], 0.0)
    o_ref[...] = out


def _dsa_prompt(p, bp, seq):
    tq = 128
    nq = seq // tq
    k_top = min(IDX_TOPK_MAX, seq // 4)
    qrow = lambda b, i: (b * nq + i, 0)
    kv = lambda b, i: (b, 0)
    rows = DSA_HEADS * tq
    return pl.pallas_call(
        functools.partial(_dsa_prompt_kernel, k_top=k_top),
        grid=(bp, nq),
        in_specs=[pl.BlockSpec((tq, 256), qrow), pl.BlockSpec((tq, 128), qrow), pl.BlockSpec((tq, 256), qrow),
                  pl.BlockSpec((seq, 128), kv), pl.BlockSpec((seq, 256), kv), pl.BlockSpec((seq, 256), kv)],
        out_specs=pl.BlockSpec((tq, 256), qrow),
        out_shape=jax.ShapeDtypeStruct((bp * seq, BRANCH_W), F32),
        scratch_shapes=[pltpu.VMEM((seq // _DSA_C, tq, _DSA_C), I32), pltpu.VMEM((rows, IDX_DIM), BF16),
                        pltpu.VMEM((rows, 256), BF16), pltpu.VMEM((tq, 1), I32),
                        pltpu.VMEM((rows, 1), F32), pltpu.VMEM((rows, 1), F32), pltpu.VMEM((rows, 1), F32),
                        pltpu.VMEM((rows, 256), F32), pltpu.VMEM((rows, _DSA_C), BF16)],
        compiler_params=_cparams(2),
        name="dsa_prompt",
    )(p["iq"], p["misc"], p["dq"], p["miscb"], p["dkb"], p["dvb"])


_DIFF_C = 512


def _diff_prompt_kernel(fq_ref, fk_ref, fv_ref, lam_ref, gsub_ref, o_ref,
                        qa_scr, m_scr, l_scr, alpha_scr, acc_scr, p_scr, s_scr, *, lam_init):
    tq = fq_ref.shape[0]
    c_sz = _DIFF_C
    nh = 2 * DIFF_HEADS
    i = pl.program_id(1)
    nch = ((i + 1) * tq + c_sz - 1) // c_sz
    qpos = i * tq + lax.broadcasted_iota(I32, (tq, 1), 0)
    lane_c = lax.broadcasted_iota(I32, (tq, c_sz), 1)
    fq = fq_ref[...]
    grp = _group_of_lane((tq, BRANCH_W), 5)
    for h in range(nh):
        qa_scr[h * tq:(h + 1) * tq, :] = jnp.where(grp == h, fq, jnp.zeros_like(fq))
    rows = nh * tq
    m_scr[...] = jnp.full((rows, 1), NEG, F32)
    l_scr[...] = jnp.zeros((rows, 1), F32)
    acc_scr[...] = jnp.zeros((rows, BRANCH_W), F32)

    def body(c, carry):
        start = pl.multiple_of(c * c_sz, c_sz)
        s_scr[...] = _nk_dot(qa_scr[...], fk_ref[pl.ds(start, c_sz), :])
        bias = jnp.where(c * c_sz + lane_c <= qpos, 0.0, NEG)
        for h in range(nh):
            sh = s_scr[h * tq:(h + 1) * tq, :] * (DIFF_HEAD_DIM ** -0.5) + bias
            _online_update(sh, m_scr, l_scr, alpha_scr, p_scr, h * tq, (h + 1) * tq)
        pv = jnp.dot(p_scr[...], fv_ref[pl.ds(start, c_sz), :], preferred_element_type=F32)
        acc_scr[...] = alpha_scr[...] * acc_scr[...] + pv
        return carry

    lax.fori_loop(0, nch, body, 0)
    lam = _diff_lambda(lam_ref, lam_init)
    o_ref[...] = _diff_finish(acc_scr, l_scr, tq, lam, lam_init, gsub_ref[...])


def _diff_prompt(p, lam_p, gsub, lam_init, bp, seq):
    tq = 128
    nq = seq // tq
    qrow = lambda b, i: (b * nq + i, 0)
    kv = lambda b, i: (b, 0)
    const = lambda b, i: (0, 0)
    rows = 2 * DIFF_HEADS * tq
    return pl.pallas_call(
        functools.partial(_diff_prompt_kernel, lam_init=lam_init),
        grid=(bp, nq),
        in_specs=[pl.BlockSpec((tq, 256), qrow), pl.BlockSpec((seq, 256), kv), pl.BlockSpec((seq, 256), kv),
                  pl.BlockSpec((4, DIFF_HEAD_DIM), const), pl.BlockSpec((1, 256), const)],
        out_specs=pl.BlockSpec((tq, 256), qrow),
        out_shape=jax.ShapeDtypeStruct((bp * seq, BRANCH_W), F32),
        scratch_shapes=[pltpu.VMEM((rows, 256), BF16), pltpu.VMEM((rows, 1), F32), pltpu.VMEM((rows, 1), F32),
                        pltpu.VMEM((rows, 1), F32), pltpu.VMEM((rows, 256), F32), pltpu.VMEM((rows, _DIFF_C), BF16),
                        pltpu.VMEM((rows, _DIFF_C), F32)],
        compiler_params=_cparams(2),
        name="diff_prompt",
    )(p["fq"], p["fkb"], p["fvb"], lam_p, gsub)


def _ssd_kernel(par_ref, xbc_ref, dts_ref, z_ref, cprev_ref, hprev_ref, cw_ref, cb_ref, ng_ref,
                y_ref, hout_ref, cout_ref, e_scr, h_scr):
    qin = xbc_ref.shape[0]
    q = SSD_CHUNK
    c = pl.program_id(1)
    last = pl.num_programs(1) - 1

    @pl.when(c == 0)
    def _():
        e_scr[0:8, :] = cprev_ref[...]
        h_scr[...] = hprev_ref[...]
        if qin < q:
            e_scr[8 + qin:8 + q, :] = jnp.zeros((q - qin, SSD_CONV_DIM), F32)

    e_scr[8:8 + qin, :] = xbc_ref[...]
    conv = cb_ref[...] + jnp.zeros((q, SSD_CONV_DIM), F32)
    for k in range(SSD_CONV):
        conv = conv + e_scr[5 + k:5 + k + q, :] * cw_ref[k:k + 1, :]
    new_tail = e_scr[qin:qin + 8, :]
    u = _silu(conv)
    xs = u[:, :BRANCH_W]
    b_all = u[:, BRANCH_W:BRANCH_W + SSD_GROUPS * SSD_STATE]
    bm_t = b_all.T.astype(BF16)
    bm = b_all.astype(BF16)
    cm = u[:, BRANCH_W + SSD_GROUPS * SSD_STATE:].astype(BF16)

    dts = dts_ref[...]
    if qin < q:
        dts = jnp.concatenate([dts, jnp.zeros((q - qin, LANES), F32)], axis=0)
    rowi = lax.broadcasted_iota(I32, (q, LANES), 0)
    lanei = lax.broadcasted_iota(I32, (q, LANES), 1)
    bias_l = jnp.zeros((q, LANES), F32)
    alog_l = jnp.zeros((q, LANES), F32)
    for h in range(SSD_HEADS):
        bias_l = jnp.where(lanei == h, par_ref[h], bias_l)
        alog_l = jnp.where(lanei == h, par_ref[SSD_HEADS + h], alog_l)
    pre = dts + bias_l
    dt = jnp.maximum(pre, 0.0) + jnp.log1p(jnp.exp(-jnp.abs(pre)))
    dt = jnp.where(rowi < qin, jnp.where(lanei < SSD_HEADS, dt, 0.0), 0.0)
    adt = dt * (-jnp.exp(alog_l))
    r_qq = lax.broadcasted_iota(I32, (q, q), 0)
    c_qq = lax.broadcasted_iota(I32, (q, q), 1)
    causal = r_qq >= c_qq
    acs_col = jnp.dot(jnp.where(causal, 1.0, 0.0), adt, preferred_element_type=F32, precision=HIGHEST)
    acs_row = jnp.dot(adt.T[0:8, :], jnp.where(r_qq <= c_qq, 1.0, 0.0), preferred_element_type=F32,
                      precision=HIGHEST)

    grp = _group_of_lane((q, BRANCH_W), 6)
    grp1 = _group_of_lane((1, BRANCH_W), 6)
    dt_b = jnp.zeros((q, BRANCH_W), F32)
    a_b = jnp.zeros((q, BRANCH_W), F32)
    d_b = jnp.zeros((1, BRANCH_W), F32)
    for h in range(SSD_HEADS):
        dt_b = jnp.where(grp == h, dt[:, h:h + 1], dt_b)
        a_b = jnp.where(grp == h, acs_col[:, h:h + 1], a_b)
        d_b = jnp.where(grp1 == h, par_ref[2 * SSD_HEADS + h], d_b)
    xdt = xs * dt_b
    xdt_b = xdt.astype(BF16)
    a_last = a_b[q - 1:q, :]

    y_diag = jnp.zeros((q, BRANCH_W), F32)
    hpg = SSD_HEADS // SSD_GROUPS
    cbs = [_nk_dot(cm[:, 64 * g:64 * (g + 1)], bm[:, 64 * g:64 * (g + 1)]) for g in range(SSD_GROUPS)]
    for h in range(SSD_HEADS):
        lmat = jnp.exp(jnp.where(causal, acs_col[:, h:h + 1] - acs_row[h:h + 1, :], -jnp.inf))
        y_diag = y_diag + jnp.dot((cbs[h // hpg] * lmat).astype(BF16),
                                  jnp.where(grp == h, xdt_b, jnp.zeros_like(xdt_b)), preferred_element_type=F32)
    ht = h_scr[...]
    htb = ht.astype(BF16)
    xdec = (xdt * jnp.exp(a_last - a_b)).astype(BF16)
    st = [jnp.dot(bm_t[64 * g:64 * (g + 1), :], xdec, preferred_element_type=F32) for g in range(SSD_GROUPS)]
    yo = [jnp.dot(cm[:, 64 * g:64 * (g + 1)], htb, preferred_element_type=F32) for g in range(SSD_GROUPS)]
    grp_n = _group_of_lane((SSD_STATE, BRANCH_W), 6)
    h_scr[...] = ht * jnp.exp(a_last) + jnp.where(grp_n < hpg, st[0], st[1])
    y = y_diag + jnp.where(grp < hpg, yo[0], yo[1]) * jnp.exp(a_b) + xs * d_b
    gte = y * _silu(z_ref[...]) if qin == q else y[0:qin, :] * _silu(z_ref[...])
    half = BRANCH_W // SSD_GROUPS
    outs = []
    for g in range(SSD_GROUPS):
        gg = gte[:, half * g:half * (g + 1)]
        outs.append(gg * lax.rsqrt(jnp.mean(gg * gg, axis=-1, keepdims=True) + EPS))
    y_ref[...] = jnp.concatenate(outs, axis=1) * ng_ref[...]
    e_scr[0:8, :] = new_tail

    @pl.when(c == last)
    def _():
        hout_ref[...] = h_scr[...]
        cout_ref[...] = new_tail


def _ssd(p, par, cprev8, hprev, cw, cb, ng, nb, rows_per_b, qin):
    nc = rows_per_b // qin
    row = lambda b, c: (b * nc + c, 0)
    const = lambda b, c: (0, 0)
    y, hout, cout = pl.pallas_call(
        _ssd_kernel,
        grid=(nb, nc),
        in_specs=[pl.BlockSpec(memory_space=pltpu.SMEM),
                  pl.BlockSpec((qin, SSD_CONV_DIM), row), pl.BlockSpec((qin, LANES), row),
                  pl.BlockSpec((qin, 256), lambda b, c: (b * nc + c, 2)),
                  pl.BlockSpec((None, 8, SSD_CONV_DIM), lambda b, c: (b, 0, 0)),
                  pl.BlockSpec((None, SSD_STATE, BRANCH_W), lambda b, c: (b, 0, 0)),
                  pl.BlockSpec((SSD_CONV, SSD_CONV_DIM), const), pl.BlockSpec((1, SSD_CONV_DIM), const),
                  pl.BlockSpec((1, BRANCH_W), const)],
        out_specs=[pl.BlockSpec((qin, BRANCH_W), row),
                   pl.BlockSpec((None, SSD_STATE, BRANCH_W), lambda b, c: (b, 0, 0)),
                   pl.BlockSpec((None, 8, SSD_CONV_DIM), lambda b, c: (b, 0, 0))],
        out_shape=[jax.ShapeDtypeStruct((nb * rows_per_b, BRANCH_W), F32),
                   jax.ShapeDtypeStruct((nb, SSD_STATE, BRANCH_W), F32),
                   jax.ShapeDtypeStruct((nb, 8, SSD_CONV_DIM), F32)],
        scratch_shapes=[pltpu.VMEM((8 + SSD_CHUNK, SSD_CONV_DIM), F32),
                        pltpu.VMEM((SSD_STATE, BRANCH_W), F32)],
        compiler_params=_cparams(2),
        name="ssd",
    )(par, p["sxbc"], p["dts"], p["gates"], cprev8, _state_to_lanes(hprev), cw, cb, ng)
    return y, _state_from_lanes(hout), cout[:, 8 - (SSD_CONV - 1):, :]


def _state_to_lanes(h):
    b = h.shape[0]
    return h.reshape(b, SSD_HEADS * SSD_HEAD_DIM, SSD_STATE).transpose(0, 2, 1)


def _state_from_lanes(ht):
    b = ht.shape[0]
    return ht.transpose(0, 2, 1).reshape(b, SSD_HEADS, SSD_HEAD_DIM, SSD_STATE)


def _pool_kernel(u_ref, halo_ref, w_ref, scale_ref, o_ref, hout_ref, e_scr, *, start_pos):
    t = u_ref.shape[0]
    c = pl.program_id(1)
    last = pl.num_programs(1) - 1

    @pl.when(c == 0)
    def _():
        e_scr[0:8, :] = jnp.zeros((8, BRANCH_W), F32)
        e_scr[8:24, :] = halo_ref[...]

    u = u_ref[...]
    e_scr[24:24 + t, :] = u
    new_halo = e_scr[8 + t:24 + t, :]
    n = 16 + t
    cur = e_scr[8:8 + n, :]
    stages = []
    for k in (1, 2, 4, 8):
        cur = cur + e_scr[8 - k:8 - k + n, :]
        stages.append(cur[16:, :])
        e_scr[8:8 + n, :] = cur
    grp = _group_of_lane((t, BRANCH_W), 6)
    win = jnp.where(grp == 0, stages[0], jnp.where(grp == 1, stages[1], jnp.where(grp == 2, stages[2], stages[3])))
    wlen = jnp.where(grp == 0, 2.0, jnp.where(grp == 1, 4.0, jnp.where(grp == 2, 8.0, 16.0)))
    n_avail = (start_pos + c * t + 1 + lax.broadcasted_iota(I32, (t, BRANCH_W), 0)).astype(F32)
    d = win / jnp.minimum(wlen, n_avail) - u
    o_ref[...] = jnp.dot(d.astype(BF16), w_ref[...], preferred_element_type=F32) * scale_ref[...]
    e_scr[8:24, :] = new_halo

    @pl.when(c == last)
    def _():
        hout_ref[...] = new_halo


def _pool(u, halo16, wbd, scale, nb, rows_per_b, t, start_pos):
    nc = rows_per_b // t
    row = lambda b, c: (b * nc + c, 0)
    const = lambda b, c: (0, 0)
    o, hout = pl.pallas_call(
        functools.partial(_pool_kernel, start_pos=start_pos),
        grid=(nb, nc),
        in_specs=[pl.BlockSpec((t, BRANCH_W), row), pl.BlockSpec((None, 16, BRANCH_W), lambda b, c: (b, 0, 0)),
                  pl.BlockSpec((BRANCH_W, BRANCH_W), const), pl.BlockSpec((1, BRANCH_W), const)],
        out_specs=[pl.BlockSpec((t, BRANCH_W), row), pl.BlockSpec((None, 16, BRANCH_W), lambda b, c: (b, 0, 0))],
        out_shape=[jax.ShapeDtypeStruct((nb * rows_per_b, BRANCH_W), F32),
                   jax.ShapeDtypeStruct((nb, 16, BRANCH_W), F32)],
        scratch_shapes=[pltpu.VMEM((24 + t, BRANCH_W), F32)],
        compiler_params=_cparams(2),
        name="pool",
    )(u, halo16, wbd, scale)
    return o, hout[:, 1:, :]


def _merge_kernel(x_ref, ng_ref, wmg_ref, dsa_ref, diff_ref, ssd_ref, pool_ref, gates_ref, wbr_ref, wout_ref, y_ref):
    x = x_ref[...]
    h = x * lax.rsqrt(jnp.mean(x * x, axis=-1, keepdims=True) + EPS) * ng_ref[...]
    hb = h.astype(BF16)
    d = x.shape[1]
    br = (dsa_ref[...] * _silu(gates_ref[:, 0:256]),
          diff_ref[...] * _silu(gates_ref[:, 256:512]),
          ssd_ref[...],
          pool_ref[...] * _silu(gates_ref[:, 768:1024]))
    m = jnp.zeros(x.shape, F32)
    for n in range(N_BRANCH):
        mg = jnp.dot(hb, wmg_ref[:, n * d:(n + 1) * d], preferred_element_type=F32)
        up = jnp.dot(br[n].astype(BF16), wbr_ref[n], preferred_element_type=F32)
        m = m + _sigmoid(mg) * up
    y_ref[...] = x + jnp.dot(m.astype(BF16), wout_ref[...], preferred_element_type=F32)


def _merge(x, ng, wmg, dsa_o, diff_o, ssd_o, pool_o, gates, wbr, wout, tm):
    n, d = x.shape
    row = lambda i: (i, 0)
    const = lambda i: (0, 0)
    return pl.pallas_call(
        _merge_kernel,
        grid=(n // tm,),
        in_specs=[pl.BlockSpec((tm, d), row), pl.BlockSpec((1, d), const), pl.BlockSpec(wmg.shape, const),
                  pl.BlockSpec((tm, 256), row), pl.BlockSpec((tm, 256), row), pl.BlockSpec((tm, 256), row),
                  pl.BlockSpec((tm, 256), row), pl.BlockSpec((tm, 1024), row),
                  pl.BlockSpec(wbr.shape, lambda i: (0, 0, 0)), pl.BlockSpec(wout.shape, const)],
        out_specs=pl.BlockSpec((tm, d), row),
        out_shape=jax.ShapeDtypeStruct((n, d), F32),
        compiler_params=_cparams(1),
        name="merge",
    )(x, ng, wmg, dsa_o, diff_o, ssd_o, pool_o, gates, wbr, wout)


_PAGES_PER_STEP = 8


def _page_specs(layer, block_cols, npp):
    def spec(k):
        return pl.BlockSpec((None, None, PAGE_SIZE, block_cols),
                            lambda b, g, pt, k=k: (layer, pt[b, g * npp + k], 0, 0))
    return [spec(k) for k in range(npp)]


def _dsa_sidx_kernel(pt_ref, iq_ref, misc_ref, *rest, npp, n_pages, k_top):
    pages = rest[:npp]
    bias_ref = rest[npp]
    key_scr, qi_scr, jst_scr = rest[npp + 1:]
    t = iq_ref.shape[0]
    g = pl.program_id(1)
    last = pl.num_programs(1) - 1
    wts = misc_ref[:, 64:64 + IDX_HEADS]

    def scores(kb):
        s = _nk_dot(qi_scr[...], kb)
        acc = None
        for h in range(IDX_HEADS):
            v = wts[:, h:h + 1] * jnp.maximum(s[h * t:(h + 1) * t, :] * (IDX_DIM ** -0.5), 0.0)
            acc = v if acc is None else acc + v
        return jnp.where(acc == 0.0, 0.0, acc)

    @pl.when(g == 0)
    def _():
        iq = iq_ref[...]
        for h in range(IDX_HEADS):
            qi_scr[h * t:(h + 1) * t, :] = iq[:, IDX_DIM * h:IDX_DIM * (h + 1)]
        knew = jnp.concatenate([misc_ref[:, :IDX_DIM], jnp.zeros((PAGE_SIZE - t, IDX_DIM), F32)], axis=0)
        sc = scores(knew.astype(BF16))
        causal = lax.broadcasted_iota(I32, (t, LANES), 1) <= lax.broadcasted_iota(I32, (t, LANES), 0)
        key_scr[n_pages] = jnp.where(causal, _sortable(sc), INT_MIN)

    for k in range(npp):
        key_scr[g * npp + k] = _sortable(scores(pages[k][...].astype(BF16)))

    @pl.when(g == last)
    def _():
        keys = key_scr[...]
        kpos = (lax.broadcasted_iota(I32, keys.shape, 0) * PAGE_SIZE + lax.broadcasted_iota(I32, keys.shape, 2))

        def count(m):
            return jnp.sum(jnp.sum(m, axis=0), axis=1, keepdims=True)

        kf = float(k_top)
        c0 = count(jnp.where(keys >= 0, 1.0, 0.0))
        cand0 = jnp.where(c0 >= kf, 0, INT_MIN).astype(I32)

        def bit_body(b, cand):
            trial = cand | jnp.left_shift(jnp.int32(1), 30 - b)
            cnt = count(jnp.where(keys >= trial[None], 1.0, 0.0))
            return jnp.where(cnt >= kf, trial, cand)

        thr = lax.fori_loop(0, 31, bit_body, cand0)
        cnt_gt = count(jnp.where(keys > thr[None], 1.0, 0.0))
        cnt_ge = count(jnp.where(keys >= thr[None], 1.0, 0.0))
        need = jnp.where(cnt_ge > kf, jnp.where(thr > INT_MIN, 1.0, 0.0), 0.0)
        rem = kf - cnt_gt
        big = jnp.int32(2 ** 30)
        jst_scr[...] = jnp.full((t, 1), big, I32)

        @pl.when(jnp.max(need) > 0.0)
        def _():
            nbits = ((n_pages + 1) * PAGE_SIZE - 1).bit_length()

            def jb(b, pos):
                trial = pos | jnp.left_shift(jnp.int32(1), nbits - 1 - b)
                cnt = count(jnp.where(keys == thr[None], jnp.where(kpos < trial[None], 1.0, 0.0), 0.0))
                return jnp.where(cnt < rem, trial, pos)

            pos = lax.fori_loop(0, nbits, jb, jnp.zeros((t, 1), I32))
            jst_scr[...] = jnp.where(need > 0.0, pos, big)

        jst = jst_scr[...]
        sel = jnp.where(keys > thr[None], 0.0,
                        jnp.where(keys == thr[None], jnp.where(kpos <= jst[None], 0.0, NEG), NEG))
        newc = lax.broadcasted_iota(I32, keys.shape, 0) == n_pages
        causal = lax.broadcasted_iota(I32, keys.shape, 2) <= lax.broadcasted_iota(I32, keys.shape, 1)
        bias_ref[...] = jnp.where(newc, jnp.where(causal, sel, NEG), sel)


def _dsa_sidx(ps, cache_ik, layer, page_table, bs, t):
    n_pages = page_table.shape[1]
    npp = _PAGES_PER_STEP
    k_top = min(IDX_TOPK_MAX, (n_pages * PAGE_SIZE + t) // 4)
    row = lambda b, g, pt: (b, 0)
    grid_spec = pltpu.PrefetchScalarGridSpec(
        num_scalar_prefetch=1, grid=(bs, n_pages // npp),
        in_specs=[pl.BlockSpec((t, 256), row), pl.BlockSpec((t, 128), row)] + _page_specs(layer, IDX_DIM, npp),
        out_specs=pl.BlockSpec((None, n_pages + 1, t, LANES), lambda b, g, pt: (b, 0, 0, 0)),
        scratch_shapes=[pltpu.VMEM((n_pages + 1, t, LANES), I32), pltpu.VMEM((IDX_HEADS * t, IDX_DIM), BF16),
                        pltpu.VMEM((t, 1), I32)])
    return pl.pallas_call(
        functools.partial(_dsa_sidx_kernel, npp=npp, n_pages=n_pages, k_top=k_top),
        grid_spec=grid_spec,
        out_shape=jax.ShapeDtypeStruct((bs, n_pages + 1, t, LANES), F32),
        compiler_params=_cparams(2),
        name="dsa_sample_index",
    )(page_table, ps["iq"], ps["misc"], *([cache_ik] * npp))


def _paged_attn_kernel(pt_ref, q_ref, kn_ref, vn_ref, *rest, npp, nheads, log2_hd, scale, use_bias, finish):
    idx = 0
    if use_bias:
        bias_new_ref, bias_pg_ref = rest[0], rest[1]
        idx = 2
    kpages = rest[idx:idx + npp]
    vpages = rest[idx + npp:idx + 2 * npp]
    extra = rest[idx + 2 * npp:-6]
    o_ref = rest[-6]
    qa_scr, m_scr, l_scr, alpha_scr, acc_scr = rest[-5:]
    t = q_ref.shape[0]
    rows = nheads * t
    g = pl.program_id(1)
    last = pl.num_programs(1) - 1

    def tile_rows(b):
        return jnp.concatenate([b] * nheads, axis=0)

    def step(s, vb):
        m_old = m_scr[...]
        m_new = jnp.maximum(m_old, jnp.max(s, axis=1, keepdims=True))
        alpha = jnp.exp(m_old - m_new)
        p = jnp.exp(s - m_new)
        l_scr[...] = alpha * l_scr[...] + jnp.sum(p, axis=1, keepdims=True)
        m_scr[...] = m_new
        acc_scr[...] = alpha * acc_scr[...] + jnp.dot(p.astype(BF16), vb, preferred_element_type=F32)

    @pl.when(g == 0)
    def _():
        qv = q_ref[...]
        grp = _group_of_lane((t, BRANCH_W), log2_hd)
        for h in range(nheads):
            qa_scr[h * t:(h + 1) * t, :] = jnp.where(grp == h, qv, jnp.zeros_like(qv))
        m_scr[...] = jnp.full((rows, 1), NEG, F32)
        l_scr[...] = jnp.zeros((rows, 1), F32)
        acc_scr[...] = jnp.zeros((rows, BRANCH_W), F32)
        pad = jnp.zeros((PAGE_SIZE - t, BRANCH_W), BF16)
        kn = jnp.concatenate([kn_ref[...], pad], axis=0)
        vn = jnp.concatenate([vn_ref[...], pad], axis=0)
        if use_bias:
            bias = bias_new_ref[0]
        else:
            causal = lax.broadcasted_iota(I32, (t, LANES), 1) <= lax.broadcasted_iota(I32, (t, LANES), 0)
            bias = jnp.where(causal, 0.0, NEG)
        step(_nk_dot(qa_scr[...], kn) * scale + tile_rows(bias), vn)

    kb = jnp.concatenate([kp[...].astype(BF16) for kp in kpages], axis=0)
    vb = jnp.concatenate([vp[...].astype(BF16) for vp in vpages], axis=0)
    s = _nk_dot(qa_scr[...], kb) * scale
    if use_bias:
        s = s + tile_rows(jnp.concatenate([bias_pg_ref[k] for k in range(npp)], axis=1))
    step(s, vb)

    @pl.when(g == last)
    def _():
        o_ref[...] = finish(acc_scr, l_scr, t, *extra)


def _dsa_finish(acc_ref, l_ref, t):
    grp = _group_of_lane((t, BRANCH_W), 6)
    out = jnp.zeros((t, BRANCH_W), F32)
    for h in range(DSA_HEADS):
        out = out + jnp.where(grp == h, acc_ref[h * t:(h + 1) * t, :] / l_ref[h * t:(h + 1) * t, :], 0.0)
    return out


def _diff_sample_finish(acc_ref, l_ref, t, lam_ref, gsub_ref, *, lam_init):
    return _diff_finish(acc_ref, l_ref, t, _diff_lambda(lam_ref, lam_init), lam_init, gsub_ref[...])


def _paged_attn(q, kn, vn, cache_k, cache_v, layer, page_table, bs, t, *, nheads, log2_hd, scale, bias=None,
                extra=(), finish, name):
    n_pages = page_table.shape[1]
    npp = _PAGES_PER_STEP
    row = lambda b, g, pt: (b, 0)
    const = lambda b, g, pt: (0, 0)
    in_specs = [pl.BlockSpec((t, 256), row), pl.BlockSpec((t, 256), row), pl.BlockSpec((t, 256), row)]
    args = [q, kn, vn]
    if bias is not None:
        in_specs += [pl.BlockSpec((None, 1, t, LANES), lambda b, g, pt: (b, n_pages, 0, 0)),
                     pl.BlockSpec((None, npp, t, LANES), lambda b, g, pt: (b, g, 0, 0))]
        args += [bias, bias]
    in_specs += _page_specs(layer, 256, npp) + _page_specs(layer, 256, npp)
    args += [cache_k] * npp + [cache_v] * npp
    for e in extra:
        in_specs.append(pl.BlockSpec(e.shape, const))
        args.append(e)
    rows = nheads * t
    grid_spec = pltpu.PrefetchScalarGridSpec(
        num_scalar_prefetch=1, grid=(bs, n_pages // npp), in_specs=in_specs,
        out_specs=pl.BlockSpec((t, 256), row),
        scratch_shapes=[pltpu.VMEM((rows, 256), BF16), pltpu.VMEM((rows, 1), F32), pltpu.VMEM((rows, 1), F32),
                        pltpu.VMEM((rows, 1), F32), pltpu.VMEM((rows, 256), F32)])
    return pl.pallas_call(
        functools.partial(_paged_attn_kernel, npp=npp, nheads=nheads, log2_hd=log2_hd, scale=scale,
                          use_bias=bias is not None, finish=finish),
        grid_spec=grid_spec,
        out_shape=jax.ShapeDtypeStruct((bs * t, BRANCH_W), F32),
        compiler_params=_cparams(2),
        name=name,
    )(page_table, *args)


def _rope_table(pos, half, reps, extra_cos=None):
    inv = ROPE_THETA ** (-jnp.arange(half, dtype=F32) / half)
    ang = pos.astype(F32)[:, None] * inv[None, :]
    cos, sin = jnp.cos(ang), jnp.sin(ang)
    cos_h = jnp.concatenate([cos, cos], axis=1)
    sin_h = jnp.concatenate([-sin, sin], axis=1)
    cos_t, sin_t = jnp.tile(cos_h, (1, reps)), jnp.tile(sin_h, (1, reps))
    if extra_cos is not None:
        n = pos.shape[0]
        cos_t = jnp.concatenate([cos_t, jnp.broadcast_to(extra_cos[None, :], (n, extra_cos.shape[0]))], axis=1)
        sin_t = jnp.concatenate([sin_t, jnp.zeros((n, extra_cos.shape[0]), F32)], axis=1)
    return jnp.concatenate([cos_t, sin_t], axis=1)


def _tables(pos):
    misc_scale = jnp.concatenate([jnp.full((IDX_HEADS,), IDX_HEADS ** -0.5, F32),
                                  jnp.ones((LANES - IDX_DIM - IDX_HEADS,), F32)])
    return (_rope_table(pos, DSA_HEAD_DIM // 2, 4), _rope_table(pos, DIFF_HEAD_DIM // 2, 8),
            _rope_table(pos, IDX_DIM // 2, 1, misc_scale))


def _pack_w_in(w):
    sizes = (256, 256, 256, 256, 256, 64, 4, 256, 256, 256, 256, 256, 512, 4, 256, 256, 4096)
    offs = [0]
    for s in sizes:
        offs.append(offs[-1] + s)
    (dq, dk, dv, dg, iq, ik, iw, fq, fk, fv, fg, sz, sxbc, sdt, pu, pg, mg) = [
        w[:, offs[i]:offs[i + 1]] for i in range(len(sizes))]
    d = w.shape[0]
    misc = jnp.concatenate([ik, iw, jnp.zeros((d, LANES - IDX_DIM - IDX_HEADS), w.dtype)], axis=1)
    dts = jnp.concatenate([sdt, jnp.zeros((d, LANES - SSD_HEADS), w.dtype)], axis=1)
    packed = jnp.concatenate([dq, dk, dv, iq, fq, fk, fv, dg, fg, sz, pg, sxbc, pu, misc, dts], axis=1)
    return packed.astype(BF16), mg.astype(BF16)


def _block_diag(w):
    g, n, _ = w.shape
    out = jnp.zeros((g * n, g * n), w.dtype)
    for i in range(g):
        out = out.at[i * n:(i + 1) * n, i * n:(i + 1) * n].set(w[i])
    return out


def kernel(x_prompt, x_sample, cache_dsa_k, cache_dsa_v, cache_idx_k, cache_diff_k, cache_diff_v, state_ssm,
           state_conv, state_pool, page_table, norm_g, w_in, dsa_qk_g, diff_qk_g, diff_lam, diff_subln,
           ssd_conv_w, ssd_conv_b, ssd_dt_bias, ssd_a_log, ssd_d, ssd_norm, pool_w, pool_scale, w_branch, w_out):
    bp, seq, d = x_prompt.shape
    bs, t, _ = x_sample.shape
    depth = norm_g.shape[0]
    n_pages = page_table.shape[1]
    past = n_pages * PAGE_SIZE
    n_pool = cache_dsa_k.shape[1]

    tab_p = _tables(jnp.arange(seq, dtype=I32))
    tab_s = tuple(jnp.tile(a, (bs, 1)) for a in _tables(past + jnp.arange(t, dtype=I32)))
    ck = cache_dsa_k.reshape(depth, n_pool, PAGE_SIZE, BRANCH_W)
    cv = cache_dsa_v.reshape(depth, n_pool, PAGE_SIZE, BRANCH_W)
    cfk = cache_diff_k.reshape(depth, n_pool, PAGE_SIZE, BRANCH_W)
    cfv = cache_diff_v.reshape(depth, n_pool, PAGE_SIZE, BRANCH_W)

    tm = 256
    tm_s = min(tm, bs * t)
    pool_t = min(512, seq)
    xp = x_prompt.reshape(bp * seq, d)
    xs = x_sample.reshape(bs * t, d)
    acc_p = {n: [] for n in ("dk", "dv", "ik", "fk", "fv", "ssm", "conv", "pool")}
    acc_s = {n: [] for n in acc_p}

    for l in range(depth):
        lam_init = 0.8 - 0.6 * math.exp(-0.3 * l)
        wp, wmg = _pack_w_in(w_in[l])
        ng = norm_g[l].reshape(1, d)
        gains = jnp.stack([jnp.tile(dsa_qk_g[l, 0], 4), jnp.tile(dsa_qk_g[l, 1], 4),
                           jnp.tile(diff_qk_g[l, 0], 8), jnp.tile(diff_qk_g[l, 1], 8)])
        gsub = jnp.tile(diff_subln[l], 4).reshape(1, BRANCH_W)
        lam_p = diff_lam[l]
        ssd_par = jnp.concatenate([ssd_dt_bias[l], ssd_a_log[l], ssd_d[l]]).astype(F32)
        cw, cb = ssd_conv_w[l], ssd_conv_b[l].reshape(1, SSD_CONV_DIM)
        sng = ssd_norm[l].reshape(1, BRANCH_W)
        wbd = _block_diag(pool_w[l]).astype(BF16)
        pscale = pool_scale[l].reshape(1, BRANCH_W)
        wbr = w_branch[l].astype(BF16)
        wout = w_out[l].astype(BF16)

        pp = _proj(xp, ng, wp, *tab_p, gains, tm, seq // tm)
        dsa_o = _dsa_prompt(pp, bp, seq)
        diff_o = _diff_prompt(pp, lam_p, gsub, lam_init, bp, seq)
        ssd_o, ssm_new, conv_new = _ssd(pp, ssd_par, jnp.zeros((bp, 8, SSD_CONV_DIM), F32),
                                        jnp.zeros((bp, SSD_HEADS, SSD_HEAD_DIM, SSD_STATE), F32), cw, cb, sng,
                                        bp, seq, SSD_CHUNK)
        pool_o, pool_new = _pool(pp["pu"], jnp.zeros((bp, 16, BRANCH_W), F32), wbd, pscale, bp, seq, pool_t, 0)
        xp = _merge(xp, ng, wmg, dsa_o, diff_o, ssd_o, pool_o, pp["gates"], wbr, wout, tm)
        for n, v in zip(acc_p, (pp["dk"].reshape(bp, seq, DSA_HEADS, DSA_HEAD_DIM),
                                pp["dv"].reshape(bp, seq, DSA_HEADS, DSA_HEAD_DIM),
                                pp["misc"][:, :IDX_DIM].reshape(bp, seq, IDX_DIM),
                                pp["fk"].reshape(bp, seq, 2 * DIFF_HEADS, DIFF_HEAD_DIM),
                                pp["fv"].reshape(bp, seq, DIFF_HEADS, DIFF_V_DIM),
                                ssm_new, conv_new, pool_new)):
            acc_p[n].append(v)

        ps = _proj(xs, ng, wp, *tab_s, gains, tm_s, (bs * t) // tm_s)
        bias = _dsa_sidx(ps, cache_idx_k, l, page_table, bs, t)
        dsa_o = _paged_attn(ps["dq"], ps["dkb"], ps["dvb"], ck, cv, l, page_table, bs, t,
                            nheads=DSA_HEADS, log2_hd=6, scale=DSA_HEAD_DIM ** -0.5, bias=bias,
                            finish=_dsa_finish, name="dsa_sample_attn")
        diff_o = _paged_attn(ps["fq"], ps["fkb"], ps["fvb"], cfk, cfv, l, page_table, bs, t,
                             nheads=2 * DIFF_HEADS, log2_hd=5, scale=DIFF_HEAD_DIM ** -0.5,
                             extra=(lam_p, gsub), finish=functools.partial(_diff_sample_finish, lam_init=lam_init),
                             name="diff_sample_attn")
        cprev8 = jnp.concatenate([jnp.zeros((bs, 8 - (SSD_CONV - 1), SSD_CONV_DIM), F32), state_conv[l]], axis=1)
        ssd_o, ssm_new, conv_new = _ssd(ps, ssd_par, cprev8, state_ssm[l], cw, cb, sng, bs, t, t)
        halo = jnp.concatenate([jnp.zeros((bs, 1, BRANCH_W), F32), state_pool[l]], axis=1)
        pool_o, pool_new = _pool(ps["pu"], halo, wbd, pscale, bs, t, t, past)
        xs = _merge(xs, ng, wmg, dsa_o, diff_o, ssd_o, pool_o, ps["gates"], wbr, wout, tm_s)
        for n, v in zip(acc_s, (ps["dk"].reshape(bs, t, DSA_HEADS, DSA_HEAD_DIM),
                                ps["dv"].reshape(bs, t, DSA_HEADS, DSA_HEAD_DIM),
                                ps["misc"][:, :IDX_DIM].reshape(bs, t, IDX_DIM),
                                ps["fk"].reshape(bs, t, 2 * DIFF_HEADS, DIFF_HEAD_DIM),
                                ps["fv"].reshape(bs, t, DIFF_HEADS, DIFF_V_DIM),
                                ssm_new, conv_new, pool_new)):
            acc_s[n].append(v)

    names = ("dk", "dv", "ik", "fk", "fv", "ssm", "conv", "pool")
    return ((xp.reshape(bp, seq, d), xs.reshape(bs, t, d))
            + tuple(jnp.stack(acc_p[n]) for n in names) + tuple(jnp.stack(acc_s[n]) for n in names))
```

```python
import functools
import math

import jax
import jax.numpy as jnp
from jax import lax
from jax.experimental import pallas as pl
from jax.experimental.pallas import tpu as pltpu

F32 = jnp.float32
BF16 = jnp.bfloat16
I32 = jnp.int32

BRANCH_W = 256
DSA_HEADS = 4
DSA_HEAD_DIM = 64
IDX_HEADS = 4
IDX_DIM = 64
IDX_TOPK_MAX = 256
DIFF_HEADS = 4
DIFF_HEAD_DIM = 32
DIFF_V_DIM = 64
SSD_HEADS = 4
SSD_HEAD_DIM = 64
SSD_STATE = 64
SSD_GROUPS = 2
SSD_CONV = 4
SSD_CHUNK = 128
SSD_CONV_DIM = 512
POOL_WINDOWS = (2, 4, 8, 16)
POOL_STATE = 15
PAGE_SIZE = 128
ROPE_THETA = 10000.0
EPS = 1e-6
N_BRANCH = 4

LANES = 128
SUBLANES = 8
VMEM_LIMIT = 56 * 1024 * 1024

LOG2E = math.log2(math.e)
NEG = -1e30
INT_MIN = -(2 ** 31)
HIGHEST = lax.Precision.HIGHEST

_SLABS = (("dq", 256), ("dk", 256), ("dv", 256), ("iq", 256), ("fq", 256), ("fk", 256), ("fv", 256),
          ("gates", 1024), ("sxbc", 512), ("pu", 256), ("misc", 128), ("dts", 128))
_OFF = {}
_o = 0
for _n, _w in _SLABS:
    _OFF[_n] = (_o, _o + _w)
    _o += _w
W_PACKED = _o


def _cparams(n_axes, vmem=VMEM_LIMIT):
    return pltpu.CompilerParams(dimension_semantics=("arbitrary",) * n_axes, vmem_limit_bytes=vmem)


def _nk_dot(a, b):
    return lax.dot_general(a, b, (((1,), (1,)), ((), ())), preferred_element_type=F32)


def _silu(x):
    return x * (1.0 / (1.0 + jnp.exp(-x)))


def _sigmoid(x):
    return 1.0 / (1.0 + jnp.exp(-x))


def _group_of_lane(shape, log2_width):
    return lax.broadcasted_iota(I32, shape, len(shape) - 1) >> log2_width


def _proj_kernel(x_ref, ng_ref, w_ref, t64_ref, t32_ref, tmisc_ref, gains_ref,
                 dq_ref, dk_ref, dv_ref, dvb_ref, iq_ref, misc_ref, fq_ref, fk_ref, fv_ref, fvb_ref,
                 gates_ref, sxbc_ref, pu_ref, dts_ref, dkt_ref, ikt_ref, fkt_ref):
    x = x_ref[...]
    h = x * lax.rsqrt(jnp.mean(x * x, axis=-1, keepdims=True) + EPS) * ng_ref[...]
    hb = h.astype(BF16)

    def mm(name):
        a, b = _OFF[name]
        return jnp.dot(hb, w_ref[:, a:b], preferred_element_type=F32)

    def head_norm(z, g, log2_hd):
        n = z.shape[1]
        r = lax.broadcasted_iota(I32, (n, n), 0) >> log2_hd
        c = lax.broadcasted_iota(I32, (n, n), 1) >> log2_hd
        bd = jnp.where(r == c, 1.0 / (1 << log2_hd), 0.0).astype(F32)
        ms = jnp.dot(z * z, bd, preferred_element_type=F32, precision=HIGHEST)
        return z * lax.rsqrt(ms + EPS) * g

    def rope(z, tab_ref, half):
        n = z.shape[1]
        cos = tab_ref[:, :n]
        sin = tab_ref[:, n:]
        lane = lax.broadcasted_iota(I32, z.shape, 1)
        first = (lane & (2 * half - 1)) < half
        partner = jnp.where(first, pltpu.roll(z, n - half, 1), pltpu.roll(z, half, 1))
        return z * cos + partner * sin

    dq = rope(head_norm(mm("dq"), gains_ref[0:1, :], 6), t64_ref, 32)
    dq_ref[...] = (dq * (DSA_HEAD_DIM ** -0.5 * LOG2E)).astype(BF16)
    dk = rope(head_norm(mm("dk"), gains_ref[1:2, :], 6), t64_ref, 32)
    dk_ref[...] = dk
    dkt_ref[...] = dk.T.astype(BF16)
    dv = mm("dv")
    dv_ref[...] = dv
    dvb_ref[...] = dv.astype(BF16)
    iq_ref[...] = (rope(mm("iq"), t64_ref, 32) * (IDX_DIM ** -0.5)).astype(BF16)
    misc = rope(mm("misc"), tmisc_ref, 32)
    misc_ref[...] = misc
    ikt_ref[...] = misc.T[0:IDX_DIM, :].astype(BF16)
    fq = rope(head_norm(mm("fq"), gains_ref[2:3, :], 5), t32_ref, 16)
    fq_ref[...] = (fq * (DIFF_HEAD_DIM ** -0.5 * LOG2E)).astype(BF16)
    fk = rope(head_norm(mm("fk"), gains_ref[3:4, :], 5), t32_ref, 16)
    fk_ref[...] = fk
    fkt_ref[...] = fk.T.astype(BF16)
    fv = mm("fv")
    fv_ref[...] = fv
    fvb_ref[...] = fv.astype(BF16)
    for j in range(4):
        a = _OFF["gates"][0] + 256 * j
        gates_ref[:, 256 * j:256 * (j + 1)] = jnp.dot(hb, w_ref[:, a:a + 256], preferred_element_type=F32)
    for j in range(2):
        a = _OFF["sxbc"][0] + 256 * j
        sxbc_ref[:, 256 * j:256 * (j + 1)] = jnp.dot(hb, w_ref[:, a:a + 256], preferred_element_type=F32)
    pu_ref[...] = mm("pu")
    dts_ref[...] = mm("dts")


_PROJ_OUTS = (("dq", 256, BF16), ("dk", 256, F32), ("dv", 256, F32), ("dvb", 256, BF16), ("iq", 256, BF16),
              ("misc", 128, F32), ("fq", 256, BF16), ("fk", 256, F32), ("fv", 256, F32), ("fvb", 256, BF16),
              ("gates", 1024, F32), ("sxbc", 512, F32), ("pu", 256, F32), ("dts", 128, F32))
_PROJ_T_OUTS = (("dkT", 256), ("ikT", IDX_DIM), ("fkT", 256))


def _proj(x, ng, wp, t64, t32, tmisc, gains, tm, tab_blocks):
    n, d = x.shape
    row = lambda i: (i, 0)
    tab = lambda i: (i % tab_blocks, 0)
    const = lambda i: (0, 0)
    outs = _PROJ_OUTS
    res = pl.pallas_call(
        _proj_kernel,
        grid=(n // tm,),
        in_specs=[pl.BlockSpec((tm, d), row), pl.BlockSpec((1, d), const), pl.BlockSpec(wp.shape, const),
                  pl.BlockSpec((tm, 512), tab), pl.BlockSpec((tm, 512), tab), pl.BlockSpec((tm, 256), tab),
                  pl.BlockSpec((4, 256), const)],
        out_specs=([pl.BlockSpec((tm, w), row) for _, w, _ in outs]
                   + [pl.BlockSpec((None, w, tm), lambda i: (i, 0, 0)) for _, w in _PROJ_T_OUTS]),
        out_shape=([jax.ShapeDtypeStruct((n, w), dt) for _, w, dt in outs]
                   + [jax.ShapeDtypeStruct((n // tm, w, tm), BF16) for _, w in _PROJ_T_OUTS]),
        compiler_params=_cparams(1),
        name="proj",
    )(x, ng, wp, t64, t32, tmisc, gains)
    return dict(zip([o[0] for o in outs] + [o[0] for o in _PROJ_T_OUTS], res))


def _sortable(x):
    bits = pltpu.bitcast(x, I32)
    return bits ^ ((bits >> 31) & 0x7FFFFFFF)


def _flash_rows(s_ref, bias, m_ref, l_ref, alpha_ref, p_ref, r0, r1):
    c_sz = s_ref.shape[1]
    nq = c_sz // LANES
    parts = []
    for q in range(nq):
        v = s_ref[r0:r1, q * LANES:(q + 1) * LANES]
        parts.append(v if bias is None else v + bias[:, q * LANES:(q + 1) * LANES])
    mx = parts[0]
    for q in range(1, nq):
        mx = jnp.maximum(mx, parts[q])
    m_old = m_ref[r0:r1, :]
    m_new = jnp.maximum(m_old, jnp.max(mx, axis=1, keepdims=True))
    alpha = jnp.exp2(m_old - m_new)
    m_ref[r0:r1, :] = m_new
    alpha_ref[r0:r1, :] = alpha
    lsum = None
    for q in range(nq):
        p = jnp.exp2(parts[q] - m_new)
        p_ref[r0:r1, q * LANES:(q + 1) * LANES] = p.astype(BF16)
        lsum = p if lsum is None else lsum + p
    l_ref[r0:r1, :] = alpha * l_ref[r0:r1, :] + lsum


def _flash_loop(n, qa_ref, kt_ref, v_ref, bias_fn, nh, tq, s_scr, p_scr, alpha_scr, m_scr, l_scr, acc_scr):
    c_sz = kt_ref.shape[2]
    rows = nh * tq
    m_scr[...] = jnp.full((rows, LANES), NEG, F32)
    l_scr[...] = jnp.zeros((rows, LANES), F32)
    acc_scr[...] = jnp.zeros((rows, BRANCH_W), F32)

    def scores(c, slot):
        s_scr[slot] = jnp.dot(qa_ref[...], kt_ref[c], preferred_element_type=F32)

    def softmax(c, slot):
        bias = bias_fn(c)
        for h in range(nh):
            _flash_rows(s_scr.at[slot], bias, m_scr, l_scr, alpha_scr.at[slot], p_scr.at[slot],
                        h * tq, (h + 1) * tq)

    def values(c, slot):
        start = pl.multiple_of(c * c_sz, c_sz)
        pv = jnp.dot(p_scr[slot], v_ref[pl.ds(start, c_sz), :], preferred_element_type=F32)
        alpha = alpha_scr[slot]
        acc_scr[...] = jnp.concatenate([alpha, alpha], axis=1) * acc_scr[...] + pv

    scores(0, 0)
    scores(jnp.minimum(1, n - 1), 1)
    softmax(0, 0)

    def body(j, carry):
        c = 2 * j + 1
        scores(jnp.minimum(c + 1, n - 1), 0)
        values(c - 1, 0)
        softmax(c, 1)
        scores(jnp.minimum(c + 2, n - 1), 1)
        values(c, 1)
        softmax(c + 1, 0)
        return carry

    lax.fori_loop(0, (n - 1) // 2, body, 0)

    @pl.when((n - 1) % 2 == 1)
    def _():
        values(n - 2, 0)
        softmax(n - 1, 1)

    values(n - 1, (n - 1) & 1)


def _diff_lambda(lam_ref, lam_init):
    lp = lam_ref[...]
    s1 = jnp.sum(lp[0:1, :] * lp[1:2, :], axis=1, keepdims=True)
    s2 = jnp.sum(lp[2:3, :] * lp[3:4, :], axis=1, keepdims=True)
    return jnp.exp(s1) - jnp.exp(s2) + lam_init


def _diff_finish(acc_ref, l_ref, rows, lam, lam_init, gsub):
    grp = _group_of_lane((rows, BRANCH_W), 6)
    out = jnp.zeros((rows, BRANCH_W), F32)
    for j in range(DIFF_HEADS):
        r1, r2 = 2 * j * rows, (2 * j + 1) * rows
        a1 = acc_ref[r1:r1 + rows, :] / jnp.sum(l_ref[r1:r1 + rows, :], axis=1, keepdims=True)
        a2 = acc_ref[r2:r2 + rows, :] / jnp.sum(l_ref[r2:r2 + rows, :], axis=1, keepdims=True)
        o = a1 - lam * a2
        ms = jnp.sum(jnp.where(grp == j, o * o, 0.0), axis=1, keepdims=True) * (1.0 / DIFF_V_DIM)
        out = out + jnp.where(grp == j, o * lax.rsqrt(ms + EPS), 0.0)
    return out * gsub * (1.0 - lam_init)


def _dsa_prompt_kernel(iq_ref, misc_ref, dq_ref, ik_ref, dk_ref, dv_ref, o_ref,
                       key_scr, qi_scr, qa_scr, m_scr, l_scr, alpha_scr, acc_scr, p_scr, s_scr, *, k_top):
    tq = iq_ref.shape[0]
    c_sz = ik_ref.shape[2]
    seq = ik_ref.shape[0] * c_sz
    i = pl.program_id(1)
    nch = ((i + 1) * tq + c_sz - 1) // c_sz
    qpos = i * tq + lax.broadcasted_iota(I32, (tq, 1), 0)
    lane_c = lax.broadcasted_iota(I32, (tq, c_sz), 1)

    iq = iq_ref[...]
    dq = dq_ref[...]
    grp = _group_of_lane((tq, BRANCH_W), 6)
    for h in range(IDX_HEADS):
        qi_scr[h * tq:(h + 1) * tq, :] = iq[:, IDX_DIM * h:IDX_DIM * (h + 1)]
        qa_scr[h * tq:(h + 1) * tq, :] = jnp.where(grp == h, dq, jnp.zeros_like(dq))
    wts = misc_ref[:, 64:64 + IDX_HEADS]

    def p1(c, carry):
        s_scr[0] = jnp.dot(qi_scr[...], ik_ref[c], preferred_element_type=F32)
        acc = None
        for h in range(IDX_HEADS):
            t = wts[:, h:h + 1] * jnp.maximum(s_scr[0, h * tq:(h + 1) * tq, :], 0.0)
            acc = t if acc is None else acc + t
        acc = jnp.where(acc == 0.0, 0.0, acc)
        key_scr[c] = jnp.where(c * c_sz + lane_c <= qpos, _sortable(acc), INT_MIN)
        return carry

    lax.fori_loop(0, nch, p1, 0)

    def count(pred):
        def body(c, cnt):
            m = pred(key_scr[c], c)
            part = m[:, 0:LANES]
            for q in range(1, c_sz // LANES):
                part = part + m[:, q * LANES:(q + 1) * LANES]
            return cnt + part
        cnt = lax.fori_loop(0, nch, body, jnp.zeros((tq, LANES), F32))
        return jnp.sum(cnt, axis=1, keepdims=True)

    kf = float(k_top)
    c0 = count(lambda k, c: jnp.where(k >= 0, 1.0, 0.0))
    cand0 = jnp.where(c0 >= kf, 0, INT_MIN).astype(I32)

    def bit_body(b, cand):
        trial = cand | jnp.left_shift(jnp.int32(1), 30 - b)
        cnt = count(lambda k, c: jnp.where(k >= trial, 1.0, 0.0))
        return jnp.where(cnt >= kf, trial, cand)

    thr = lax.fori_loop(0, 31, bit_body, cand0)
    cnt_gt = count(lambda k, c: jnp.where(k > thr, 1.0, 0.0))
    cnt_ge = count(lambda k, c: jnp.where(k >= thr, 1.0, 0.0))
    need = jnp.where(cnt_ge > kf, jnp.where(thr > INT_MIN, 1.0, 0.0), 0.0)
    rem = kf - cnt_gt

    @pl.when(jnp.max(need) > 0.0)
    def _():
        nbits = max(1, (seq - 1).bit_length())

        def jb(b, pos):
            trial = pos | jnp.left_shift(jnp.int32(1), nbits - 1 - b)
            cnt = count(lambda k, c: jnp.where(k == thr, jnp.where(c * c_sz + lane_c < trial, 1.0, 0.0), 0.0))
            return jnp.where(cnt < rem, trial, pos)

        pos = lax.fori_loop(0, nbits, jb, jnp.zeros((tq, 1), I32))
        cut = jnp.where(need > 0.0, pos, jnp.int32(2 ** 30))

        def demote(c, carry):
            k = key_scr[c]
            key_scr[c] = jnp.where(k == thr, jnp.where(c * c_sz + lane_c > cut, INT_MIN, k), k)
            return carry

        lax.fori_loop(0, nch, demote, 0)

    thr_sel = jnp.where(thr == INT_MIN, INT_MIN + 1, thr)

    def sel_bias(c):
        return jnp.where(key_scr[c] >= thr_sel, 0.0, NEG)

    _flash_loop(nch, qa_scr, dk_ref, dv_ref, sel_bias, DSA_HEADS, tq, s_scr, p_scr, alpha_scr, m_scr, l_scr, acc_scr)
    o_ref[...] = _dsa_finish(acc_scr, l_scr, tq)


def _dsa_prompt(p, bp, seq, tq):
    c_sz = p["dkT"].shape[2]
    nq = seq // tq
    k_top = min(IDX_TOPK_MAX, seq // 4)
    qrow = lambda b, i: (b * nq + i, 0)
    kv = lambda b, i: (b, 0)
    kvt = lambda b, i: (b, 0, 0)
    rows = DSA_HEADS * tq
    return pl.pallas_call(
        functools.partial(_dsa_prompt_kernel, k_top=k_top),
        grid=(bp, nq),
        in_specs=[pl.BlockSpec((tq, 256), qrow), pl.BlockSpec((tq, 128), qrow), pl.BlockSpec((tq, 256), qrow),
                  pl.BlockSpec((seq // c_sz, IDX_DIM, c_sz), kvt), pl.BlockSpec((seq // c_sz, 256, c_sz), kvt),
                  pl.BlockSpec((seq, 256), kv)],
        out_specs=pl.BlockSpec((tq, 256), qrow),
        out_shape=jax.ShapeDtypeStruct((bp * seq, BRANCH_W), F32),
        scratch_shapes=[pltpu.VMEM((seq // c_sz, tq, c_sz), I32), pltpu.VMEM((rows, IDX_DIM), BF16),
                        pltpu.VMEM((rows, 256), BF16), pltpu.VMEM((rows, LANES), F32),
                        pltpu.VMEM((rows, LANES), F32), pltpu.VMEM((2, rows, LANES), F32),
                        pltpu.VMEM((rows, 256), F32), pltpu.VMEM((2, rows, c_sz), BF16),
                        pltpu.VMEM((2, rows, c_sz), F32)],
        compiler_params=_cparams(2),
        name="dsa_prompt",
    )(p["iq"], p["misc"], p["dq"], p["ikT"], p["dkT"], p["dvb"])


def _diff_prompt_kernel(fq_ref, fk_ref, fv_ref, lam_ref, gsub_ref, o_ref,
                        qa_scr, m_scr, l_scr, alpha_scr, acc_scr, p_scr, s_scr, *, lam_init):
    tq = fq_ref.shape[0]
    c_sz = fk_ref.shape[2]
    nh = 2 * DIFF_HEADS
    i = pl.program_id(1)
    n_full = (i * tq + 1) // c_sz
    qpos = i * tq + lax.broadcasted_iota(I32, (tq, 1), 0)
    lane_c = lax.broadcasted_iota(I32, (tq, c_sz), 1)
    fq = fq_ref[...]
    grp = _group_of_lane((tq, BRANCH_W), 5)
    for h in range(nh):
        qa_scr[h * tq:(h + 1) * tq, :] = jnp.where(grp == h, fq, jnp.zeros_like(fq))

    def causal_bias(c):
        return jnp.where(c * c_sz + lane_c <= qpos, 0.0, NEG)

    _flash_loop(n_full + 1, qa_scr, fk_ref, fv_ref, causal_bias, nh, tq,
                s_scr, p_scr, alpha_scr, m_scr, l_scr, acc_scr)
    lam = _diff_lambda(lam_ref, lam_init)
    o_ref[...] = _diff_finish(acc_scr, l_scr, tq, lam, lam_init, gsub_ref[...])


def _diff_prompt(p, lam_p, gsub, lam_init, bp, seq, tq):
    c_sz = p["fkT"].shape[2]
    nq = seq // tq
    qrow = lambda b, i: (b * nq + i, 0)
    kv = lambda b, i: (b, 0)
    const = lambda b, i: (0, 0)
    rows = 2 * DIFF_HEADS * tq
    return pl.pallas_call(
        functools.partial(_diff_prompt_kernel, lam_init=lam_init),
        grid=(bp, nq),
        in_specs=[pl.BlockSpec((tq, 256), qrow),
                  pl.BlockSpec((seq // c_sz, 256, c_sz), lambda b, i: (b, 0, 0)), pl.BlockSpec((seq, 256), kv),
                  pl.BlockSpec((4, DIFF_HEAD_DIM), const), pl.BlockSpec((1, 256), const)],
        out_specs=pl.BlockSpec((tq, 256), qrow),
        out_shape=jax.ShapeDtypeStruct((bp * seq, BRANCH_W), F32),
        scratch_shapes=[pltpu.VMEM((rows, 256), BF16), pltpu.VMEM((rows, LANES), F32),
                        pltpu.VMEM((rows, LANES), F32), pltpu.VMEM((2, rows, LANES), F32),
                        pltpu.VMEM((rows, 256), F32), pltpu.VMEM((2, rows, c_sz), BF16),
                        pltpu.VMEM((2, rows, c_sz), F32)],
        compiler_params=_cparams(2),
        name="diff_prompt",
    )(p["fq"], p["fkT"], p["fvb"], lam_p, gsub)


def _ssd_kernel(par_ref, xbc_ref, dts_ref, z_ref, cprev_ref, hprev_ref, cw_ref, cb_ref, ng_ref,
                y_ref, hout_ref, cout_ref, e_scr, h_scr):
    qin = xbc_ref.shape[0]
    q = SSD_CHUNK
    c = pl.program_id(1)
    last = pl.num_programs(1) - 1

    @pl.when(c == 0)
    def _():
        e_scr[0:8, :] = cprev_ref[...]
        h_scr[...] = hprev_ref[...]
        if qin < q:
            e_scr[8 + qin:8 + q, :] = jnp.zeros((q - qin, SSD_CONV_DIM), F32)

    e_scr[8:8 + qin, :] = xbc_ref[...]
    conv = cb_ref[...] + jnp.zeros((q, SSD_CONV_DIM), F32)
    for k in range(SSD_CONV):
        conv = conv + e_scr[5 + k:5 + k + q, :] * cw_ref[k:k + 1, :]
    new_tail = e_scr[qin:qin + 8, :]
    u = _silu(conv)
    xs = u[:, :BRANCH_W]
    b_all = u[:, BRANCH_W:BRANCH_W + SSD_GROUPS * SSD_STATE]
    bm_t = b_all.T.astype(BF16)
    bm = b_all.astype(BF16)
    cm = u[:, BRANCH_W + SSD_GROUPS * SSD_STATE:].astype(BF16)

    dts = dts_ref[...]
    if qin < q:
        dts = jnp.concatenate([dts, jnp.zeros((q - qin, LANES), F32)], axis=0)
    rowi = lax.broadcasted_iota(I32, (q, LANES), 0)
    lanei = lax.broadcasted_iota(I32, (q, LANES), 1)
    bias_l = jnp.zeros((q, LANES), F32)
    alog_l = jnp.zeros((q, LANES), F32)
    for h in range(SSD_HEADS):
        bias_l = jnp.where(lanei == h, par_ref[h], bias_l)
        alog_l = jnp.where(lanei == h, par_ref[SSD_HEADS + h], alog_l)
    pre = dts + bias_l
    dt = jnp.maximum(pre, 0.0) + jnp.log1p(jnp.exp(-jnp.abs(pre)))
    dt = jnp.where(rowi < qin, jnp.where(lanei < SSD_HEADS, dt, 0.0), 0.0)
    adt = dt * (-jnp.exp(alog_l))
    r_qq = lax.broadcasted_iota(I32, (q, q), 0)
    c_qq = lax.broadcasted_iota(I32, (q, q), 1)
    causal = r_qq >= c_qq
    acs_col = jnp.dot(jnp.where(causal, 1.0, 0.0), adt, preferred_element_type=F32, precision=HIGHEST)
    acs_row = jnp.dot(adt.T[0:8, :], jnp.where(r_qq <= c_qq, 1.0, 0.0), preferred_element_type=F32,
                      precision=HIGHEST)

    grp = _group_of_lane((q, BRANCH_W), 6)
    grp1 = _group_of_lane((1, BRANCH_W), 6)
    dt_b = jnp.zeros((q, BRANCH_W), F32)
    a_b = jnp.zeros((q, BRANCH_W), F32)
    d_b = jnp.zeros((1, BRANCH_W), F32)
    for h in range(SSD_HEADS):
        dt_b = jnp.where(grp == h, dt[:, h:h + 1], dt_b)
        a_b = jnp.where(grp == h, acs_col[:, h:h + 1], a_b)
        d_b = jnp.where(grp1 == h, par_ref[2 * SSD_HEADS + h], d_b)
    xdt = xs * dt_b
    xdt_b = xdt.astype(BF16)
    a_last = a_b[q - 1:q, :]

    y_diag = jnp.zeros((q, BRANCH_W), F32)
    hpg = SSD_HEADS // SSD_GROUPS
    cbs = [_nk_dot(cm[:, 64 * g:64 * (g + 1)], bm[:, 64 * g:64 * (g + 1)]) for g in range(SSD_GROUPS)]
    for h in range(SSD_HEADS):
        lmat = jnp.exp(jnp.where(causal, acs_col[:, h:h + 1] - acs_row[h:h + 1, :], -jnp.inf))
        y_diag = y_diag + jnp.dot((cbs[h // hpg] * lmat).astype(BF16),
                                  jnp.where(grp == h, xdt_b, jnp.zeros_like(xdt_b)), preferred_element_type=F32)
    ht = h_scr[...]
    htb = ht.astype(BF16)
    xdec = (xdt * jnp.exp(a_last - a_b)).astype(BF16)
    st = [jnp.dot(bm_t[64 * g:64 * (g + 1), :], xdec, preferred_element_type=F32) for g in range(SSD_GROUPS)]
    yo = [jnp.dot(cm[:, 64 * g:64 * (g + 1)], htb, preferred_element_type=F32) for g in range(SSD_GROUPS)]
    grp_n = _group_of_lane((SSD_STATE, BRANCH_W), 6)
    h_scr[...] = ht * jnp.exp(a_last) + jnp.where(grp_n < hpg, st[0], st[1])
    y = y_diag + jnp.where(grp < hpg, yo[0], yo[1]) * jnp.exp(a_b) + xs * d_b
    gte = y * _silu(z_ref[...]) if qin == q else y[0:qin, :] * _silu(z_ref[...])
    half = BRANCH_W // SSD_GROUPS
    outs = []
    for g in range(SSD_GROUPS):
        gg = gte[:, half * g:half * (g + 1)]
        outs.append(gg * lax.rsqrt(jnp.mean(gg * gg, axis=-1, keepdims=True) + EPS))
    y_ref[...] = jnp.concatenate(outs, axis=1) * ng_ref[...]
    e_scr[0:8, :] = new_tail

    @pl.when(c == last)
    def _():
        hout_ref[...] = h_scr[...]
        cout_ref[...] = new_tail


def _ssd(p, par, cprev8, hprev, cw, cb, ng, nb, rows_per_b, qin):
    nc = rows_per_b // qin
    row = lambda b, c: (b * nc + c, 0)
    const = lambda b, c: (0, 0)
    y, hout, cout = pl.pallas_call(
        _ssd_kernel,
        grid=(nb, nc),
        in_specs=[pl.BlockSpec(memory_space=pltpu.SMEM),
                  pl.BlockSpec((qin, SSD_CONV_DIM), row), pl.BlockSpec((qin, LANES), row),
                  pl.BlockSpec((qin, 256), lambda b, c: (b * nc + c, 2)),
                  pl.BlockSpec((None, 8, SSD_CONV_DIM), lambda b, c: (b, 0, 0)),
                  pl.BlockSpec((None, SSD_STATE, BRANCH_W), lambda b, c: (b, 0, 0)),
                  pl.BlockSpec((SSD_CONV, SSD_CONV_DIM), const), pl.BlockSpec((1, SSD_CONV_DIM), const),
                  pl.BlockSpec((1, BRANCH_W), const)],
        out_specs=[pl.BlockSpec((qin, BRANCH_W), row),
                   pl.BlockSpec((None, SSD_STATE, BRANCH_W), lambda b, c: (b, 0, 0)),
                   pl.BlockSpec((None, 8, SSD_CONV_DIM), lambda b, c: (b, 0, 0))],
        out_shape=[jax.ShapeDtypeStruct((nb * rows_per_b, BRANCH_W), F32),
                   jax.ShapeDtypeStruct((nb, SSD_STATE, BRANCH_W), F32),
                   jax.ShapeDtypeStruct((nb, 8, SSD_CONV_DIM), F32)],
        scratch_shapes=[pltpu.VMEM((8 + SSD_CHUNK, SSD_CONV_DIM), F32),
                        pltpu.VMEM((SSD_STATE, BRANCH_W), F32)],
        compiler_params=_cparams(2),
        name="ssd",
    )(par, p["sxbc"], p["dts"], p["gates"], cprev8, _state_to_lanes(hprev), cw, cb, ng)
    return y, _state_from_lanes(hout), cout[:, 8 - (SSD_CONV - 1):, :]


def _state_to_lanes(h):
    b = h.shape[0]
    return h.reshape(b, SSD_HEADS * SSD_HEAD_DIM, SSD_STATE).transpose(0, 2, 1)


def _state_from_lanes(ht):
    b = ht.shape[0]
    return ht.transpose(0, 2, 1).reshape(b, SSD_HEADS, SSD_HEAD_DIM, SSD_STATE)


def _pool_kernel(u_ref, halo_ref, w_ref, scale_ref, o_ref, hout_ref, e_scr, *, start_pos):
    t = u_ref.shape[0]
    c = pl.program_id(1)
    last = pl.num_programs(1) - 1

    @pl.when(c == 0)
    def _():
        e_scr[0:8, :] = jnp.zeros((8, BRANCH_W), F32)
        e_scr[8:24, :] = halo_ref[...]

    u = u_ref[...]
    e_scr[24:24 + t, :] = u
    new_halo = e_scr[8 + t:24 + t, :]
    n = 16 + t
    cur = e_scr[8:8 + n, :]
    stages = []
    for k in (1, 2, 4, 8):
        cur = cur + e_scr[8 - k:8 - k + n, :]
        stages.append(cur[16:, :])
        e_scr[8:8 + n, :] = cur
    grp = _group_of_lane((t, BRANCH_W), 6)
    win = jnp.where(grp == 0, stages[0], jnp.where(grp == 1, stages[1], jnp.where(grp == 2, stages[2], stages[3])))
    wlen = jnp.where(grp == 0, 2.0, jnp.where(grp == 1, 4.0, jnp.where(grp == 2, 8.0, 16.0)))
    n_avail = (start_pos + c * t + 1 + lax.broadcasted_iota(I32, (t, BRANCH_W), 0)).astype(F32)
    d = win / jnp.minimum(wlen, n_avail) - u
    o_ref[...] = jnp.dot(d.astype(BF16), w_ref[...], preferred_element_type=F32) * scale_ref[...]
    e_scr[8:24, :] = new_halo

    @pl.when(c == last)
    def _():
        hout_ref[...] = new_halo


def _pool(u, halo16, wbd, scale, nb, rows_per_b, t, start_pos):
    nc = rows_per_b // t
    row = lambda b, c: (b * nc + c, 0)
    const = lambda b, c: (0, 0)
    o, hout = pl.pallas_call(
        functools.partial(_pool_kernel, start_pos=start_pos),
        grid=(nb, nc),
        in_specs=[pl.BlockSpec((t, BRANCH_W), row), pl.BlockSpec((None, 16, BRANCH_W), lambda b, c: (b, 0, 0)),
                  pl.BlockSpec((BRANCH_W, BRANCH_W), const), pl.BlockSpec((1, BRANCH_W), const)],
        out_specs=[pl.BlockSpec((t, BRANCH_W), row), pl.BlockSpec((None, 16, BRANCH_W), lambda b, c: (b, 0, 0))],
        out_shape=[jax.ShapeDtypeStruct((nb * rows_per_b, BRANCH_W), F32),
                   jax.ShapeDtypeStruct((nb, 16, BRANCH_W), F32)],
        scratch_shapes=[pltpu.VMEM((24 + t, BRANCH_W), F32)],
        compiler_params=_cparams(2),
        name="pool",
    )(u, halo16, wbd, scale)
    return o, hout[:, 1:, :]


def _merge_kernel(x_ref, ng_ref, wmg_ref, dsa_ref, diff_ref, ssd_ref, pool_ref, gates_ref, wbr_ref, wout_ref, y_ref):
    x = x_ref[...]
    h = x * lax.rsqrt(jnp.mean(x * x, axis=-1, keepdims=True) + EPS) * ng_ref[...]
    hb = h.astype(BF16)
    d = x.shape[1]
    br = (dsa_ref[...] * _silu(gates_ref[:, 0:256]),
          diff_ref[...] * _silu(gates_ref[:, 256:512]),
          ssd_ref[...],
          pool_ref[...] * _silu(gates_ref[:, 768:1024]))
    m = jnp.zeros(x.shape, F32)
    for n in range(N_BRANCH):
        mg = jnp.dot(hb, wmg_ref[:, n * d:(n + 1) * d], preferred_element_type=F32)
        up = jnp.dot(br[n].astype(BF16), wbr_ref[n], preferred_element_type=F32)
        m = m + _sigmoid(mg) * up
    y_ref[...] = x + jnp.dot(m.astype(BF16), wout_ref[...], preferred_element_type=F32)


def _merge(x, ng, wmg, dsa_o, diff_o, ssd_o, pool_o, gates, wbr, wout, tm):
    n, d = x.shape
    row = lambda i: (i, 0)
    const = lambda i: (0, 0)
    return pl.pallas_call(
        _merge_kernel,
        grid=(n // tm,),
        in_specs=[pl.BlockSpec((tm, d), row), pl.BlockSpec((1, d), const), pl.BlockSpec(wmg.shape, const),
                  pl.BlockSpec((tm, 256), row), pl.BlockSpec((tm, 256), row), pl.BlockSpec((tm, 256), row),
                  pl.BlockSpec((tm, 256), row), pl.BlockSpec((tm, 1024), row),
                  pl.BlockSpec(wbr.shape, lambda i: (0, 0, 0)), pl.BlockSpec(wout.shape, const)],
        out_specs=pl.BlockSpec((tm, d), row),
        out_shape=jax.ShapeDtypeStruct((n, d), F32),
        compiler_params=_cparams(1),
        name="merge",
    )(x, ng, wmg, dsa_o, diff_o, ssd_o, pool_o, gates, wbr, wout)


_PAGES_PER_STEP = 8
_KEY_CHUNK = 512
_Q_BLOCK = 128
_MERGE_ROWS = 256


def _page_specs(layer, features, npp):
    def spec(k):
        return pl.BlockSpec((None, None, features, PAGE_SIZE),
                            lambda b, g, pt, k=k: (layer, pt[b, g * npp + k], 0, 0))
    return [spec(k) for k in range(npp)]


def _pages_view(cache):
    depth, n_pool, page = cache.shape[:3]
    c = cache.reshape(depth, n_pool, page, -1)
    return jnp.swapaxes(c, 2, 3)


def _dsa_sidx_kernel(pt_ref, iq_ref, misc_ref, *rest, npp, n_pages, k_top):
    pages = rest[:npp]
    bias_ref = rest[npp]
    key_scr, qi_scr, jst_scr = rest[npp + 1:]
    t = iq_ref.shape[0]
    g = pl.program_id(1)
    last = pl.num_programs(1) - 1
    wts = misc_ref[:, 64:64 + IDX_HEADS]

    def scores(kt):
        s = jnp.dot(qi_scr[...], kt, preferred_element_type=F32)
        acc = None
        for h in range(IDX_HEADS):
            v = wts[:, h:h + 1] * jnp.maximum(s[h * t:(h + 1) * t, :], 0.0)
            acc = v if acc is None else acc + v
        return jnp.where(acc == 0.0, 0.0, acc)

    @pl.when(g == 0)
    def _():
        iq = iq_ref[...]
        for h in range(IDX_HEADS):
            qi_scr[h * t:(h + 1) * t, :] = iq[:, IDX_DIM * h:IDX_DIM * (h + 1)]
        knew = jnp.concatenate([misc_ref[...], jnp.zeros((PAGE_SIZE - t, LANES), F32)], axis=0)
        sc = scores(knew.T[0:IDX_DIM, :].astype(BF16))
        causal = lax.broadcasted_iota(I32, (t, LANES), 1) <= lax.broadcasted_iota(I32, (t, LANES), 0)
        key_scr[n_pages] = jnp.where(causal, _sortable(sc), INT_MIN)

    sc = scores(jnp.concatenate([pg[...].astype(BF16) for pg in pages], axis=1))
    for k in range(npp):
        key_scr[g * npp + k] = _sortable(sc[:, k * PAGE_SIZE:(k + 1) * PAGE_SIZE])

    @pl.when(g == last)
    def _():
        keys = key_scr[...]
        kpos = (lax.broadcasted_iota(I32, keys.shape, 0) * PAGE_SIZE + lax.broadcasted_iota(I32, keys.shape, 2))

        def count(m):
            return jnp.sum(jnp.sum(m, axis=0), axis=1, keepdims=True)

        kf = float(k_top)
        c0 = count(jnp.where(keys >= 0, 1.0, 0.0))
        cand0 = jnp.where(c0 >= kf, 0, INT_MIN).astype(I32)

        def bit_body(b, cand):
            trial = cand | jnp.left_shift(jnp.int32(1), 30 - b)
            cnt = count(jnp.where(keys >= trial[None], 1.0, 0.0))
            return jnp.where(cnt >= kf, trial, cand)

        thr = lax.fori_loop(0, 31, bit_body, cand0)
        cnt_gt = count(jnp.where(keys > thr[None], 1.0, 0.0))
        cnt_ge = count(jnp.where(keys >= thr[None], 1.0, 0.0))
        need = jnp.where(cnt_ge > kf, jnp.where(thr > INT_MIN, 1.0, 0.0), 0.0)
        rem = kf - cnt_gt
        big = jnp.int32(2 ** 30)
        jst_scr[...] = jnp.full((t, 1), big, I32)

        @pl.when(jnp.max(need) > 0.0)
        def _():
            nbits = ((n_pages + 1) * PAGE_SIZE - 1).bit_length()

            def jb(b, pos):
                trial = pos | jnp.left_shift(jnp.int32(1), nbits - 1 - b)
                cnt = count(jnp.where(keys == thr[None], jnp.where(kpos < trial[None], 1.0, 0.0), 0.0))
                return jnp.where(cnt < rem, trial, pos)

            pos = lax.fori_loop(0, nbits, jb, jnp.zeros((t, 1), I32))
            jst_scr[...] = jnp.where(need > 0.0, pos, big)

        jst = jst_scr[...]
        sel = jnp.where(keys > thr[None], 0.0,
                        jnp.where(keys == thr[None], jnp.where(kpos <= jst[None], 0.0, NEG), NEG))
        newc = lax.broadcasted_iota(I32, keys.shape, 0) == n_pages
        causal = lax.broadcasted_iota(I32, keys.shape, 2) <= lax.broadcasted_iota(I32, keys.shape, 1)
        bias_ref[...] = jnp.where(newc, jnp.where(causal, sel, NEG), sel)


def _dsa_sidx(ps, cache_ik, layer, page_table, bs, t):
    n_pages = page_table.shape[1]
    npp = _PAGES_PER_STEP
    k_top = min(IDX_TOPK_MAX, (n_pages * PAGE_SIZE + t) // 4)
    row = lambda b, g, pt: (b, 0)
    grid_spec = pltpu.PrefetchScalarGridSpec(
        num_scalar_prefetch=1, grid=(bs, n_pages // npp),
        in_specs=[pl.BlockSpec((t, 256), row), pl.BlockSpec((t, 128), row)] + _page_specs(layer, IDX_DIM, npp),
        out_specs=pl.BlockSpec((None, n_pages + 1, t, LANES), lambda b, g, pt: (b, 0, 0, 0)),
        scratch_shapes=[pltpu.VMEM((n_pages + 1, t, LANES), I32), pltpu.VMEM((IDX_HEADS * t, IDX_DIM), BF16),
                        pltpu.VMEM((t, 1), I32)])
    return pl.pallas_call(
        functools.partial(_dsa_sidx_kernel, npp=npp, n_pages=n_pages, k_top=k_top),
        grid_spec=grid_spec,
        out_shape=jax.ShapeDtypeStruct((bs, n_pages + 1, t, LANES), F32),
        compiler_params=_cparams(2),
        name="dsa_sample_index",
    )(page_table, ps["iq"], ps["misc"], *([cache_ik] * npp))


def _paged_attn_kernel(pt_ref, q_ref, kn_ref, vn_ref, *rest, npp, nheads, log2_hd, use_bias, finish):
    idx = 0
    if use_bias:
        bias_new_ref, bias_pg_ref = rest[0], rest[1]
        idx = 2
    kpages = rest[idx:idx + npp]
    vpages = rest[idx + npp:idx + 2 * npp]
    n_scr = 7
    extra = rest[idx + 2 * npp:-(n_scr + 1)]
    o_ref = rest[-(n_scr + 1)]
    qa_scr, m_scr, l_scr, alpha_scr, acc_scr, p_scr, s_scr = rest[-n_scr:]
    t = q_ref.shape[0]
    rows = nheads * t
    g = pl.program_id(1)
    last = pl.num_programs(1) - 1

    def tile_rows(b):
        return jnp.concatenate([b] * nheads, axis=0)

    def accumulate(pv):
        alpha = alpha_scr[...]
        acc_scr[...] = jnp.concatenate([alpha, alpha], axis=1) * acc_scr[...] + pv

    @pl.when(g == 0)
    def _():
        qv = q_ref[...]
        grp = _group_of_lane((t, BRANCH_W), log2_hd)
        for h in range(nheads):
            qa_scr[h * t:(h + 1) * t, :] = jnp.where(grp == h, qv, jnp.zeros_like(qv))
        m_scr[...] = jnp.full((rows, LANES), NEG, F32)
        l_scr[...] = jnp.zeros((rows, LANES), F32)
        acc_scr[...] = jnp.zeros((rows, BRANCH_W), F32)
        kn = jnp.concatenate([kn_ref[...], jnp.zeros((PAGE_SIZE - t, BRANCH_W), F32)], axis=0)
        vn = jnp.concatenate([vn_ref[...], jnp.zeros((PAGE_SIZE - t, BRANCH_W), BF16)], axis=0)
        if use_bias:
            bias = bias_new_ref[0]
        else:
            causal = lax.broadcasted_iota(I32, (t, LANES), 1) <= lax.broadcasted_iota(I32, (t, LANES), 0)
            bias = jnp.where(causal, 0.0, NEG)
        s_new = s_scr.at[:, 0:PAGE_SIZE]
        p_new = p_scr.at[:, 0:PAGE_SIZE]
        s_new[...] = jnp.dot(qa_scr[...], kn.T.astype(BF16), preferred_element_type=F32)
        _flash_rows(s_new, tile_rows(bias), m_scr, l_scr, alpha_scr, p_new, 0, rows)
        accumulate(jnp.dot(p_new[...], vn, preferred_element_type=F32))

    kt = jnp.concatenate([kp[...].astype(BF16) for kp in kpages], axis=1)
    vt = jnp.concatenate([vp[...].astype(BF16) for vp in vpages], axis=1)
    s_scr[...] = jnp.dot(qa_scr[...], kt, preferred_element_type=F32)
    bias = None
    if use_bias:
        bias = tile_rows(jnp.concatenate([bias_pg_ref[k] for k in range(npp)], axis=1))
    _flash_rows(s_scr, bias, m_scr, l_scr, alpha_scr, p_scr, 0, rows)
    accumulate(_nk_dot(p_scr[...], vt))

    @pl.when(g == last)
    def _():
        o_ref[...] = finish(acc_scr, l_scr, t, *extra)


def _dsa_finish(acc_ref, l_ref, t):
    grp = _group_of_lane((t, BRANCH_W), 6)
    out = jnp.zeros((t, BRANCH_W), F32)
    for h in range(DSA_HEADS):
        l_row = jnp.sum(l_ref[h * t:(h + 1) * t, :], axis=1, keepdims=True)
        out = out + jnp.where(grp == h, acc_ref[h * t:(h + 1) * t, :] / l_row, 0.0)
    return out


def _diff_sample_finish(acc_ref, l_ref, t, lam_ref, gsub_ref, *, lam_init):
    return _diff_finish(acc_ref, l_ref, t, _diff_lambda(lam_ref, lam_init), lam_init, gsub_ref[...])


def _paged_attn(q, kn, vn, cache_k, cache_v, layer, page_table, bs, t, *, nheads, log2_hd, bias=None,
                extra=(), finish, name):
    n_pages = page_table.shape[1]
    npp = _PAGES_PER_STEP
    row = lambda b, g, pt: (b, 0)
    const = lambda b, g, pt: (0, 0)
    in_specs = [pl.BlockSpec((t, 256), row), pl.BlockSpec((t, 256), row), pl.BlockSpec((t, 256), row)]
    args = [q, kn, vn]
    if bias is not None:
        in_specs += [pl.BlockSpec((None, 1, t, LANES), lambda b, g, pt: (b, n_pages, 0, 0)),
                     pl.BlockSpec((None, npp, t, LANES), lambda b, g, pt: (b, g, 0, 0))]
        args += [bias, bias]
    in_specs += _page_specs(layer, 256, npp) + _page_specs(layer, 256, npp)
    args += [cache_k] * npp + [cache_v] * npp
    for e in extra:
        in_specs.append(pl.BlockSpec(e.shape, const))
        args.append(e)
    rows = nheads * t
    grid_spec = pltpu.PrefetchScalarGridSpec(
        num_scalar_prefetch=1, grid=(bs, n_pages // npp), in_specs=in_specs,
        out_specs=pl.BlockSpec((t, 256), row),
        scratch_shapes=[pltpu.VMEM((rows, 256), BF16), pltpu.VMEM((rows, LANES), F32),
                        pltpu.VMEM((rows, LANES), F32), pltpu.VMEM((rows, LANES), F32),
                        pltpu.VMEM((rows, 256), F32), pltpu.VMEM((rows, npp * PAGE_SIZE), BF16),
                        pltpu.VMEM((rows, npp * PAGE_SIZE), F32)])
    return pl.pallas_call(
        functools.partial(_paged_attn_kernel, npp=npp, nheads=nheads, log2_hd=log2_hd,
                          use_bias=bias is not None, finish=finish),
        grid_spec=grid_spec,
        out_shape=jax.ShapeDtypeStruct((bs * t, BRANCH_W), F32),
        compiler_params=_cparams(2),
        name=name,
    )(page_table, *args)


def _rope_table(pos, half, reps, extra_cos=None):
    inv = ROPE_THETA ** (-jnp.arange(half, dtype=F32) / half)
    ang = pos.astype(F32)[:, None] * inv[None, :]
    cos, sin = jnp.cos(ang), jnp.sin(ang)
    cos_h = jnp.concatenate([cos, cos], axis=1)
    sin_h = jnp.concatenate([-sin, sin], axis=1)
    cos_t, sin_t = jnp.tile(cos_h, (1, reps)), jnp.tile(sin_h, (1, reps))
    if extra_cos is not None:
        n = pos.shape[0]
        cos_t = jnp.concatenate([cos_t, jnp.broadcast_to(extra_cos[None, :], (n, extra_cos.shape[0]))], axis=1)
        sin_t = jnp.concatenate([sin_t, jnp.zeros((n, extra_cos.shape[0]), F32)], axis=1)
    return jnp.concatenate([cos_t, sin_t], axis=1)


def _tables(pos):
    misc_scale = jnp.concatenate([jnp.full((IDX_HEADS,), IDX_HEADS ** -0.5, F32),
                                  jnp.ones((LANES - IDX_DIM - IDX_HEADS,), F32)])
    return (_rope_table(pos, DSA_HEAD_DIM // 2, 4), _rope_table(pos, DIFF_HEAD_DIM // 2, 8),
            _rope_table(pos, IDX_DIM // 2, 1, misc_scale))


def _pack_w_in(w):
    sizes = (256, 256, 256, 256, 256, 64, 4, 256, 256, 256, 256, 256, 512, 4, 256, 256, 4096)
    offs = [0]
    for s in sizes:
        offs.append(offs[-1] + s)
    (dq, dk, dv, dg, iq, ik, iw, fq, fk, fv, fg, sz, sxbc, sdt, pu, pg, mg) = [
        w[:, offs[i]:offs[i + 1]] for i in range(len(sizes))]
    d = w.shape[0]
    misc = jnp.concatenate([ik, iw, jnp.zeros((d, LANES - IDX_DIM - IDX_HEADS), w.dtype)], axis=1)
    dts = jnp.concatenate([sdt, jnp.zeros((d, LANES - SSD_HEADS), w.dtype)], axis=1)
    packed = jnp.concatenate([dq, dk, dv, iq, fq, fk, fv, dg, fg, sz, pg, sxbc, pu, misc, dts], axis=1)
    return packed.astype(BF16), mg.astype(BF16)


def _block_diag(w):
    g, n, _ = w.shape
    out = jnp.zeros((g * n, g * n), w.dtype)
    for i in range(g):
        out = out.at[i * n:(i + 1) * n, i * n:(i + 1) * n].set(w[i])
    return out


def kernel(x_prompt, x_sample, cache_dsa_k, cache_dsa_v, cache_idx_k, cache_diff_k, cache_diff_v, state_ssm,
           state_conv, state_pool, page_table, norm_g, w_in, dsa_qk_g, diff_qk_g, diff_lam, diff_subln,
           ssd_conv_w, ssd_conv_b, ssd_dt_bias, ssd_a_log, ssd_d, ssd_norm, pool_w, pool_scale, w_branch, w_out):
    bp, seq, d = x_prompt.shape
    bs, t, _ = x_sample.shape
    depth = norm_g.shape[0]
    n_pages = page_table.shape[1]
    past = n_pages * PAGE_SIZE
    n_pool = cache_dsa_k.shape[1]

    tab_p = _tables(jnp.arange(seq, dtype=I32))
    tab_s = tuple(jnp.tile(a, (bs, 1)) for a in _tables(past + jnp.arange(t, dtype=I32)))
    ck, cv, cik = _pages_view(cache_dsa_k), _pages_view(cache_dsa_v), _pages_view(cache_idx_k)
    cfk, cfv = _pages_view(cache_diff_k), _pages_view(cache_diff_v)

    tm = min(_KEY_CHUNK, seq)
    tq = min(_Q_BLOCK, seq)
    tm_s = min(256, bs * t)
    pool_t = min(512, seq)
    xp = x_prompt.reshape(bp * seq, d)
    xs = x_sample.reshape(bs * t, d)
    acc_p = {n: [] for n in ("dk", "dv", "ik", "fk", "fv", "ssm", "conv", "pool")}
    acc_s = {n: [] for n in acc_p}

    for l in range(depth):
        lam_init = 0.8 - 0.6 * math.exp(-0.3 * l)
        wp, wmg = _pack_w_in(w_in[l])
        ng = norm_g[l].reshape(1, d)
        gains = jnp.stack([jnp.tile(dsa_qk_g[l, 0], 4), jnp.tile(dsa_qk_g[l, 1], 4),
                           jnp.tile(diff_qk_g[l, 0], 8), jnp.tile(diff_qk_g[l, 1], 8)])
        gsub = jnp.tile(diff_subln[l], 4).reshape(1, BRANCH_W)
        lam_p = diff_lam[l]
        ssd_par = jnp.concatenate([ssd_dt_bias[l], ssd_a_log[l], ssd_d[l]]).astype(F32)
        cw, cb = ssd_conv_w[l], ssd_conv_b[l].reshape(1, SSD_CONV_DIM)
        sng = ssd_norm[l].reshape(1, BRANCH_W)
        wbd = _block_diag(pool_w[l]).astype(BF16)
        pscale = pool_scale[l].reshape(1, BRANCH_W)
        wbr = w_branch[l].astype(BF16)
        wout = w_out[l].astype(BF16)

        pp = _proj(xp, ng, wp, *tab_p, gains, tm, seq // tm)
        dsa_o = _dsa_prompt(pp, bp, seq, tq)
        diff_o = _diff_prompt(pp, lam_p, gsub, lam_init, bp, seq, tq)
        ssd_o, ssm_new, conv_new = _ssd(pp, ssd_par, jnp.zeros((bp, 8, SSD_CONV_DIM), F32),
                                        jnp.zeros((bp, SSD_HEADS, SSD_HEAD_DIM, SSD_STATE), F32), cw, cb, sng,
                                        bp, seq, SSD_CHUNK)
        pool_o, pool_new = _pool(pp["pu"], jnp.zeros((bp, 16, BRANCH_W), F32), wbd, pscale, bp, seq, pool_t, 0)
        xp = _merge(xp, ng, wmg, dsa_o, diff_o, ssd_o, pool_o, pp["gates"], wbr, wout, min(_MERGE_ROWS, seq))
        for n, v in zip(acc_p, (pp["dk"].reshape(bp, seq, DSA_HEADS, DSA_HEAD_DIM),
                                pp["dv"].reshape(bp, seq, DSA_HEADS, DSA_HEAD_DIM),
                                pp["misc"][:, :IDX_DIM].reshape(bp, seq, IDX_DIM),
                                pp["fk"].reshape(bp, seq, 2 * DIFF_HEADS, DIFF_HEAD_DIM),
                                pp["fv"].reshape(bp, seq, DIFF_HEADS, DIFF_V_DIM),
                                ssm_new, conv_new, pool_new)):
            acc_p[n].append(v)

        ps = _proj(xs, ng, wp, *tab_s, gains, tm_s, (bs * t) // tm_s)
        bias = _dsa_sidx(ps, cik, l, page_table, bs, t)
        dsa_o = _paged_attn(ps["dq"], ps["dk"], ps["dvb"], ck, cv, l, page_table, bs, t,
                            nheads=DSA_HEADS, log2_hd=6, bias=bias, finish=_dsa_finish, name="dsa_sample_attn")
        diff_o = _paged_attn(ps["fq"], ps["fk"], ps["fvb"], cfk, cfv, l, page_table, bs, t,
                             nheads=2 * DIFF_HEADS, log2_hd=5, extra=(lam_p, gsub),
                             finish=functools.partial(_diff_sample_finish, lam_init=lam_init),
                             name="diff_sample_attn")
        cprev8 = jnp.concatenate([jnp.zeros((bs, 8 - (SSD_CONV - 1), SSD_CONV_DIM), F32), state_conv[l]], axis=1)
        ssd_o, ssm_new, conv_new = _ssd(ps, ssd_par, cprev8, state_ssm[l], cw, cb, sng, bs, t, t)
        halo = jnp.concatenate([jnp.zeros((bs, 1, BRANCH_W), F32), state_pool[l]], axis=1)
        pool_o, pool_new = _pool(ps["pu"], halo, wbd, pscale, bs, t, t, past)
        xs = _merge(xs, ng, wmg, dsa_o, diff_o, ssd_o, pool_o, ps["gates"], wbr, wout, tm_s)
        for n, v in zip(acc_s, (ps["dk"].reshape(bs, t, DSA_HEADS, DSA_HEAD_DIM),
                                ps["dv"].reshape(bs, t, DSA_HEADS, DSA_HEAD_DIM),
                                ps["misc"][:, :IDX_DIM].reshape(bs, t, IDX_DIM),
                                ps["fk"].reshape(bs, t, 2 * DIFF_HEADS, DIFF_HEAD_DIM),
                                ps["fv"].reshape(bs, t, DIFF_HEADS, DIFF_V_DIM),
                                ssm_new, conv_new, pool_new)):
            acc_s[n].append(v)

    names = ("dk", "dv", "ik", "fk", "fv", "ssm", "conv", "pool")
    return ((xp.reshape(bp, seq, d), xs.reshape(bs, t, d))
            + tuple(jnp.stack(acc_p[n]) for n in names) + tuple(jnp.stack(acc_s[n]) for n in names))
```

```python
import functools
import math

import jax
import jax.numpy as jnp
from jax import lax
from jax.experimental import pallas as pl
from jax.experimental.pallas import tpu as pltpu

F32 = jnp.float32
BF16 = jnp.bfloat16
I32 = jnp.int32

BRANCH_W = 256
DSA_HEADS = 4
DSA_HEAD_DIM = 64
IDX_HEADS = 4
IDX_DIM = 64
IDX_TOPK_MAX = 256
DIFF_HEADS = 4
DIFF_HEAD_DIM = 32
DIFF_V_DIM = 64
SSD_HEADS = 4
SSD_HEAD_DIM = 64
SSD_STATE = 64
SSD_GROUPS = 2
SSD_CONV = 4
SSD_CHUNK = 128
SSD_CONV_DIM = 512
POOL_WINDOWS = (2, 4, 8, 16)
POOL_STATE = 15
PAGE_SIZE = 128
ROPE_THETA = 10000.0
EPS = 1e-6
N_BRANCH = 4

LANES = 128
SUBLANES = 8
VMEM_LIMIT = 56 * 1024 * 1024

LOG2E = math.log2(math.e)
NEG = -1e30
INT_MIN = -(2 ** 31)
HIGHEST = lax.Precision.HIGHEST

_SLABS = (("dq", 256), ("dk", 256), ("dv", 256), ("iq", 256), ("fq", 256), ("fk", 256), ("fv", 256),
          ("gates", 1024), ("sxbc", 512), ("pu", 256), ("misc", 128), ("dts", 128))
_OFF = {}
_o = 0
for _n, _w in _SLABS:
    _OFF[_n] = (_o, _o + _w)
    _o += _w
W_PACKED = _o


def _cparams(n_axes, vmem=VMEM_LIMIT):
    return pltpu.CompilerParams(dimension_semantics=("arbitrary",) * n_axes, vmem_limit_bytes=vmem)


def _nk_dot(a, b):
    return lax.dot_general(a, b, (((1,), (1,)), ((), ())), preferred_element_type=F32)


def _silu(x):
    return x * (1.0 / (1.0 + jnp.exp(-x)))


def _sigmoid(x):
    return 1.0 / (1.0 + jnp.exp(-x))


def _group_of_lane(shape, log2_width):
    return lax.broadcasted_iota(I32, shape, len(shape) - 1) >> log2_width


def _proj_kernel(x_ref, ng_ref, w_ref, t64_ref, t32_ref, tmisc_ref, gains_ref,
                 dq_ref, dk_ref, dv_ref, dvb_ref, iq_ref, misc_ref, fq_ref, fk_ref, fv_ref, fvb_ref,
                 gates_ref, sxbc_ref, pu_ref, dts_ref, dkt_ref, ikt_ref, fkt_ref):
    x = x_ref[...]
    h = x * lax.rsqrt(jnp.mean(x * x, axis=-1, keepdims=True) + EPS) * ng_ref[...]
    hb = h.astype(BF16)

    def mm(name):
        a, b = _OFF[name]
        return jnp.dot(hb, w_ref[:, a:b], preferred_element_type=F32)

    def head_norm(z, g, log2_hd):
        n = z.shape[1]
        r = lax.broadcasted_iota(I32, (n, n), 0) >> log2_hd
        c = lax.broadcasted_iota(I32, (n, n), 1) >> log2_hd
        bd = jnp.where(r == c, 1.0 / (1 << log2_hd), 0.0).astype(F32)
        ms = jnp.dot(z * z, bd, preferred_element_type=F32, precision=HIGHEST)
        return z * lax.rsqrt(ms + EPS) * g

    def rope(z, tab_ref, half):
        n = z.shape[1]
        cos = tab_ref[:, :n]
        sin = tab_ref[:, n:]
        lane = lax.broadcasted_iota(I32, z.shape, 1)
        first = (lane & (2 * half - 1)) < half
        partner = jnp.where(first, pltpu.roll(z, n - half, 1), pltpu.roll(z, half, 1))
        return z * cos + partner * sin

    dq = rope(head_norm(mm("dq"), gains_ref[0:1, :], 6), t64_ref, 32)
    dq_ref[...] = (dq * (DSA_HEAD_DIM ** -0.5 * LOG2E)).astype(BF16)
    dk = rope(head_norm(mm("dk"), gains_ref[1:2, :], 6), t64_ref, 32)
    dk_ref[...] = dk
    dkt_ref[...] = dk.T.astype(BF16)
    dv = mm("dv")
    dv_ref[...] = dv
    dvb_ref[...] = dv.astype(BF16)
    iq_ref[...] = (rope(mm("iq"), t64_ref, 32) * (IDX_DIM ** -0.5)).astype(BF16)
    misc = rope(mm("misc"), tmisc_ref, 32)
    misc_ref[...] = misc
    ikt_ref[...] = misc.T[0:IDX_DIM, :].astype(BF16)
    fq = rope(head_norm(mm("fq"), gains_ref[2:3, :], 5), t32_ref, 16)
    fq_ref[...] = (fq * (DIFF_HEAD_DIM ** -0.5 * LOG2E)).astype(BF16)
    fk = rope(head_norm(mm("fk"), gains_ref[3:4, :], 5), t32_ref, 16)
    fk_ref[...] = fk
    fkt_ref[...] = fk.T.astype(BF16)
    fv = mm("fv")
    fv_ref[...] = fv
    fvb_ref[...] = fv.astype(BF16)
    for j in range(4):
        a = _OFF["gates"][0] + 256 * j
        gates_ref[:, 256 * j:256 * (j + 1)] = jnp.dot(hb, w_ref[:, a:a + 256], preferred_element_type=F32)
    for j in range(2):
        a = _OFF["sxbc"][0] + 256 * j
        sxbc_ref[:, 256 * j:256 * (j + 1)] = jnp.dot(hb, w_ref[:, a:a + 256], preferred_element_type=F32)
    pu_ref[...] = mm("pu")
    dts_ref[...] = mm("dts")


_PROJ_OUTS = (("dq", 256, BF16), ("dk", 256, F32), ("dv", 256, F32), ("dvb", 256, BF16), ("iq", 256, BF16),
              ("misc", 128, F32), ("fq", 256, BF16), ("fk", 256, F32), ("fv", 256, F32), ("fvb", 256, BF16),
              ("gates", 1024, F32), ("sxbc", 512, F32), ("pu", 256, F32), ("dts", 128, F32))
_PROJ_T_OUTS = (("dkT", 256), ("ikT", IDX_DIM), ("fkT", 256))


def _proj(x, ng, wp, t64, t32, tmisc, gains, tm, tab_blocks):
    n, d = x.shape
    row = lambda i: (i, 0)
    tab = lambda i: (i % tab_blocks, 0)
    const = lambda i: (0, 0)
    outs = _PROJ_OUTS
    res = pl.pallas_call(
        _proj_kernel,
        grid=(n // tm,),
        in_specs=[pl.BlockSpec((tm, d), row), pl.BlockSpec((1, d), const), pl.BlockSpec(wp.shape, const),
                  pl.BlockSpec((tm, 512), tab), pl.BlockSpec((tm, 512), tab), pl.BlockSpec((tm, 256), tab),
                  pl.BlockSpec((4, 256), const)],
        out_specs=([pl.BlockSpec((tm, w), row) for _, w, _ in outs]
                   + [pl.BlockSpec((None, w, tm), lambda i: (i, 0, 0)) for _, w in _PROJ_T_OUTS]),
        out_shape=([jax.ShapeDtypeStruct((n, w), dt) for _, w, dt in outs]
                   + [jax.ShapeDtypeStruct((n // tm, w, tm), BF16) for _, w in _PROJ_T_OUTS]),
        compiler_params=_cparams(1),
        name="proj",
    )(x, ng, wp, t64, t32, tmisc, gains)
    return dict(zip([o[0] for o in outs] + [o[0] for o in _PROJ_T_OUTS], res))


def _sortable(x):
    bits = pltpu.bitcast(x, I32)
    return bits ^ ((bits >> 31) & 0x7FFFFFFF)


def _flash_rows(s_ref, bias, m_ref, l_ref, alpha_ref, p_ref, r0, r1):
    c_sz = s_ref.shape[1]
    nq = c_sz // LANES
    parts = []
    for q in range(nq):
        v = s_ref[r0:r1, q * LANES:(q + 1) * LANES]
        parts.append(v if bias is None else v + bias[:, q * LANES:(q + 1) * LANES])
    mx = parts[0]
    for q in range(1, nq):
        mx = jnp.maximum(mx, parts[q])
    m_old = m_ref[r0:r1, :]
    m_new = jnp.maximum(m_old, jnp.max(mx, axis=1, keepdims=True))
    alpha = jnp.exp2(m_old - m_new)
    m_ref[r0:r1, :] = m_new
    alpha_ref[r0:r1, :] = alpha
    lsum = None
    for q in range(nq):
        p = jnp.exp2(parts[q] - m_new)
        p_ref[r0:r1, q * LANES:(q + 1) * LANES] = p.astype(BF16)
        lsum = p if lsum is None else lsum + p
    l_ref[r0:r1, :] = alpha * l_ref[r0:r1, :] + lsum


def _flash_loop(n, qa_ref, kt_ref, v_ref, bias_fn, nh, tq, s_scr, p_scr, alpha_scr, m_scr, l_scr, acc_scr):
    c_sz = kt_ref.shape[2]
    rows = nh * tq
    m_scr[...] = jnp.full((rows, LANES), NEG, F32)
    l_scr[...] = jnp.zeros((rows, LANES), F32)
    acc_scr[...] = jnp.zeros((rows, BRANCH_W), F32)

    def scores(c, slot):
        s_scr[slot] = jnp.dot(qa_ref[...], kt_ref[c], preferred_element_type=F32)

    def softmax(c, slot):
        bias = bias_fn(c)
        for h in range(nh):
            _flash_rows(s_scr.at[slot], bias, m_scr, l_scr, alpha_scr.at[slot], p_scr.at[slot],
                        h * tq, (h + 1) * tq)

    def values(c, slot):
        start = pl.multiple_of(c * c_sz, c_sz)
        pv = jnp.dot(p_scr[slot], v_ref[pl.ds(start, c_sz), :], preferred_element_type=F32)
        alpha = alpha_scr[slot]
        acc_scr[...] = jnp.concatenate([alpha, alpha], axis=1) * acc_scr[...] + pv

    scores(0, 0)
    scores(jnp.minimum(1, n - 1), 1)
    softmax(0, 0)

    def body(j, carry):
        c = 2 * j + 1
        scores(jnp.minimum(c + 1, n - 1), 0)
        values(c - 1, 0)
        softmax(c, 1)
        scores(jnp.minimum(c + 2, n - 1), 1)
        values(c, 1)
        softmax(c + 1, 0)
        return carry

    lax.fori_loop(0, (n - 1) // 2, body, 0)

    @pl.when((n - 1) % 2 == 1)
    def _():
        values(n - 2, 0)
        softmax(n - 1, 1)

    values(n - 1, (n - 1) & 1)


def _diff_lambda(lam_ref, lam_init):
    lp = lam_ref[...]
    s1 = jnp.sum(lp[0:1, :] * lp[1:2, :], axis=1, keepdims=True)
    s2 = jnp.sum(lp[2:3, :] * lp[3:4, :], axis=1, keepdims=True)
    return jnp.exp(s1) - jnp.exp(s2) + lam_init


def _diff_finish(acc_ref, l_ref, rows, lam, lam_init, gsub):
    grp = _group_of_lane((rows, BRANCH_W), 6)
    out = jnp.zeros((rows, BRANCH_W), F32)
    for j in range(DIFF_HEADS):
        r1, r2 = 2 * j * rows, (2 * j + 1) * rows
        a1 = acc_ref[r1:r1 + rows, :] / jnp.sum(l_ref[r1:r1 + rows, :], axis=1, keepdims=True)
        a2 = acc_ref[r2:r2 + rows, :] / jnp.sum(l_ref[r2:r2 + rows, :], axis=1, keepdims=True)
        o = a1 - lam * a2
        ms = jnp.sum(jnp.where(grp == j, o * o, 0.0), axis=1, keepdims=True) * (1.0 / DIFF_V_DIM)
        out = out + jnp.where(grp == j, o * lax.rsqrt(ms + EPS), 0.0)
    return out * gsub * (1.0 - lam_init)


def _dsa_prompt_kernel(iq_ref, misc_ref, dq_ref, ik_ref, dk_ref, dv_ref, o_ref,
                       key_scr, hb_scr, qi_scr, qa_scr, m_scr, l_scr, alpha_scr, acc_scr, p_scr, s_scr, *, k_top):
    tq = iq_ref.shape[0]
    c_sz = ik_ref.shape[2]
    seq = ik_ref.shape[0] * c_sz
    i = pl.program_id(1)
    nch = ((i + 1) * tq + c_sz - 1) // c_sz
    qpos = i * tq + lax.broadcasted_iota(I32, (tq, 1), 0)
    lane_c = lax.broadcasted_iota(I32, (tq, c_sz), 1)

    iq = iq_ref[...]
    dq = dq_ref[...]
    grp = _group_of_lane((tq, BRANCH_W), 6)
    for h in range(IDX_HEADS):
        qi_scr[h * tq:(h + 1) * tq, :] = iq[:, IDX_DIM * h:IDX_DIM * (h + 1)]
        qa_scr[h * tq:(h + 1) * tq, :] = jnp.where(grp == h, dq, jnp.zeros_like(dq))
    wts = misc_ref[:, 64:64 + IDX_HEADS]

    def p1(c, carry):
        s_scr[0] = jnp.dot(qi_scr[...], ik_ref[c], preferred_element_type=F32)
        acc = None
        for h in range(IDX_HEADS):
            t = wts[:, h:h + 1] * jnp.maximum(s_scr[0, h * tq:(h + 1) * tq, :], 0.0)
            acc = t if acc is None else acc + t
        acc = jnp.where(acc == 0.0, 0.0, acc)
        causal = c * c_sz + lane_c <= qpos
        key_scr[c] = jnp.where(causal, _sortable(acc), INT_MIN)
        hi = pltpu.bitcast(pltpu.bitcast(acc, I32) & -65536, F32)
        hb_scr[c] = jnp.where(causal, hi, -jnp.inf).astype(BF16)
        return carry

    lax.fori_loop(0, nch, p1, 0)

    def count_hi(trial_hi):
        tb16 = (trial_hi ^ ((trial_hi >> 31) & 0x7FFF)) & 0xFFFF
        tb = pltpu.bitcast(tb16 << 16, F32).astype(BF16)
        tb = jnp.broadcast_to(tb, (tq, LANES))

        one_b = jnp.ones((tq, LANES), BF16)
        zero_b = jnp.zeros((tq, LANES), BF16)

        def body(c, cnt):
            blk = hb_scr[c]
            for q in range(c_sz // LANES):
                cnt = cnt + jnp.where(blk[:, q * LANES:(q + 1) * LANES] >= tb, one_b, zero_b)
            return cnt

        cnt = lax.fori_loop(0, nch, body, jnp.zeros((tq, LANES), BF16))
        return jnp.sum(cnt.astype(F32), axis=1, keepdims=True)

    def count(pred):
        def body(c, cnt):
            m = pred(key_scr[c], c)
            part = m[:, 0:LANES]
            for q in range(1, c_sz // LANES):
                part = part + m[:, q * LANES:(q + 1) * LANES]
            return cnt + part
        cnt = lax.fori_loop(0, nch, body, jnp.zeros((tq, LANES), F32))
        return jnp.sum(cnt, axis=1, keepdims=True)

    kf = float(k_top)
    short = qpos + 1 < k_top
    c0 = count(lambda k, c: jnp.where(k >= 0, 1.0, 0.0))
    hi0 = jnp.where(c0 >= kf, 0, -32768).astype(I32)
    cnt0 = jnp.where(c0 >= kf, c0, 2.0 ** 30)

    def hi_body(exact):
        def body(b, st):
            cand_hi, cnt_c = st
            trial = cand_hi | jnp.left_shift(jnp.int32(1), 14 - b)
            if exact:
                trial32 = jnp.left_shift(trial, 16)
                cnt = count(lambda k, c: jnp.where(k >= trial32, 1.0, 0.0))
            else:
                cnt = count_hi(trial)
            return jnp.where(cnt >= kf, trial, cand_hi), jnp.where(cnt >= kf, cnt, cnt_c)
        return body

    hi0, cnt0 = lax.fori_loop(0, 8, hi_body(False), (hi0, cnt0))
    near_zero = jnp.max(jnp.where(hi0 == 0, 1.0, jnp.where(hi0 == -128, 1.0, 0.0))) > 0.0
    hi0, cnt0 = lax.cond(near_zero,
                         lambda st: lax.fori_loop(8, 15, hi_body(True), st),
                         lambda st: lax.fori_loop(8, 15, hi_body(False), st), (hi0, cnt0))
    cand0 = jnp.left_shift(hi0, 16)

    def refine(shift, cand, cnt_c):
        trial = cand | jnp.left_shift(jnp.int32(1), shift)
        cnt = count(lambda k, c: jnp.where(k >= trial, 1.0, 0.0))
        return jnp.where(cnt >= kf, trial, cand), jnp.where(cnt >= kf, cnt, cnt_c)

    def unresolved(cnt_c):
        return (jnp.max(jnp.where(short, 0.0, jnp.where(cnt_c == kf, 0.0, 1.0))) > 0.0).astype(I32)

    def bits_cond(st):
        return jnp.logical_and(st[0] < 8, st[3] > 0)

    def bits_body(st):
        j, cand, cnt_c, _ = st
        cand, cnt_c = refine(15 - 2 * j, cand, cnt_c)
        cand, cnt_c = refine(14 - 2 * j, cand, cnt_c)
        return j + 1, cand, cnt_c, unresolved(cnt_c)

    _, thr, cnt_ge, _ = lax.while_loop(bits_cond, bits_body, (jnp.int32(0), cand0, cnt0, unresolved(cnt0)))
    need = jnp.where(short, 0.0, jnp.where(cnt_ge > kf, 1.0, 0.0))

    @pl.when(jnp.max(need) > 0.0)
    def _():
        rem = kf - count(lambda k, c: jnp.where(k > thr, 1.0, 0.0))
        nbits = max(1, (seq - 1).bit_length())

        def jb(b, pos):
            trial = pos | jnp.left_shift(jnp.int32(1), nbits - 1 - b)
            cnt = count(lambda k, c: jnp.where(k == thr, jnp.where(c * c_sz + lane_c < trial, 1.0, 0.0), 0.0))
            return jnp.where(cnt < rem, trial, pos)

        pos = lax.fori_loop(0, nbits, jb, jnp.zeros((tq, 1), I32))
        cut = jnp.where(need > 0.0, pos, jnp.int32(2 ** 30))

        def demote(c, carry):
            k = key_scr[c]
            key_scr[c] = jnp.where(k == thr, jnp.where(c * c_sz + lane_c > cut, INT_MIN, k), k)
            return carry

        lax.fori_loop(0, nch, demote, 0)

    thr_sel = jnp.where(short, INT_MIN + 1, thr)

    def sel_bias(c):
        return jnp.where(key_scr[c] >= thr_sel, 0.0, NEG)

    _flash_loop(nch, qa_scr, dk_ref, dv_ref, sel_bias, DSA_HEADS, tq, s_scr, p_scr, alpha_scr, m_scr, l_scr, acc_scr)
    o_ref[...] = _dsa_finish(acc_scr, l_scr, tq)


def _dsa_prompt(p, bp, seq, tq):
    c_sz = p["dkT"].shape[2]
    nq = seq // tq
    k_top = min(IDX_TOPK_MAX, seq // 4)
    qrow = lambda b, i: (b * nq + i, 0)
    kv = lambda b, i: (b, 0)
    kvt = lambda b, i: (b, 0, 0)
    rows = DSA_HEADS * tq
    return pl.pallas_call(
        functools.partial(_dsa_prompt_kernel, k_top=k_top),
        grid=(bp, nq),
        in_specs=[pl.BlockSpec((tq, 256), qrow), pl.BlockSpec((tq, 128), qrow), pl.BlockSpec((tq, 256), qrow),
                  pl.BlockSpec((seq // c_sz, IDX_DIM, c_sz), kvt), pl.BlockSpec((seq // c_sz, 256, c_sz), kvt),
                  pl.BlockSpec((seq, 256), kv)],
        out_specs=pl.BlockSpec((tq, 256), qrow),
        out_shape=jax.ShapeDtypeStruct((bp * seq, BRANCH_W), F32),
        scratch_shapes=[pltpu.VMEM((seq // c_sz, tq, c_sz), I32), pltpu.VMEM((seq // c_sz, tq, c_sz), BF16),
                        pltpu.VMEM((rows, IDX_DIM), BF16),
                        pltpu.VMEM((rows, 256), BF16), pltpu.VMEM((rows, LANES), F32),
                        pltpu.VMEM((rows, LANES), F32), pltpu.VMEM((2, rows, LANES), F32),
                        pltpu.VMEM((rows, 256), F32), pltpu.VMEM((2, rows, c_sz), BF16),
                        pltpu.VMEM((2, rows, c_sz), F32)],
        compiler_params=_cparams(2),
        name="dsa_prompt",
    )(p["iq"], p["misc"], p["dq"], p["ikT"], p["dkT"], p["dvb"])


def _diff_prompt_kernel(fq_ref, fk_ref, fv_ref, lam_ref, gsub_ref, o_ref,
                        qa_scr, m_scr, l_scr, alpha_scr, acc_scr, p_scr, s_scr, *, lam_init):
    tq = fq_ref.shape[0]
    c_sz = fk_ref.shape[2]
    nh = 2 * DIFF_HEADS
    i = pl.program_id(1)
    n_full = (i * tq + 1) // c_sz
    qpos = i * tq + lax.broadcasted_iota(I32, (tq, 1), 0)
    lane_c = lax.broadcasted_iota(I32, (tq, c_sz), 1)
    fq = fq_ref[...]
    grp = _group_of_lane((tq, BRANCH_W), 5)
    for h in range(nh):
        qa_scr[h * tq:(h + 1) * tq, :] = jnp.where(grp == h, fq, jnp.zeros_like(fq))

    def causal_bias(c):
        return jnp.where(c * c_sz + lane_c <= qpos, 0.0, NEG)

    _flash_loop(n_full + 1, qa_scr, fk_ref, fv_ref, causal_bias, nh, tq,
                s_scr, p_scr, alpha_scr, m_scr, l_scr, acc_scr)
    lam = _diff_lambda(lam_ref, lam_init)
    o_ref[...] = _diff_finish(acc_scr, l_scr, tq, lam, lam_init, gsub_ref[...])


def _diff_prompt(p, lam_p, gsub, lam_init, bp, seq, tq):
    c_sz = p["fkT"].shape[2]
    nq = seq // tq
    qrow = lambda b, i: (b * nq + i, 0)
    kv = lambda b, i: (b, 0)
    const = lambda b, i: (0, 0)
    rows = 2 * DIFF_HEADS * tq
    return pl.pallas_call(
        functools.partial(_diff_prompt_kernel, lam_init=lam_init),
        grid=(bp, nq),
        in_specs=[pl.BlockSpec((tq, 256), qrow),
                  pl.BlockSpec((seq // c_sz, 256, c_sz), lambda b, i: (b, 0, 0)), pl.BlockSpec((seq, 256), kv),
                  pl.BlockSpec((4, DIFF_HEAD_DIM), const), pl.BlockSpec((1, 256), const)],
        out_specs=pl.BlockSpec((tq, 256), qrow),
        out_shape=jax.ShapeDtypeStruct((bp * seq, BRANCH_W), F32),
        scratch_shapes=[pltpu.VMEM((rows, 256), BF16), pltpu.VMEM((rows, LANES), F32),
                        pltpu.VMEM((rows, LANES), F32), pltpu.VMEM((2, rows, LANES), F32),
                        pltpu.VMEM((rows, 256), F32), pltpu.VMEM((2, rows, c_sz), BF16),
                        pltpu.VMEM((2, rows, c_sz), F32)],
        compiler_params=_cparams(2),
        name="diff_prompt",
    )(p["fq"], p["fkT"], p["fvb"], lam_p, gsub)


def _ssd_kernel(par_ref, xbc_ref, dts_ref, z_ref, cprev_ref, hprev_ref, cw_ref, cb_ref, ng_ref,
                y_ref, hout_ref, cout_ref, e_scr, h_scr):
    qin = xbc_ref.shape[0]
    q = SSD_CHUNK
    c = pl.program_id(1)
    last = pl.num_programs(1) - 1

    @pl.when(c == 0)
    def _():
        e_scr[0:8, :] = cprev_ref[...]
        h_scr[...] = hprev_ref[...]
        if qin < q:
            e_scr[8 + qin:8 + q, :] = jnp.zeros((q - qin, SSD_CONV_DIM), F32)

    e_scr[8:8 + qin, :] = xbc_ref[...]
    conv = cb_ref[...] + jnp.zeros((q, SSD_CONV_DIM), F32)
    for k in range(SSD_CONV):
        conv = conv + e_scr[5 + k:5 + k + q, :] * cw_ref[k:k + 1, :]
    new_tail = e_scr[qin:qin + 8, :]
    u = _silu(conv)
    xs = u[:, :BRANCH_W]
    b_all = u[:, BRANCH_W:BRANCH_W + SSD_GROUPS * SSD_STATE]
    bm_t = b_all.T.astype(BF16)
    bm = b_all.astype(BF16)
    cm = u[:, BRANCH_W + SSD_GROUPS * SSD_STATE:].astype(BF16)

    dts = dts_ref[...]
    if qin < q:
        dts = jnp.concatenate([dts, jnp.zeros((q - qin, LANES), F32)], axis=0)
    rowi = lax.broadcasted_iota(I32, (q, LANES), 0)
    lanei = lax.broadcasted_iota(I32, (q, LANES), 1)
    bias_l = jnp.zeros((q, LANES), F32)
    alog_l = jnp.zeros((q, LANES), F32)
    for h in range(SSD_HEADS):
        bias_l = jnp.where(lanei == h, par_ref[h], bias_l)
        alog_l = jnp.where(lanei == h, par_ref[SSD_HEADS + h], alog_l)
    pre = dts + bias_l
    dt = jnp.maximum(pre, 0.0) + jnp.log1p(jnp.exp(-jnp.abs(pre)))
    dt = jnp.where(rowi < qin, jnp.where(lanei < SSD_HEADS, dt, 0.0), 0.0)
    adt = dt * (-jnp.exp(alog_l))
    r_qq = lax.broadcasted_iota(I32, (q, q), 0)
    c_qq = lax.broadcasted_iota(I32, (q, q), 1)
    causal = r_qq >= c_qq
    acs_col = jnp.dot(jnp.where(causal, 1.0, 0.0), adt, preferred_element_type=F32, precision=HIGHEST)
    acs_row = jnp.dot(adt.T[0:8, :], jnp.where(r_qq <= c_qq, 1.0, 0.0), preferred_element_type=F32,
                      precision=HIGHEST)

    grp = _group_of_lane((q, BRANCH_W), 6)
    grp1 = _group_of_lane((1, BRANCH_W), 6)
    dt_b = jnp.zeros((q, BRANCH_W), F32)
    a_b = jnp.zeros((q, BRANCH_W), F32)
    d_b = jnp.zeros((1, BRANCH_W), F32)
    for h in range(SSD_HEADS):
        dt_b = jnp.where(grp == h, dt[:, h:h + 1], dt_b)
        a_b = jnp.where(grp == h, acs_col[:, h:h + 1], a_b)
        d_b = jnp.where(grp1 == h, par_ref[2 * SSD_HEADS + h], d_b)
    xdt = xs * dt_b
    xdt_b = xdt.astype(BF16)
    a_last = a_b[q - 1:q, :]

    y_diag = jnp.zeros((q, BRANCH_W), F32)
    hpg = SSD_HEADS // SSD_GROUPS
    cbs = [_nk_dot(cm[:, 64 * g:64 * (g + 1)], bm[:, 64 * g:64 * (g + 1)]) for g in range(SSD_GROUPS)]
    for h in range(SSD_HEADS):
        lmat = jnp.exp(jnp.where(causal, acs_col[:, h:h + 1] - acs_row[h:h + 1, :], -jnp.inf))
        y_diag = y_diag + jnp.dot((cbs[h // hpg] * lmat).astype(BF16),
                                  jnp.where(grp == h, xdt_b, jnp.zeros_like(xdt_b)), preferred_element_type=F32)
    ht = h_scr[...]
    htb = ht.astype(BF16)
    xdec = (xdt * jnp.exp(a_last - a_b)).astype(BF16)
    st = [jnp.dot(bm_t[64 * g:64 * (g + 1), :], xdec, preferred_element_type=F32) for g in range(SSD_GROUPS)]
    yo = [jnp.dot(cm[:, 64 * g:64 * (g + 1)], htb, preferred_element_type=F32) for g in range(SSD_GROUPS)]
    grp_n = _group_of_lane((SSD_STATE, BRANCH_W), 6)
    h_scr[...] = ht * jnp.exp(a_last) + jnp.where(grp_n < hpg, st[0], st[1])
    y = y_diag + jnp.where(grp < hpg, yo[0], yo[1]) * jnp.exp(a_b) + xs * d_b
    gte = y * _silu(z_ref[...]) if qin == q else y[0:qin, :] * _silu(z_ref[...])
    half = BRANCH_W // SSD_GROUPS
    outs = []
    for g in range(SSD_GROUPS):
        gg = gte[:, half * g:half * (g + 1)]
        outs.append(gg * lax.rsqrt(jnp.mean(gg * gg, axis=-1, keepdims=True) + EPS))
    y_ref[...] = jnp.concatenate(outs, axis=1) * ng_ref[...]
    e_scr[0:8, :] = new_tail

    @pl.when(c == last)
    def _():
        hout_ref[...] = h_scr[...]
        cout_ref[...] = new_tail


def _ssd(p, par, cprev8, hprev, cw, cb, ng, nb, rows_per_b, qin):
    nc = rows_per_b // qin
    row = lambda b, c: (b * nc + c, 0)
    const = lambda b, c: (0, 0)
    y, hout, cout = pl.pallas_call(
        _ssd_kernel,
        grid=(nb, nc),
        in_specs=[pl.BlockSpec(memory_space=pltpu.SMEM),
                  pl.BlockSpec((qin, SSD_CONV_DIM), row), pl.BlockSpec((qin, LANES), row),
                  pl.BlockSpec((qin, 256), lambda b, c: (b * nc + c, 2)),
                  pl.BlockSpec((None, 8, SSD_CONV_DIM), lambda b, c: (b, 0, 0)),
                  pl.BlockSpec((None, SSD_STATE, BRANCH_W), lambda b, c: (b, 0, 0)),
                  pl.BlockSpec((SSD_CONV, SSD_CONV_DIM), const), pl.BlockSpec((1, SSD_CONV_DIM), const),
                  pl.BlockSpec((1, BRANCH_W), const)],
        out_specs=[pl.BlockSpec((qin, BRANCH_W), row),
                   pl.BlockSpec((None, SSD_STATE, BRANCH_W), lambda b, c: (b, 0, 0)),
                   pl.BlockSpec((None, 8, SSD_CONV_DIM), lambda b, c: (b, 0, 0))],
        out_shape=[jax.ShapeDtypeStruct((nb * rows_per_b, BRANCH_W), F32),
                   jax.ShapeDtypeStruct((nb, SSD_STATE, BRANCH_W), F32),
                   jax.ShapeDtypeStruct((nb, 8, SSD_CONV_DIM), F32)],
        scratch_shapes=[pltpu.VMEM((8 + SSD_CHUNK, SSD_CONV_DIM), F32),
                        pltpu.VMEM((SSD_STATE, BRANCH_W), F32)],
        compiler_params=_cparams(2),
        name="ssd",
    )(par, p["sxbc"], p["dts"], p["gates"], cprev8, _state_to_lanes(hprev), cw, cb, ng)
    return y, _state_from_lanes(hout), cout[:, 8 - (SSD_CONV - 1):, :]


def _state_to_lanes(h):
    b = h.shape[0]
    return h.reshape(b, SSD_HEADS * SSD_HEAD_DIM, SSD_STATE).transpose(0, 2, 1)


def _state_from_lanes(ht):
    b = ht.shape[0]
    return ht.transpose(0, 2, 1).reshape(b, SSD_HEADS, SSD_HEAD_DIM, SSD_STATE)


def _pool_kernel(u_ref, halo_ref, w_ref, scale_ref, o_ref, hout_ref, e_scr, *, start_pos):
    t = u_ref.shape[0]
    c = pl.program_id(1)
    last = pl.num_programs(1) - 1

    @pl.when(c == 0)
    def _():
        e_scr[0:8, :] = jnp.zeros((8, BRANCH_W), F32)
        e_scr[8:24, :] = halo_ref[...]

    u = u_ref[...]
    e_scr[24:24 + t, :] = u
    new_halo = e_scr[8 + t:24 + t, :]
    n = 16 + t
    cur = e_scr[8:8 + n, :]
    stages = []
    for k in (1, 2, 4, 8):
        cur = cur + e_scr[8 - k:8 - k + n, :]
        stages.append(cur[16:, :])
        e_scr[8:8 + n, :] = cur
    grp = _group_of_lane((t, BRANCH_W), 6)
    win = jnp.where(grp == 0, stages[0], jnp.where(grp == 1, stages[1], jnp.where(grp == 2, stages[2], stages[3])))
    wlen = jnp.where(grp == 0, 2.0, jnp.where(grp == 1, 4.0, jnp.where(grp == 2, 8.0, 16.0)))
    n_avail = (start_pos + c * t + 1 + lax.broadcasted_iota(I32, (t, BRANCH_W), 0)).astype(F32)
    d = win / jnp.minimum(wlen, n_avail) - u
    o_ref[...] = jnp.dot(d.astype(BF16), w_ref[...], preferred_element_type=F32) * scale_ref[...]
    e_scr[8:24, :] = new_halo

    @pl.when(c == last)
    def _():
        hout_ref[...] = new_halo


def _pool(u, halo16, wbd, scale, nb, rows_per_b, t, start_pos):
    nc = rows_per_b // t
    row = lambda b, c: (b * nc + c, 0)
    const = lambda b, c: (0, 0)
    o, hout = pl.pallas_call(
        functools.partial(_pool_kernel, start_pos=start_pos),
        grid=(nb, nc),
        in_specs=[pl.BlockSpec((t, BRANCH_W), row), pl.BlockSpec((None, 16, BRANCH_W), lambda b, c: (b, 0, 0)),
                  pl.BlockSpec((BRANCH_W, BRANCH_W), const), pl.BlockSpec((1, BRANCH_W), const)],
        out_specs=[pl.BlockSpec((t, BRANCH_W), row), pl.BlockSpec((None, 16, BRANCH_W), lambda b, c: (b, 0, 0))],
        out_shape=[jax.ShapeDtypeStruct((nb * rows_per_b, BRANCH_W), F32),
                   jax.ShapeDtypeStruct((nb, 16, BRANCH_W), F32)],
        scratch_shapes=[pltpu.VMEM((24 + t, BRANCH_W), F32)],
        compiler_params=_cparams(2),
        name="pool",
    )(u, halo16, wbd, scale)
    return o, hout[:, 1:, :]


def _merge_kernel(x_ref, ng_ref, wmg_ref, dsa_ref, diff_ref, ssd_ref, pool_ref, gates_ref, wbr_ref, wout_ref, y_ref):
    x = x_ref[...]
    h = x * lax.rsqrt(jnp.mean(x * x, axis=-1, keepdims=True) + EPS) * ng_ref[...]
    hb = h.astype(BF16)
    d = x.shape[1]
    br = (dsa_ref[...] * _silu(gates_ref[:, 0:256]),
          diff_ref[...] * _silu(gates_ref[:, 256:512]),
          ssd_ref[...],
          pool_ref[...] * _silu(gates_ref[:, 768:1024]))
    m = jnp.zeros(x.shape, F32)
    for n in range(N_BRANCH):
        mg = jnp.dot(hb, wmg_ref[:, n * d:(n + 1) * d], preferred_element_type=F32)
        up = jnp.dot(br[n].astype(BF16), wbr_ref[n], preferred_element_type=F32)
        m = m + _sigmoid(mg) * up
    y_ref[...] = x + jnp.dot(m.astype(BF16), wout_ref[...], preferred_element_type=F32)


def _merge(x, ng, wmg, dsa_o, diff_o, ssd_o, pool_o, gates, wbr, wout, tm):
    n, d = x.shape
    row = lambda i: (i, 0)
    const = lambda i: (0, 0)
    return pl.pallas_call(
        _merge_kernel,
        grid=(n // tm,),
        in_specs=[pl.BlockSpec((tm, d), row), pl.BlockSpec((1, d), const), pl.BlockSpec(wmg.shape, const),
                  pl.BlockSpec((tm, 256), row), pl.BlockSpec((tm, 256), row), pl.BlockSpec((tm, 256), row),
                  pl.BlockSpec((tm, 256), row), pl.BlockSpec((tm, 1024), row),
                  pl.BlockSpec(wbr.shape, lambda i: (0, 0, 0)), pl.BlockSpec(wout.shape, const)],
        out_specs=pl.BlockSpec((tm, d), row),
        out_shape=jax.ShapeDtypeStruct((n, d), F32),
        compiler_params=_cparams(1),
        name="merge",
    )(x, ng, wmg, dsa_o, diff_o, ssd_o, pool_o, gates, wbr, wout)


_PAGES_PER_STEP = 32
_KEY_CHUNK = 512
_Q_BLOCK = 128
_MERGE_ROWS = 256


def _page_specs(layer, features, npp):
    def spec(k):
        return pl.BlockSpec((None, None, features, PAGE_SIZE),
                            lambda b, g, pt, k=k: (layer, pt[b, g * npp + k], 0, 0))
    return [spec(k) for k in range(npp)]


def _pages_view(cache):
    depth, n_pool, page = cache.shape[:3]
    c = cache.reshape(depth, n_pool, page, -1)
    return jnp.swapaxes(c, 2, 3)


def _dsa_sidx_kernel(pt_ref, iq_ref, misc_ref, *rest, npp, n_pages, k_top):
    pages = rest[:npp]
    bias_ref = rest[npp]
    key_scr, qi_scr, jst_scr = rest[npp + 1:]
    t = iq_ref.shape[0]
    g = pl.program_id(1)
    last = pl.num_programs(1) - 1
    wts = misc_ref[:, 64:64 + IDX_HEADS]

    def scores(kt):
        s = jnp.dot(qi_scr[...], kt, preferred_element_type=F32)
        acc = None
        for h in range(IDX_HEADS):
            v = wts[:, h:h + 1] * jnp.maximum(s[h * t:(h + 1) * t, :], 0.0)
            acc = v if acc is None else acc + v
        return jnp.where(acc == 0.0, 0.0, acc)

    @pl.when(g == 0)
    def _():
        iq = iq_ref[...]
        for h in range(IDX_HEADS):
            qi_scr[h * t:(h + 1) * t, :] = iq[:, IDX_DIM * h:IDX_DIM * (h + 1)]
        knew = jnp.concatenate([misc_ref[...], jnp.zeros((PAGE_SIZE - t, LANES), F32)], axis=0)
        sc = scores(knew.T[0:IDX_DIM, :].astype(BF16))
        causal = lax.broadcasted_iota(I32, (t, LANES), 1) <= lax.broadcasted_iota(I32, (t, LANES), 0)
        key_scr[n_pages] = jnp.where(causal, _sortable(sc), INT_MIN)

    sc = scores(jnp.concatenate([pg[...].astype(BF16) for pg in pages], axis=1))
    for k in range(npp):
        key_scr[g * npp + k] = _sortable(sc[:, k * PAGE_SIZE:(k + 1) * PAGE_SIZE])

    @pl.when(g == last)
    def _():
        keys = key_scr[...]
        kpos = (lax.broadcasted_iota(I32, keys.shape, 0) * PAGE_SIZE + lax.broadcasted_iota(I32, keys.shape, 2))

        def count(m):
            return jnp.sum(jnp.sum(m, axis=0), axis=1, keepdims=True)

        kf = float(k_top)
        c0 = count(jnp.where(keys >= 0, 1.0, 0.0))
        cand0 = jnp.where(c0 >= kf, 0, INT_MIN).astype(I32)

        def bit_body(b, cand):
            trial = cand | jnp.left_shift(jnp.int32(1), 30 - b)
            cnt = count(jnp.where(keys >= trial[None], 1.0, 0.0))
            return jnp.where(cnt >= kf, trial, cand)

        thr = lax.fori_loop(0, 31, bit_body, cand0)
        cnt_gt = count(jnp.where(keys > thr[None], 1.0, 0.0))
        cnt_ge = count(jnp.where(keys >= thr[None], 1.0, 0.0))
        need = jnp.where(cnt_ge > kf, jnp.where(thr > INT_MIN, 1.0, 0.0), 0.0)
        rem = kf - cnt_gt
        big = jnp.int32(2 ** 30)
        jst_scr[...] = jnp.full((t, 1), big, I32)

        @pl.when(jnp.max(need) > 0.0)
        def _():
            nbits = ((n_pages + 1) * PAGE_SIZE - 1).bit_length()

            def jb(b, pos):
                trial = pos | jnp.left_shift(jnp.int32(1), nbits - 1 - b)
                cnt = count(jnp.where(keys == thr[None], jnp.where(kpos < trial[None], 1.0, 0.0), 0.0))
                return jnp.where(cnt < rem, trial, pos)

            pos = lax.fori_loop(0, nbits, jb, jnp.zeros((t, 1), I32))
            jst_scr[...] = jnp.where(need > 0.0, pos, big)

        jst = jst_scr[...]
        sel = jnp.where(keys > thr[None], 0.0,
                        jnp.where(keys == thr[None], jnp.where(kpos <= jst[None], 0.0, NEG), NEG))
        newc = lax.broadcasted_iota(I32, keys.shape, 0) == n_pages
        causal = lax.broadcasted_iota(I32, keys.shape, 2) <= lax.broadcasted_iota(I32, keys.shape, 1)
        bias_ref[...] = jnp.where(newc, jnp.where(causal, sel, NEG), sel)


def _dsa_sidx(ps, cache_ik, layer, page_table, bs, t):
    n_pages = page_table.shape[1]
    npp = math.gcd(n_pages, _PAGES_PER_STEP)
    k_top = min(IDX_TOPK_MAX, (n_pages * PAGE_SIZE + t) // 4)
    row = lambda b, g, pt: (b, 0)
    grid_spec = pltpu.PrefetchScalarGridSpec(
        num_scalar_prefetch=1, grid=(bs, n_pages // npp),
        in_specs=[pl.BlockSpec((t, 256), row), pl.BlockSpec((t, 128), row)] + _page_specs(layer, IDX_DIM, npp),
        out_specs=pl.BlockSpec((None, n_pages + 1, t, LANES), lambda b, g, pt: (b, 0, 0, 0)),
        scratch_shapes=[pltpu.VMEM((n_pages + 1, t, LANES), I32), pltpu.VMEM((IDX_HEADS * t, IDX_DIM), BF16),
                        pltpu.VMEM((t, 1), I32)])
    return pl.pallas_call(
        functools.partial(_dsa_sidx_kernel, npp=npp, n_pages=n_pages, k_top=k_top),
        grid_spec=grid_spec,
        out_shape=jax.ShapeDtypeStruct((bs, n_pages + 1, t, LANES), F32),
        compiler_params=_cparams(2),
        name="dsa_sample_index",
    )(page_table, ps["iq"], ps["misc"], *([cache_ik] * npp))


def _paged_attn_kernel(pt_ref, q_ref, kn_ref, vn_ref, *rest, npp, nheads, log2_hd, use_bias, finish):
    idx = 0
    if use_bias:
        bias_new_ref, bias_pg_ref = rest[0], rest[1]
        idx = 2
    kpages = rest[idx:idx + npp]
    vpages = rest[idx + npp:idx + 2 * npp]
    n_scr = 7
    extra = rest[idx + 2 * npp:-(n_scr + 1)]
    o_ref = rest[-(n_scr + 1)]
    qa_scr, m_scr, l_scr, alpha_scr, acc_scr, p_scr, s_scr = rest[-n_scr:]
    t = q_ref.shape[0]
    rows = nheads * t
    g = pl.program_id(1)
    last = pl.num_programs(1) - 1

    def tile_rows(b):
        return jnp.concatenate([b] * nheads, axis=0)

    def accumulate(pv):
        alpha = alpha_scr[...]
        acc_scr[...] = jnp.concatenate([alpha, alpha], axis=1) * acc_scr[...] + pv

    @pl.when(g == 0)
    def _():
        qv = q_ref[...]
        grp = _group_of_lane((t, BRANCH_W), log2_hd)
        for h in range(nheads):
            qa_scr[h * t:(h + 1) * t, :] = jnp.where(grp == h, qv, jnp.zeros_like(qv))
        m_scr[...] = jnp.full((rows, LANES), NEG, F32)
        l_scr[...] = jnp.zeros((rows, LANES), F32)
        acc_scr[...] = jnp.zeros((rows, BRANCH_W), F32)
        kn = jnp.concatenate([kn_ref[...], jnp.zeros((PAGE_SIZE - t, BRANCH_W), F32)], axis=0)
        vn = jnp.concatenate([vn_ref[...], jnp.zeros((PAGE_SIZE - t, BRANCH_W), BF16)], axis=0)
        if use_bias:
            bias = bias_new_ref[0]
        else:
            causal = lax.broadcasted_iota(I32, (t, LANES), 1) <= lax.broadcasted_iota(I32, (t, LANES), 0)
            bias = jnp.where(causal, 0.0, NEG)
        s_new = s_scr.at[:, 0:PAGE_SIZE]
        p_new = p_scr.at[:, 0:PAGE_SIZE]
        s_new[...] = jnp.dot(qa_scr[...], kn.T.astype(BF16), preferred_element_type=F32)
        _flash_rows(s_new, tile_rows(bias), m_scr, l_scr, alpha_scr, p_new, 0, rows)
        accumulate(jnp.dot(p_new[...], vn, preferred_element_type=F32))

    kt = jnp.concatenate([kp[...].astype(BF16) for kp in kpages], axis=1)
    vt = jnp.concatenate([vp[...].astype(BF16) for vp in vpages], axis=1)
    s_scr[...] = jnp.dot(qa_scr[...], kt, preferred_element_type=F32)
    bias = None
    if use_bias:
        bias = tile_rows(jnp.concatenate([bias_pg_ref[k] for k in range(npp)], axis=1))
    _flash_rows(s_scr, bias, m_scr, l_scr, alpha_scr, p_scr, 0, rows)
    accumulate(_nk_dot(p_scr[...], vt))

    @pl.when(g == last)
    def _():
        o_ref[...] = finish(acc_scr, l_scr, t, *extra)


def _dsa_finish(acc_ref, l_ref, t):
    grp = _group_of_lane((t, BRANCH_W), 6)
    out = jnp.zeros((t, BRANCH_W), F32)
    for h in range(DSA_HEADS):
        l_row = jnp.sum(l_ref[h * t:(h + 1) * t, :], axis=1, keepdims=True)
        out = out + jnp.where(grp == h, acc_ref[h * t:(h + 1) * t, :] / l_row, 0.0)
    return out


def _diff_sample_finish(acc_ref, l_ref, t, lam_ref, gsub_ref, *, lam_init):
    return _diff_finish(acc_ref, l_ref, t, _diff_lambda(lam_ref, lam_init), lam_init, gsub_ref[...])


def _paged_attn(q, kn, vn, cache_k, cache_v, layer, page_table, bs, t, *, nheads, log2_hd, bias=None,
                extra=(), finish, name):
    n_pages = page_table.shape[1]
    npp = math.gcd(n_pages, _PAGES_PER_STEP)
    row = lambda b, g, pt: (b, 0)
    const = lambda b, g, pt: (0, 0)
    in_specs = [pl.BlockSpec((t, 256), row), pl.BlockSpec((t, 256), row), pl.BlockSpec((t, 256), row)]
    args = [q, kn, vn]
    if bias is not None:
        in_specs += [pl.BlockSpec((None, 1, t, LANES), lambda b, g, pt: (b, n_pages, 0, 0)),
                     pl.BlockSpec((None, npp, t, LANES), lambda b, g, pt: (b, g, 0, 0))]
        args += [bias, bias]
    in_specs += _page_specs(layer, 256, npp) + _page_specs(layer, 256, npp)
    args += [cache_k] * npp + [cache_v] * npp
    for e in extra:
        in_specs.append(pl.BlockSpec(e.shape, const))
        args.append(e)
    rows = nheads * t
    grid_spec = pltpu.PrefetchScalarGridSpec(
        num_scalar_prefetch=1, grid=(bs, n_pages // npp), in_specs=in_specs,
        out_specs=pl.BlockSpec((t, 256), row),
        scratch_shapes=[pltpu.VMEM((rows, 256), BF16), pltpu.VMEM((rows, LANES), F32),
                        pltpu.VMEM((rows, LANES), F32), pltpu.VMEM((rows, LANES), F32),
                        pltpu.VMEM((rows, 256), F32), pltpu.VMEM((rows, npp * PAGE_SIZE), BF16),
                        pltpu.VMEM((rows, npp * PAGE_SIZE), F32)])
    return pl.pallas_call(
        functools.partial(_paged_attn_kernel, npp=npp, nheads=nheads, log2_hd=log2_hd,
                          use_bias=bias is not None, finish=finish),
        grid_spec=grid_spec,
        out_shape=jax.ShapeDtypeStruct((bs * t, BRANCH_W), F32),
        compiler_params=_cparams(2),
        name=name,
    )(page_table, *args)


def _rope_table(pos, half, reps, extra_cos=None):
    inv = ROPE_THETA ** (-jnp.arange(half, dtype=F32) / half)
    ang = pos.astype(F32)[:, None] * inv[None, :]
    cos, sin = jnp.cos(ang), jnp.sin(ang)
    cos_h = jnp.concatenate([cos, cos], axis=1)
    sin_h = jnp.concatenate([-sin, sin], axis=1)
    cos_t, sin_t = jnp.tile(cos_h, (1, reps)), jnp.tile(sin_h, (1, reps))
    if extra_cos is not None:
        n = pos.shape[0]
        cos_t = jnp.concatenate([cos_t, jnp.broadcast_to(extra_cos[None, :], (n, extra_cos.shape[0]))], axis=1)
        sin_t = jnp.concatenate([sin_t, jnp.zeros((n, extra_cos.shape[0]), F32)], axis=1)
    return jnp.concatenate([cos_t, sin_t], axis=1)


def _tables(pos):
    misc_scale = jnp.concatenate([jnp.full((IDX_HEADS,), IDX_HEADS ** -0.5, F32),
                                  jnp.ones((LANES - IDX_DIM - IDX_HEADS,), F32)])
    return (_rope_table(pos, DSA_HEAD_DIM // 2, 4), _rope_table(pos, DIFF_HEAD_DIM // 2, 8),
            _rope_table(pos, IDX_DIM // 2, 1, misc_scale))


def _pack_w_in(w):
    sizes = (256, 256, 256, 256, 256, 64, 4, 256, 256, 256, 256, 256, 512, 4, 256, 256, 4096)
    offs = [0]
    for s in sizes:
        offs.append(offs[-1] + s)
    (dq, dk, dv, dg, iq, ik, iw, fq, fk, fv, fg, sz, sxbc, sdt, pu, pg, mg) = [
        w[:, offs[i]:offs[i + 1]] for i in range(len(sizes))]
    d = w.shape[0]
    misc = jnp.concatenate([ik, iw, jnp.zeros((d, LANES - IDX_DIM - IDX_HEADS), w.dtype)], axis=1)
    dts = jnp.concatenate([sdt, jnp.zeros((d, LANES - SSD_HEADS), w.dtype)], axis=1)
    packed = jnp.concatenate([dq, dk, dv, iq, fq, fk, fv, dg, fg, sz, pg, sxbc, pu, misc, dts], axis=1)
    return packed.astype(BF16), mg.astype(BF16)


def _block_diag(w):
    g, n, _ = w.shape
    out = jnp.zeros((g * n, g * n), w.dtype)
    for i in range(g):
        out = out.at[i * n:(i + 1) * n, i * n:(i + 1) * n].set(w[i])
    return out


def kernel(x_prompt, x_sample, cache_dsa_k, cache_dsa_v, cache_idx_k, cache_diff_k, cache_diff_v, state_ssm,
           state_conv, state_pool, page_table, norm_g, w_in, dsa_qk_g, diff_qk_g, diff_lam, diff_subln,
           ssd_conv_w, ssd_conv_b, ssd_dt_bias, ssd_a_log, ssd_d, ssd_norm, pool_w, pool_scale, w_branch, w_out):
    bp, seq, d = x_prompt.shape
    bs, t, _ = x_sample.shape
    depth = norm_g.shape[0]
    n_pages = page_table.shape[1]
    past = n_pages * PAGE_SIZE
    n_pool = cache_dsa_k.shape[1]

    tab_p = _tables(jnp.arange(seq, dtype=I32))
    tab_s = tuple(jnp.tile(a, (bs, 1)) for a in _tables(past + jnp.arange(t, dtype=I32)))
    ck, cv, cik = _pages_view(cache_dsa_k), _pages_view(cache_dsa_v), _pages_view(cache_idx_k)
    cfk, cfv = _pages_view(cache_diff_k), _pages_view(cache_diff_v)

    tm = min(_KEY_CHUNK, seq)
    tq = min(_Q_BLOCK, seq)
    tm_s = min(256, bs * t)
    pool_t = min(512, seq)
    xp = x_prompt.reshape(bp * seq, d)
    xs = x_sample.reshape(bs * t, d)
    acc_p = {n: [] for n in ("dk", "dv", "ik", "fk", "fv", "ssm", "conv", "pool")}
    acc_s = {n: [] for n in acc_p}

    for l in range(depth):
        lam_init = 0.8 - 0.6 * math.exp(-0.3 * l)
        wp, wmg = _pack_w_in(w_in[l])
        ng = norm_g[l].reshape(1, d)
        gains = jnp.stack([jnp.tile(dsa_qk_g[l, 0], 4), jnp.tile(dsa_qk_g[l, 1], 4),
                           jnp.tile(diff_qk_g[l, 0], 8), jnp.tile(diff_qk_g[l, 1], 8)])
        gsub = jnp.tile(diff_subln[l], 4).reshape(1, BRANCH_W)
        lam_p = diff_lam[l]
        ssd_par = jnp.concatenate([ssd_dt_bias[l], ssd_a_log[l], ssd_d[l]]).astype(F32)
        cw, cb = ssd_conv_w[l], ssd_conv_b[l].reshape(1, SSD_CONV_DIM)
        sng = ssd_norm[l].reshape(1, BRANCH_W)
        wbd = _block_diag(pool_w[l]).astype(BF16)
        pscale = pool_scale[l].reshape(1, BRANCH_W)
        wbr = w_branch[l].astype(BF16)
        wout = w_out[l].astype(BF16)

        pp = _proj(xp, ng, wp, *tab_p, gains, tm, seq // tm)
        dsa_o = _dsa_prompt(pp, bp, seq, tq)
        diff_o = _diff_prompt(pp, lam_p, gsub, lam_init, bp, seq, tq)
        ssd_o, ssm_new, conv_new = _ssd(pp, ssd_par, jnp.zeros((bp, 8, SSD_CONV_DIM), F32),
                                        jnp.zeros((bp, SSD_HEADS, SSD_HEAD_DIM, SSD_STATE), F32), cw, cb, sng,
                                        bp, seq, SSD_CHUNK)
        pool_o, pool_new = _pool(pp["pu"], jnp.zeros((bp, 16, BRANCH_W), F32), wbd, pscale, bp, seq, pool_t, 0)
        xp = _merge(xp, ng, wmg, dsa_o, diff_o, ssd_o, pool_o, pp["gates"], wbr, wout, min(_MERGE_ROWS, seq))
        for n, v in zip(acc_p, (pp["dk"].reshape(bp, seq, DSA_HEADS, DSA_HEAD_DIM),
                                pp["dv"].reshape(bp, seq, DSA_HEADS, DSA_HEAD_DIM),
                                pp["misc"][:, :IDX_DIM].reshape(bp, seq, IDX_DIM),
                                pp["fk"].reshape(bp, seq, 2 * DIFF_HEADS, DIFF_HEAD_DIM),
                                pp["fv"].reshape(bp, seq, DIFF_HEADS, DIFF_V_DIM),
                                ssm_new, conv_new, pool_new)):
            acc_p[n].append(v)

        ps = _proj(xs, ng, wp, *tab_s, gains, tm_s, (bs * t) // tm_s)
        bias = _dsa_sidx(ps, cik, l, page_table, bs, t)
        dsa_o = _paged_attn(ps["dq"], ps["dk"], ps["dvb"], ck, cv, l, page_table, bs, t,
                            nheads=DSA_HEADS, log2_hd=6, bias=bias, finish=_dsa_finish, name="dsa_sample_attn")
        diff_o = _paged_attn(ps["fq"], ps["fk"], ps["fvb"], cfk, cfv, l, page_table, bs, t,
                             nheads=2 * DIFF_HEADS, log2_hd=5, extra=(lam_p, gsub),
                             finish=functools.partial(_diff_sample_finish, lam_init=lam_init),
                             name="diff_sample_attn")
        cprev8 = jnp.concatenate([jnp.zeros((bs, 8 - (SSD_CONV - 1), SSD_CONV_DIM), F32), state_conv[l]], axis=1)
        ssd_o, ssm_new, conv_new = _ssd(ps, ssd_par, cprev8, state_ssm[l], cw, cb, sng, bs, t, t)
        halo = jnp.concatenate([jnp.zeros((bs, 1, BRANCH_W), F32), state_pool[l]], axis=1)
        pool_o, pool_new = _pool(ps["pu"], halo, wbd, pscale, bs, t, t, past)
        xs = _merge(xs, ng, wmg, dsa_o, diff_o, ssd_o, pool_o, ps["gates"], wbr, wout, tm_s)
        for n, v in zip(acc_s, (ps["dk"].reshape(bs, t, DSA_HEADS, DSA_HEAD_DIM),
                                ps["dv"].reshape(bs, t, DSA_HEADS, DSA_HEAD_DIM),
                                ps["misc"][:, :IDX_DIM].reshape(bs, t, IDX_DIM),
                                ps["fk"].reshape(bs, t, 2 * DIFF_HEADS, DIFF_HEAD_DIM),
                                ps["fv"].reshape(bs, t, DIFF_HEADS, DIFF_V_DIM),
                                ssm_new, conv_new, pool_new)):
            acc_s[n].append(v)

    names = ("dk", "dv", "ik", "fk", "fv", "ssm", "conv", "pool")
    return ((xp.reshape(bp, seq, d), xs.reshape(bs, t, d))
            + tuple(jnp.stack(acc_p[n]) for n in names) + tuple(jnp.stack(acc_s[n]) for n in names))
```

```python
import functools
import math

import jax
import jax.numpy as jnp
from jax import lax
from jax.experimental import pallas as pl
from jax.experimental.pallas import tpu as pltpu

F32 = jnp.float32
BF16 = jnp.bfloat16
I32 = jnp.int32

BRANCH_W = 256
DSA_HEADS = 4
DSA_HEAD_DIM = 64
IDX_HEADS = 4
IDX_DIM = 64
IDX_TOPK_MAX = 256
DIFF_HEADS = 4
DIFF_HEAD_DIM = 32
DIFF_V_DIM = 64
SSD_HEADS = 4
SSD_HEAD_DIM = 64
SSD_STATE = 64
SSD_GROUPS = 2
SSD_CONV = 4
SSD_CHUNK = 128
SSD_CONV_DIM = 512
POOL_WINDOWS = (2, 4, 8, 16)
POOL_STATE = 15
PAGE_SIZE = 128
ROPE_THETA = 10000.0
EPS = 1e-6
N_BRANCH = 4

LANES = 128
SUBLANES = 8
VMEM_LIMIT = 56 * 1024 * 1024

LOG2E = math.log2(math.e)
NEG = -1e30
INT_MIN = -(2 ** 31)
HIGHEST = lax.Precision.HIGHEST

_SLABS = (("dq", 256), ("dk", 256), ("dv", 256), ("iq", 256), ("fq", 256), ("fk", 256), ("fv", 256),
          ("gates", 1024), ("sxbc", 512), ("pu", 256), ("misc", 128), ("dts", 128))
_OFF = {}
_o = 0
for _n, _w in _SLABS:
    _OFF[_n] = (_o, _o + _w)
    _o += _w
W_PACKED = _o


def _cparams(n_axes, vmem=VMEM_LIMIT):
    return pltpu.CompilerParams(dimension_semantics=("arbitrary",) * n_axes, vmem_limit_bytes=vmem)


def _nk_dot(a, b):
    return lax.dot_general(a, b, (((1,), (1,)), ((), ())), preferred_element_type=F32)


def _silu(x):
    return x * (1.0 / (1.0 + jnp.exp(-x)))


def _sigmoid(x):
    return 1.0 / (1.0 + jnp.exp(-x))


def _group_of_lane(shape, log2_width):
    return lax.broadcasted_iota(I32, shape, len(shape) - 1) >> log2_width


def _proj_kernel(x_ref, ng_ref, w_ref, t64_ref, t32_ref, tmisc_ref, gains_ref,
                 dq_ref, dk_ref, dv_ref, dvb_ref, iq_ref, misc_ref, fq_ref, fk_ref, fv_ref, fvb_ref,
                 gates_ref, sxbc_ref, pu_ref, dts_ref, dkt_ref, ikt_ref, fkt_ref):
    x = x_ref[...]
    h = x * lax.rsqrt(jnp.mean(x * x, axis=-1, keepdims=True) + EPS) * ng_ref[...]
    hb = h.astype(BF16)

    def mm(name):
        a, b = _OFF[name]
        return jnp.dot(hb, w_ref[:, a:b], preferred_element_type=F32)

    def head_norm(z, g, log2_hd):
        n = z.shape[1]
        r = lax.broadcasted_iota(I32, (n, n), 0) >> log2_hd
        c = lax.broadcasted_iota(I32, (n, n), 1) >> log2_hd
        bd = jnp.where(r == c, 1.0 / (1 << log2_hd), 0.0).astype(F32)
        ms = jnp.dot(z * z, bd, preferred_element_type=F32, precision=HIGHEST)
        return z * lax.rsqrt(ms + EPS) * g

    def rope(z, tab_ref, half):
        n = z.shape[1]
        cos = tab_ref[:, :n]
        sin = tab_ref[:, n:]
        lane = lax.broadcasted_iota(I32, z.shape, 1)
        first = (lane & (2 * half - 1)) < half
        partner = jnp.where(first, pltpu.roll(z, n - half, 1), pltpu.roll(z, half, 1))
        return z * cos + partner * sin

    dq = rope(head_norm(mm("dq"), gains_ref[0:1, :], 6), t64_ref, 32)
    dq_ref[...] = (dq * (DSA_HEAD_DIM ** -0.5 * LOG2E)).astype(BF16)
    dk = rope(head_norm(mm("dk"), gains_ref[1:2, :], 6), t64_ref, 32)
    dk_ref[...] = dk
    dkt_ref[...] = dk.T.astype(BF16)
    dv = mm("dv")
    dv_ref[...] = dv
    dvb_ref[...] = dv.astype(BF16)
    iq_ref[...] = (rope(mm("iq"), t64_ref, 32) * (IDX_DIM ** -0.5)).astype(BF16)
    misc = rope(mm("misc"), tmisc_ref, 32)
    misc_ref[...] = misc
    ikt_ref[...] = misc.T[0:IDX_DIM, :].astype(BF16)
    fq = rope(head_norm(mm("fq"), gains_ref[2:3, :], 5), t32_ref, 16)
    fq_ref[...] = (fq * (DIFF_HEAD_DIM ** -0.5 * LOG2E)).astype(BF16)
    fk = rope(head_norm(mm("fk"), gains_ref[3:4, :], 5), t32_ref, 16)
    fk_ref[...] = fk
    fkt_ref[...] = fk.T.astype(BF16)
    fv = mm("fv")
    fv_ref[...] = fv
    fvb_ref[...] = fv.astype(BF16)
    for j in range(4):
        a = _OFF["gates"][0] + 256 * j
        gates_ref[:, 256 * j:256 * (j + 1)] = jnp.dot(hb, w_ref[:, a:a + 256], preferred_element_type=F32)
    for j in range(2):
        a = _OFF["sxbc"][0] + 256 * j
        sxbc_ref[:, 256 * j:256 * (j + 1)] = jnp.dot(hb, w_ref[:, a:a + 256], preferred_element_type=F32)
    pu_ref[...] = mm("pu")
    dts_ref[...] = mm("dts")


_PROJ_OUTS = (("dq", 256, BF16), ("dk", 256, F32), ("dv", 256, F32), ("dvb", 256, BF16), ("iq", 256, BF16),
              ("misc", 128, F32), ("fq", 256, BF16), ("fk", 256, F32), ("fv", 256, F32), ("fvb", 256, BF16),
              ("gates", 1024, F32), ("sxbc", 512, F32), ("pu", 256, F32), ("dts", 128, F32))
_PROJ_T_OUTS = (("dkT", 256), ("ikT", IDX_DIM), ("fkT", 256))


def _proj(x, ng, wp, t64, t32, tmisc, gains, tm, tab_blocks):
    n, d = x.shape
    row = lambda i: (i, 0)
    tab = lambda i: (i % tab_blocks, 0)
    const = lambda i: (0, 0)
    outs = _PROJ_OUTS
    res = pl.pallas_call(
        _proj_kernel,
        grid=(n // tm,),
        in_specs=[pl.BlockSpec((tm, d), row), pl.BlockSpec((1, d), const), pl.BlockSpec(wp.shape, const),
                  pl.BlockSpec((tm, 512), tab), pl.BlockSpec((tm, 512), tab), pl.BlockSpec((tm, 256), tab),
                  pl.BlockSpec((4, 256), const)],
        out_specs=([pl.BlockSpec((tm, w), row) for _, w, _ in outs]
                   + [pl.BlockSpec((None, w, tm), lambda i: (i, 0, 0)) for _, w in _PROJ_T_OUTS]),
        out_shape=([jax.ShapeDtypeStruct((n, w), dt) for _, w, dt in outs]
                   + [jax.ShapeDtypeStruct((n // tm, w, tm), BF16) for _, w in _PROJ_T_OUTS]),
        compiler_params=_cparams(1),
        name="proj",
    )(x, ng, wp, t64, t32, tmisc, gains)
    return dict(zip([o[0] for o in outs] + [o[0] for o in _PROJ_T_OUTS], res))


def _sortable(x):
    bits = pltpu.bitcast(x, I32)
    return bits ^ ((bits >> 31) & 0x7FFFFFFF)


def _flash_rows(s_ref, bias, m_ref, l_ref, alpha_ref, p_ref, r0, r1):
    c_sz = s_ref.shape[1]
    nq = c_sz // LANES
    parts = []
    for q in range(nq):
        v = s_ref[r0:r1, q * LANES:(q + 1) * LANES]
        parts.append(v if bias is None else v + bias[:, q * LANES:(q + 1) * LANES])
    mx = parts[0]
    for q in range(1, nq):
        mx = jnp.maximum(mx, parts[q])
    m_old = m_ref[r0:r1, :]
    m_new = jnp.maximum(m_old, jnp.max(mx, axis=1, keepdims=True))
    alpha = jnp.exp2(m_old - m_new)
    m_ref[r0:r1, :] = m_new
    alpha_ref[r0:r1, :] = alpha
    lsum = None
    for q in range(nq):
        p = jnp.exp2(parts[q] - m_new)
        p_ref[r0:r1, q * LANES:(q + 1) * LANES] = p.astype(BF16)
        lsum = p if lsum is None else lsum + p
    l_ref[r0:r1, :] = alpha * l_ref[r0:r1, :] + lsum


def _flash_loop(n, qa_ref, kt_ref, v_ref, bias_fn, nh, tq, s_scr, p_scr, alpha_scr, m_scr, l_scr, acc_scr):
    c_sz = kt_ref.shape[2]
    rows = nh * tq
    m_scr[...] = jnp.full((rows, LANES), NEG, F32)
    l_scr[...] = jnp.zeros((rows, LANES), F32)
    acc_scr[...] = jnp.zeros((rows, BRANCH_W), F32)

    def scores(c, slot):
        s_scr[slot] = jnp.dot(qa_ref[...], kt_ref[c], preferred_element_type=F32)

    def softmax(c, slot):
        bias = bias_fn(c)
        for h in range(nh):
            _flash_rows(s_scr.at[slot], bias, m_scr, l_scr, alpha_scr.at[slot], p_scr.at[slot],
                        h * tq, (h + 1) * tq)

    def values(c, slot):
        start = pl.multiple_of(c * c_sz, c_sz)
        pv = jnp.dot(p_scr[slot], v_ref[pl.ds(start, c_sz), :], preferred_element_type=F32)
        alpha = alpha_scr[slot]
        acc_scr[...] = jnp.concatenate([alpha, alpha], axis=1) * acc_scr[...] + pv

    scores(0, 0)
    scores(jnp.minimum(1, n - 1), 1)
    softmax(0, 0)

    def body(j, carry):
        c = 2 * j + 1
        scores(jnp.minimum(c + 1, n - 1), 0)
        values(c - 1, 0)
        softmax(c, 1)
        scores(jnp.minimum(c + 2, n - 1), 1)
        values(c, 1)
        softmax(c + 1, 0)
        return carry

    lax.fori_loop(0, (n - 1) // 2, body, 0)

    @pl.when((n - 1) % 2 == 1)
    def _():
        values(n - 2, 0)
        softmax(n - 1, 1)

    values(n - 1, (n - 1) & 1)


def _diff_lambda(lam_ref, lam_init):
    lp = lam_ref[...]
    s1 = jnp.sum(lp[0:1, :] * lp[1:2, :], axis=1, keepdims=True)
    s2 = jnp.sum(lp[2:3, :] * lp[3:4, :], axis=1, keepdims=True)
    return jnp.exp(s1) - jnp.exp(s2) + lam_init


def _diff_finish(acc_ref, l_ref, rows, lam, lam_init, gsub):
    grp = _group_of_lane((rows, BRANCH_W), 6)
    out = jnp.zeros((rows, BRANCH_W), F32)
    for j in range(DIFF_HEADS):
        r1, r2 = 2 * j * rows, (2 * j + 1) * rows
        a1 = acc_ref[r1:r1 + rows, :] / jnp.sum(l_ref[r1:r1 + rows, :], axis=1, keepdims=True)
        a2 = acc_ref[r2:r2 + rows, :] / jnp.sum(l_ref[r2:r2 + rows, :], axis=1, keepdims=True)
        o = a1 - lam * a2
        ms = jnp.sum(jnp.where(grp == j, o * o, 0.0), axis=1, keepdims=True) * (1.0 / DIFF_V_DIM)
        out = out + jnp.where(grp == j, o * lax.rsqrt(ms + EPS), 0.0)
    return out * gsub * (1.0 - lam_init)


def _dsa_prompt_kernel(iq_ref, misc_ref, dq_ref, ik_ref, dk_ref, dv_ref, o_ref,
                       key_scr, hb_scr, qi_scr, qa_scr, m_scr, l_scr, alpha_scr, acc_scr, p_scr, s_scr, *, k_top):
    tq = iq_ref.shape[0]
    c_sz = ik_ref.shape[2]
    seq = ik_ref.shape[0] * c_sz
    i = pl.program_id(1)
    nch = ((i + 1) * tq + c_sz - 1) // c_sz
    qpos = i * tq + lax.broadcasted_iota(I32, (tq, 1), 0)
    lane_c = lax.broadcasted_iota(I32, (tq, c_sz), 1)

    iq = iq_ref[...]
    dq = dq_ref[...]
    grp = _group_of_lane((tq, BRANCH_W), 6)
    for h in range(IDX_HEADS):
        qi_scr[h * tq:(h + 1) * tq, :] = iq[:, IDX_DIM * h:IDX_DIM * (h + 1)]
        qa_scr[h * tq:(h + 1) * tq, :] = jnp.where(grp == h, dq, jnp.zeros_like(dq))
    wts = misc_ref[:, 64:64 + IDX_HEADS]

    def p1(c, carry):
        s_scr[0] = jnp.dot(qi_scr[...], ik_ref[c], preferred_element_type=F32)
        acc = None
        for h in range(IDX_HEADS):
            t = wts[:, h:h + 1] * jnp.maximum(s_scr[0, h * tq:(h + 1) * tq, :], 0.0)
            acc = t if acc is None else acc + t
        acc = jnp.where(acc == 0.0, 0.0, acc)
        causal = c * c_sz + lane_c <= qpos
        key_scr[c] = jnp.where(causal, _sortable(acc), INT_MIN)
        hi = pltpu.bitcast(pltpu.bitcast(acc, I32) & -65536, F32)
        hb_scr[c] = jnp.where(causal, hi, -jnp.inf).astype(BF16)
        return carry

    lax.fori_loop(0, nch, p1, 0)

    grp_sz = math.gcd(_COUNT_GROUP, seq // c_sz)
    ngrp = (nch + grp_sz - 1) // grp_sz

    def fill(c, carry):
        key_scr[c] = jnp.full((tq, c_sz), INT_MIN, I32)
        hb_scr[c] = jnp.full((tq, c_sz), -jnp.inf, BF16)
        return carry

    lax.fori_loop(nch, ngrp * grp_sz, fill, 0)

    def count_hi(trial_hi):
        tb16 = (trial_hi ^ ((trial_hi >> 31) & 0x7FFF)) & 0xFFFF
        tb = pltpu.bitcast(tb16 << 16, F32).astype(BF16)
        tb = jnp.broadcast_to(tb, (tq, LANES))

        one_b = jnp.ones((tq, LANES), BF16)
        zero_b = jnp.zeros((tq, LANES), BF16)

        def body(g, cnt):
            for u in range(grp_sz):
                blk = hb_scr[g * grp_sz + u]
                for q in range(c_sz // LANES):
                    cnt = cnt + jnp.where(blk[:, q * LANES:(q + 1) * LANES] >= tb, one_b, zero_b)
            return cnt

        cnt = lax.fori_loop(0, ngrp, body, jnp.zeros((tq, LANES), BF16))
        return jnp.sum(cnt.astype(F32), axis=1, keepdims=True)

    def count(pred):
        def body(g, cnt):
            for u in range(grp_sz):
                c = g * grp_sz + u
                m = pred(key_scr[c], c)
                part = m[:, 0:LANES]
                for q in range(1, c_sz // LANES):
                    part = part + m[:, q * LANES:(q + 1) * LANES]
                cnt = cnt + part
            return cnt
        cnt = lax.fori_loop(0, ngrp, body, jnp.zeros((tq, LANES), F32))
        return jnp.sum(cnt, axis=1, keepdims=True)

    kf = float(k_top)
    short = qpos + 1 < k_top
    c0 = count(lambda k, c: jnp.where(k >= 0, 1.0, 0.0))
    hi0 = jnp.where(c0 >= kf, 0, -32768).astype(I32)
    cnt0 = jnp.where(c0 >= kf, c0, 2.0 ** 30)

    def hi_body(exact):
        def body(b, st):
            cand_hi, cnt_c = st
            trial = cand_hi | jnp.left_shift(jnp.int32(1), 14 - b)
            if exact:
                trial32 = jnp.left_shift(trial, 16)
                cnt = count(lambda k, c: jnp.where(k >= trial32, 1.0, 0.0))
            else:
                cnt = count_hi(trial)
            return jnp.where(cnt >= kf, trial, cand_hi), jnp.where(cnt >= kf, cnt, cnt_c)
        return body

    hi0, cnt0 = lax.fori_loop(0, 8, hi_body(False), (hi0, cnt0))
    near_zero = jnp.max(jnp.where(hi0 == 0, 1.0, jnp.where(hi0 == -128, 1.0, 0.0))) > 0.0
    hi0, cnt0 = lax.cond(near_zero,
                         lambda st: lax.fori_loop(8, 15, hi_body(True), st),
                         lambda st: lax.fori_loop(8, 15, hi_body(False), st), (hi0, cnt0))
    cand0 = jnp.left_shift(hi0, 16)

    def refine(shift, cand, cnt_c):
        trial = cand | jnp.left_shift(jnp.int32(1), shift)
        cnt = count(lambda k, c: jnp.where(k >= trial, 1.0, 0.0))
        return jnp.where(cnt >= kf, trial, cand), jnp.where(cnt >= kf, cnt, cnt_c)

    def unresolved(cnt_c):
        return (jnp.max(jnp.where(short, 0.0, jnp.where(cnt_c == kf, 0.0, 1.0))) > 0.0).astype(I32)

    def bits_cond(st):
        return jnp.logical_and(st[0] < 8, st[3] > 0)

    def bits_body(st):
        j, cand, cnt_c, _ = st
        cand, cnt_c = refine(15 - 2 * j, cand, cnt_c)
        cand, cnt_c = refine(14 - 2 * j, cand, cnt_c)
        return j + 1, cand, cnt_c, unresolved(cnt_c)

    _, thr, cnt_ge, _ = lax.while_loop(bits_cond, bits_body, (jnp.int32(0), cand0, cnt0, unresolved(cnt0)))
    need = jnp.where(short, 0.0, jnp.where(cnt_ge > kf, 1.0, 0.0))

    @pl.when(jnp.max(need) > 0.0)
    def _():
        rem = kf - count(lambda k, c: jnp.where(k > thr, 1.0, 0.0))
        nbits = max(1, (seq - 1).bit_length())

        def jb(b, pos):
            trial = pos | jnp.left_shift(jnp.int32(1), nbits - 1 - b)
            cnt = count(lambda k, c: jnp.where(k == thr, jnp.where(c * c_sz + lane_c < trial, 1.0, 0.0), 0.0))
            return jnp.where(cnt < rem, trial, pos)

        pos = lax.fori_loop(0, nbits, jb, jnp.zeros((tq, 1), I32))
        cut = jnp.where(need > 0.0, pos, jnp.int32(2 ** 30))

        def demote(c, carry):
            k = key_scr[c]
            key_scr[c] = jnp.where(k == thr, jnp.where(c * c_sz + lane_c > cut, INT_MIN, k), k)
            return carry

        lax.fori_loop(0, nch, demote, 0)

    thr_sel = jnp.where(short, INT_MIN + 1, thr)

    def sel_bias(c):
        return jnp.where(key_scr[c] >= thr_sel, 0.0, NEG)

    _flash_loop(nch, qa_scr, dk_ref, dv_ref, sel_bias, DSA_HEADS, tq, s_scr, p_scr, alpha_scr, m_scr, l_scr, acc_scr)
    o_ref[...] = _dsa_finish(acc_scr, l_scr, tq)


def _dsa_prompt(p, bp, seq, tq):
    c_sz = p["dkT"].shape[2]
    nq = seq // tq
    k_top = min(IDX_TOPK_MAX, seq // 4)
    qrow = lambda b, i: (b * nq + i, 0)
    kv = lambda b, i: (b, 0)
    kvt = lambda b, i: (b, 0, 0)
    rows = DSA_HEADS * tq
    return pl.pallas_call(
        functools.partial(_dsa_prompt_kernel, k_top=k_top),
        grid=(bp, nq),
        in_specs=[pl.BlockSpec((tq, 256), qrow), pl.BlockSpec((tq, 128), qrow), pl.BlockSpec((tq, 256), qrow),
                  pl.BlockSpec((seq // c_sz, IDX_DIM, c_sz), kvt), pl.BlockSpec((seq // c_sz, 256, c_sz), kvt),
                  pl.BlockSpec((seq, 256), kv)],
        out_specs=pl.BlockSpec((tq, 256), qrow),
        out_shape=jax.ShapeDtypeStruct((bp * seq, BRANCH_W), F32),
        scratch_shapes=[pltpu.VMEM((seq // c_sz, tq, c_sz), I32), pltpu.VMEM((seq // c_sz, tq, c_sz), BF16),
                        pltpu.VMEM((rows, IDX_DIM), BF16),
                        pltpu.VMEM((rows, 256), BF16), pltpu.VMEM((rows, LANES), F32),
                        pltpu.VMEM((rows, LANES), F32), pltpu.VMEM((2, rows, LANES), F32),
                        pltpu.VMEM((rows, 256), F32), pltpu.VMEM((2, rows, c_sz), BF16),
                        pltpu.VMEM((2, rows, c_sz), F32)],
        compiler_params=_cparams(2),
        name="dsa_prompt",
    )(p["iq"], p["misc"], p["dq"], p["ikT"], p["dkT"], p["dvb"])


def _diff_prompt_kernel(fq_ref, fk_ref, fv_ref, lam_ref, gsub_ref, o_ref,
                        qa_scr, m_scr, l_scr, alpha_scr, acc_scr, p_scr, s_scr, *, lam_init):
    tq = fq_ref.shape[0]
    c_sz = fk_ref.shape[2]
    nh = 2 * DIFF_HEADS
    i = pl.program_id(1)
    n_full = (i * tq + 1) // c_sz
    qpos = i * tq + lax.broadcasted_iota(I32, (tq, 1), 0)
    lane_c = lax.broadcasted_iota(I32, (tq, c_sz), 1)
    fq = fq_ref[...]
    grp = _group_of_lane((tq, BRANCH_W), 5)
    for h in range(nh):
        qa_scr[h * tq:(h + 1) * tq, :] = jnp.where(grp == h, fq, jnp.zeros_like(fq))

    def causal_bias(c):
        return jnp.where(c * c_sz + lane_c <= qpos, 0.0, NEG)

    _flash_loop(n_full + 1, qa_scr, fk_ref, fv_ref, causal_bias, nh, tq,
                s_scr, p_scr, alpha_scr, m_scr, l_scr, acc_scr)
    lam = _diff_lambda(lam_ref, lam_init)
    o_ref[...] = _diff_finish(acc_scr, l_scr, tq, lam, lam_init, gsub_ref[...])


def _diff_prompt(p, lam_p, gsub, lam_init, bp, seq, tq):
    c_sz = p["fkT"].shape[2]
    nq = seq // tq
    qrow = lambda b, i: (b * nq + i, 0)
    kv = lambda b, i: (b, 0)
    const = lambda b, i: (0, 0)
    rows = 2 * DIFF_HEADS * tq
    return pl.pallas_call(
        functools.partial(_diff_prompt_kernel, lam_init=lam_init),
        grid=(bp, nq),
        in_specs=[pl.BlockSpec((tq, 256), qrow),
                  pl.BlockSpec((seq // c_sz, 256, c_sz), lambda b, i: (b, 0, 0)), pl.BlockSpec((seq, 256), kv),
                  pl.BlockSpec((4, DIFF_HEAD_DIM), const), pl.BlockSpec((1, 256), const)],
        out_specs=pl.BlockSpec((tq, 256), qrow),
        out_shape=jax.ShapeDtypeStruct((bp * seq, BRANCH_W), F32),
        scratch_shapes=[pltpu.VMEM((rows, 256), BF16), pltpu.VMEM((rows, LANES), F32),
                        pltpu.VMEM((rows, LANES), F32), pltpu.VMEM((2, rows, LANES), F32),
                        pltpu.VMEM((rows, 256), F32), pltpu.VMEM((2, rows, c_sz), BF16),
                        pltpu.VMEM((2, rows, c_sz), F32)],
        compiler_params=_cparams(2),
        name="diff_prompt",
    )(p["fq"], p["fkT"], p["fvb"], lam_p, gsub)


def _ssd_kernel(par_ref, xbc_ref, dts_ref, z_ref, cprev_ref, hprev_ref, cw_ref, cb_ref, ng_ref,
                y_ref, hout_ref, cout_ref, e_scr, h_scr):
    qin = xbc_ref.shape[0]
    q = SSD_CHUNK
    c = pl.program_id(1)
    last = pl.num_programs(1) - 1

    @pl.when(c == 0)
    def _():
        e_scr[0:8, :] = cprev_ref[...]
        h_scr[...] = hprev_ref[...]
        if qin < q:
            e_scr[8 + qin:8 + q, :] = jnp.zeros((q - qin, SSD_CONV_DIM), F32)

    e_scr[8:8 + qin, :] = xbc_ref[...]
    conv = cb_ref[...] + jnp.zeros((q, SSD_CONV_DIM), F32)
    for k in range(SSD_CONV):
        conv = conv + e_scr[5 + k:5 + k + q, :] * cw_ref[k:k + 1, :]
    new_tail = e_scr[qin:qin + 8, :]
    u = _silu(conv)
    xs = u[:, :BRANCH_W]
    b_all = u[:, BRANCH_W:BRANCH_W + SSD_GROUPS * SSD_STATE]
    bm_t = b_all.T.astype(BF16)
    bm = b_all.astype(BF16)
    cm = u[:, BRANCH_W + SSD_GROUPS * SSD_STATE:].astype(BF16)

    dts = dts_ref[...]
    if qin < q:
        dts = jnp.concatenate([dts, jnp.zeros((q - qin, LANES), F32)], axis=0)
    rowi = lax.broadcasted_iota(I32, (q, LANES), 0)
    lanei = lax.broadcasted_iota(I32, (q, LANES), 1)
    bias_l = jnp.zeros((q, LANES), F32)
    alog_l = jnp.zeros((q, LANES), F32)
    for h in range(SSD_HEADS):
        bias_l = jnp.where(lanei == h, par_ref[h], bias_l)
        alog_l = jnp.where(lanei == h, par_ref[SSD_HEADS + h], alog_l)
    pre = dts + bias_l
    dt = jnp.maximum(pre, 0.0) + jnp.log1p(jnp.exp(-jnp.abs(pre)))
    dt = jnp.where(rowi < qin, jnp.where(lanei < SSD_HEADS, dt, 0.0), 0.0)
    adt = dt * (-jnp.exp(alog_l))
    r_qq = lax.broadcasted_iota(I32, (q, q), 0)
    c_qq = lax.broadcasted_iota(I32, (q, q), 1)
    causal = r_qq >= c_qq
    acs_col = jnp.dot(jnp.where(causal, 1.0, 0.0), adt, preferred_element_type=F32, precision=HIGHEST)
    acs_row = jnp.dot(adt.T[0:8, :], jnp.where(r_qq <= c_qq, 1.0, 0.0), preferred_element_type=F32,
                      precision=HIGHEST)

    grp = _group_of_lane((q, BRANCH_W), 6)
    grp1 = _group_of_lane((1, BRANCH_W), 6)
    dt_b = jnp.zeros((q, BRANCH_W), F32)
    a_b = jnp.zeros((q, BRANCH_W), F32)
    d_b = jnp.zeros((1, BRANCH_W), F32)
    for h in range(SSD_HEADS):
        dt_b = jnp.where(grp == h, dt[:, h:h + 1], dt_b)
        a_b = jnp.where(grp == h, acs_col[:, h:h + 1], a_b)
        d_b = jnp.where(grp1 == h, par_ref[2 * SSD_HEADS + h], d_b)
    xdt = xs * dt_b
    xdt_b = xdt.astype(BF16)
    a_last = a_b[q - 1:q, :]

    y_diag = jnp.zeros((q, BRANCH_W), F32)
    hpg = SSD_HEADS // SSD_GROUPS
    cbs = [_nk_dot(cm[:, 64 * g:64 * (g + 1)], bm[:, 64 * g:64 * (g + 1)]) for g in range(SSD_GROUPS)]
    for h in range(SSD_HEADS):
        lmat = jnp.exp(jnp.where(causal, acs_col[:, h:h + 1] - acs_row[h:h + 1, :], -jnp.inf))
        y_diag = y_diag + jnp.dot((cbs[h // hpg] * lmat).astype(BF16),
                                  jnp.where(grp == h, xdt_b, jnp.zeros_like(xdt_b)), preferred_element_type=F32)
    ht = h_scr[...]
    htb = ht.astype(BF16)
    xdec = (xdt * jnp.exp(a_last - a_b)).astype(BF16)
    st = [jnp.dot(bm_t[64 * g:64 * (g + 1), :], xdec, preferred_element_type=F32) for g in range(SSD_GROUPS)]
    yo = [jnp.dot(cm[:, 64 * g:64 * (g + 1)], htb, preferred_element_type=F32) for g in range(SSD_GROUPS)]
    grp_n = _group_of_lane((SSD_STATE, BRANCH_W), 6)
    h_scr[...] = ht * jnp.exp(a_last) + jnp.where(grp_n < hpg, st[0], st[1])
    y = y_diag + jnp.where(grp < hpg, yo[0], yo[1]) * jnp.exp(a_b) + xs * d_b
    gte = y * _silu(z_ref[...]) if qin == q else y[0:qin, :] * _silu(z_ref[...])
    half = BRANCH_W // SSD_GROUPS
    outs = []
    for g in range(SSD_GROUPS):
        gg = gte[:, half * g:half * (g + 1)]
        outs.append(gg * lax.rsqrt(jnp.mean(gg * gg, axis=-1, keepdims=True) + EPS))
    y_ref[...] = jnp.concatenate(outs, axis=1) * ng_ref[...]
    e_scr[0:8, :] = new_tail

    @pl.when(c == last)
    def _():
        hout_ref[...] = h_scr[...]
        cout_ref[...] = new_tail


def _ssd(p, par, cprev8, hprev, cw, cb, ng, nb, rows_per_b, qin):
    nc = rows_per_b // qin
    row = lambda b, c: (b * nc + c, 0)
    const = lambda b, c: (0, 0)
    y, hout, cout = pl.pallas_call(
        _ssd_kernel,
        grid=(nb, nc),
        in_specs=[pl.BlockSpec(memory_space=pltpu.SMEM),
                  pl.BlockSpec((qin, SSD_CONV_DIM), row), pl.BlockSpec((qin, LANES), row),
                  pl.BlockSpec((qin, 256), lambda b, c: (b * nc + c, 2)),
                  pl.BlockSpec((None, 8, SSD_CONV_DIM), lambda b, c: (b, 0, 0)),
                  pl.BlockSpec((None, SSD_STATE, BRANCH_W), lambda b, c: (b, 0, 0)),
                  pl.BlockSpec((SSD_CONV, SSD_CONV_DIM), const), pl.BlockSpec((1, SSD_CONV_DIM), const),
                  pl.BlockSpec((1, BRANCH_W), const)],
        out_specs=[pl.BlockSpec((qin, BRANCH_W), row),
                   pl.BlockSpec((None, SSD_STATE, BRANCH_W), lambda b, c: (b, 0, 0)),
                   pl.BlockSpec((None, 8, SSD_CONV_DIM), lambda b, c: (b, 0, 0))],
        out_shape=[jax.ShapeDtypeStruct((nb * rows_per_b, BRANCH_W), F32),
                   jax.ShapeDtypeStruct((nb, SSD_STATE, BRANCH_W), F32),
                   jax.ShapeDtypeStruct((nb, 8, SSD_CONV_DIM), F32)],
        scratch_shapes=[pltpu.VMEM((8 + SSD_CHUNK, SSD_CONV_DIM), F32),
                        pltpu.VMEM((SSD_STATE, BRANCH_W), F32)],
        compiler_params=_cparams(2),
        name="ssd",
    )(par, p["sxbc"], p["dts"], p["gates"], cprev8, _state_to_lanes(hprev), cw, cb, ng)
    return y, _state_from_lanes(hout), cout[:, 8 - (SSD_CONV - 1):, :]


def _state_to_lanes(h):
    b = h.shape[0]
    return h.reshape(b, SSD_HEADS * SSD_HEAD_DIM, SSD_STATE).transpose(0, 2, 1)


def _state_from_lanes(ht):
    b = ht.shape[0]
    return ht.transpose(0, 2, 1).reshape(b, SSD_HEADS, SSD_HEAD_DIM, SSD_STATE)


def _pool_kernel(u_ref, halo_ref, w_ref, scale_ref, o_ref, hout_ref, e_scr, *, start_pos):
    t = u_ref.shape[0]
    c = pl.program_id(1)
    last = pl.num_programs(1) - 1

    @pl.when(c == 0)
    def _():
        e_scr[0:8, :] = jnp.zeros((8, BRANCH_W), F32)
        e_scr[8:24, :] = halo_ref[...]

    u = u_ref[...]
    e_scr[24:24 + t, :] = u
    new_halo = e_scr[8 + t:24 + t, :]
    n = 16 + t
    cur = e_scr[8:8 + n, :]
    stages = []
    for k in (1, 2, 4, 8):
        cur = cur + e_scr[8 - k:8 - k + n, :]
        stages.append(cur[16:, :])
        e_scr[8:8 + n, :] = cur
    grp = _group_of_lane((t, BRANCH_W), 6)
    win = jnp.where(grp == 0, stages[0], jnp.where(grp == 1, stages[1], jnp.where(grp == 2, stages[2], stages[3])))
    wlen = jnp.where(grp == 0, 2.0, jnp.where(grp == 1, 4.0, jnp.where(grp == 2, 8.0, 16.0)))
    n_avail = (start_pos + c * t + 1 + lax.broadcasted_iota(I32, (t, BRANCH_W), 0)).astype(F32)
    d = win / jnp.minimum(wlen, n_avail) - u
    o_ref[...] = jnp.dot(d.astype(BF16), w_ref[...], preferred_element_type=F32) * scale_ref[...]
    e_scr[8:24, :] = new_halo

    @pl.when(c == last)
    def _():
        hout_ref[...] = new_halo


def _pool(u, halo16, wbd, scale, nb, rows_per_b, t, start_pos):
    nc = rows_per_b // t
    row = lambda b, c: (b * nc + c, 0)
    const = lambda b, c: (0, 0)
    o, hout = pl.pallas_call(
        functools.partial(_pool_kernel, start_pos=start_pos),
        grid=(nb, nc),
        in_specs=[pl.BlockSpec((t, BRANCH_W), row), pl.BlockSpec((None, 16, BRANCH_W), lambda b, c: (b, 0, 0)),
                  pl.BlockSpec((BRANCH_W, BRANCH_W), const), pl.BlockSpec((1, BRANCH_W), const)],
        out_specs=[pl.BlockSpec((t, BRANCH_W), row), pl.BlockSpec((None, 16, BRANCH_W), lambda b, c: (b, 0, 0))],
        out_shape=[jax.ShapeDtypeStruct((nb * rows_per_b, BRANCH_W), F32),
                   jax.ShapeDtypeStruct((nb, 16, BRANCH_W), F32)],
        scratch_shapes=[pltpu.VMEM((24 + t, BRANCH_W), F32)],
        compiler_params=_cparams(2),
        name="pool",
    )(u, halo16, wbd, scale)
    return o, hout[:, 1:, :]


def _merge_kernel(x_ref, ng_ref, wmg_ref, dsa_ref, diff_ref, ssd_ref, pool_ref, gates_ref, wbr_ref, wout_ref, y_ref):
    x = x_ref[...]
    h = x * lax.rsqrt(jnp.mean(x * x, axis=-1, keepdims=True) + EPS) * ng_ref[...]
    hb = h.astype(BF16)
    d = x.shape[1]
    br = (dsa_ref[...] * _silu(gates_ref[:, 0:256]),
          diff_ref[...] * _silu(gates_ref[:, 256:512]),
          ssd_ref[...],
          pool_ref[...] * _silu(gates_ref[:, 768:1024]))
    m = jnp.zeros(x.shape, F32)
    for n in range(N_BRANCH):
        mg = jnp.dot(hb, wmg_ref[:, n * d:(n + 1) * d], preferred_element_type=F32)
        up = jnp.dot(br[n].astype(BF16), wbr_ref[n], preferred_element_type=F32)
        m = m + _sigmoid(mg) * up
    y_ref[...] = x + jnp.dot(m.astype(BF16), wout_ref[...], preferred_element_type=F32)


def _merge(x, ng, wmg, dsa_o, diff_o, ssd_o, pool_o, gates, wbr, wout, tm):
    n, d = x.shape
    row = lambda i: (i, 0)
    const = lambda i: (0, 0)
    return pl.pallas_call(
        _merge_kernel,
        grid=(n // tm,),
        in_specs=[pl.BlockSpec((tm, d), row), pl.BlockSpec((1, d), const), pl.BlockSpec(wmg.shape, const),
                  pl.BlockSpec((tm, 256), row), pl.BlockSpec((tm, 256), row), pl.BlockSpec((tm, 256), row),
                  pl.BlockSpec((tm, 256), row), pl.BlockSpec((tm, 1024), row),
                  pl.BlockSpec(wbr.shape, lambda i: (0, 0, 0)), pl.BlockSpec(wout.shape, const)],
        out_specs=pl.BlockSpec((tm, d), row),
        out_shape=jax.ShapeDtypeStruct((n, d), F32),
        compiler_params=_cparams(1),
        name="merge",
    )(x, ng, wmg, dsa_o, diff_o, ssd_o, pool_o, gates, wbr, wout)


_PAGES_PER_STEP = 32
_KEY_CHUNK = 512
_Q_BLOCK = 128
_MERGE_ROWS = 256
_COUNT_GROUP = 4


def _page_specs(layer, features, npp):
    def spec(k):
        return pl.BlockSpec((None, None, features, PAGE_SIZE),
                            lambda b, g, pt, k=k: (layer, pt[b, g * npp + k], 0, 0))
    return [spec(k) for k in range(npp)]


def _pages_view(cache):
    depth, n_pool, page = cache.shape[:3]
    c = cache.reshape(depth, n_pool, page, -1)
    return jnp.swapaxes(c, 2, 3)


def _dsa_sidx_kernel(pt_ref, iq_ref, misc_ref, *rest, npp, n_pages, k_top):
    pages = rest[:npp]
    bias_ref = rest[npp]
    key_scr, qi_scr, jst_scr = rest[npp + 1:]
    t = iq_ref.shape[0]
    g = pl.program_id(1)
    last = pl.num_programs(1) - 1
    wts = misc_ref[:, 64:64 + IDX_HEADS]

    def scores(kt):
        s = jnp.dot(qi_scr[...], kt, preferred_element_type=F32)
        acc = None
        for h in range(IDX_HEADS):
            v = wts[:, h:h + 1] * jnp.maximum(s[h * t:(h + 1) * t, :], 0.0)
            acc = v if acc is None else acc + v
        return jnp.where(acc == 0.0, 0.0, acc)

    @pl.when(g == 0)
    def _():
        iq = iq_ref[...]
        for h in range(IDX_HEADS):
            qi_scr[h * t:(h + 1) * t, :] = iq[:, IDX_DIM * h:IDX_DIM * (h + 1)]
        knew = jnp.concatenate([misc_ref[...], jnp.zeros((PAGE_SIZE - t, LANES), F32)], axis=0)
        sc = scores(knew.T[0:IDX_DIM, :].astype(BF16))
        causal = lax.broadcasted_iota(I32, (t, LANES), 1) <= lax.broadcasted_iota(I32, (t, LANES), 0)
        key_scr[n_pages] = jnp.where(causal, _sortable(sc), INT_MIN)

    sc = scores(jnp.concatenate([pg[...].astype(BF16) for pg in pages], axis=1))
    for k in range(npp):
        key_scr[g * npp + k] = _sortable(sc[:, k * PAGE_SIZE:(k + 1) * PAGE_SIZE])

    @pl.when(g == last)
    def _():
        keys = key_scr[...]
        kpos = (lax.broadcasted_iota(I32, keys.shape, 0) * PAGE_SIZE + lax.broadcasted_iota(I32, keys.shape, 2))

        def count(m):
            return jnp.sum(jnp.sum(m, axis=0), axis=1, keepdims=True)

        kf = float(k_top)
        c0 = count(jnp.where(keys >= 0, 1.0, 0.0))
        cand0 = jnp.where(c0 >= kf, 0, INT_MIN).astype(I32)

        def bit_body(b, cand):
            trial = cand | jnp.left_shift(jnp.int32(1), 30 - b)
            cnt = count(jnp.where(keys >= trial[None], 1.0, 0.0))
            return jnp.where(cnt >= kf, trial, cand)

        thr = lax.fori_loop(0, 31, bit_body, cand0)
        cnt_gt = count(jnp.where(keys > thr[None], 1.0, 0.0))
        cnt_ge = count(jnp.where(keys >= thr[None], 1.0, 0.0))
        need = jnp.where(cnt_ge > kf, jnp.where(thr > INT_MIN, 1.0, 0.0), 0.0)
        rem = kf - cnt_gt
        big = jnp.int32(2 ** 30)
        jst_scr[...] = jnp.full((t, 1), big, I32)

        @pl.when(jnp.max(need) > 0.0)
        def _():
            nbits = ((n_pages + 1) * PAGE_SIZE - 1).bit_length()

            def jb(b, pos):
                trial = pos | jnp.left_shift(jnp.int32(1), nbits - 1 - b)
                cnt = count(jnp.where(keys == thr[None], jnp.where(kpos < trial[None], 1.0, 0.0), 0.0))
                return jnp.where(cnt < rem, trial, pos)

            pos = lax.fori_loop(0, nbits, jb, jnp.zeros((t, 1), I32))
            jst_scr[...] = jnp.where(need > 0.0, pos, big)

        jst = jst_scr[...]
        sel = jnp.where(keys > thr[None], 0.0,
                        jnp.where(keys == thr[None], jnp.where(kpos <= jst[None], 0.0, NEG), NEG))
        newc = lax.broadcasted_iota(I32, keys.shape, 0) == n_pages
        causal = lax.broadcasted_iota(I32, keys.shape, 2) <= lax.broadcasted_iota(I32, keys.shape, 1)
        bias_ref[...] = jnp.where(newc, jnp.where(causal, sel, NEG), sel)


def _dsa_sidx(ps, cache_ik, layer, page_table, bs, t):
    n_pages = page_table.shape[1]
    npp = math.gcd(n_pages, _PAGES_PER_STEP)
    k_top = min(IDX_TOPK_MAX, (n_pages * PAGE_SIZE + t) // 4)
    row = lambda b, g, pt: (b, 0)
    grid_spec = pltpu.PrefetchScalarGridSpec(
        num_scalar_prefetch=1, grid=(bs, n_pages // npp),
        in_specs=[pl.BlockSpec((t, 256), row), pl.BlockSpec((t, 128), row)] + _page_specs(layer, IDX_DIM, npp),
        out_specs=pl.BlockSpec((None, n_pages + 1, t, LANES), lambda b, g, pt: (b, 0, 0, 0)),
        scratch_shapes=[pltpu.VMEM((n_pages + 1, t, LANES), I32), pltpu.VMEM((IDX_HEADS * t, IDX_DIM), BF16),
                        pltpu.VMEM((t, 1), I32)])
    return pl.pallas_call(
        functools.partial(_dsa_sidx_kernel, npp=npp, n_pages=n_pages, k_top=k_top),
        grid_spec=grid_spec,
        out_shape=jax.ShapeDtypeStruct((bs, n_pages + 1, t, LANES), F32),
        compiler_params=_cparams(2),
        name="dsa_sample_index",
    )(page_table, ps["iq"], ps["misc"], *([cache_ik] * npp))


def _paged_attn_kernel(pt_ref, q_ref, kn_ref, vn_ref, *rest, npp, nheads, log2_hd, use_bias, finish):
    idx = 0
    if use_bias:
        bias_new_ref, bias_pg_ref = rest[0], rest[1]
        idx = 2
    kpages = rest[idx:idx + npp]
    vpages = rest[idx + npp:idx + 2 * npp]
    n_scr = 7
    extra = rest[idx + 2 * npp:-(n_scr + 1)]
    o_ref = rest[-(n_scr + 1)]
    qa_scr, m_scr, l_scr, alpha_scr, acc_scr, p_scr, s_scr = rest[-n_scr:]
    t = q_ref.shape[0]
    rows = nheads * t
    g = pl.program_id(1)
    last = pl.num_programs(1) - 1

    def tile_rows(b):
        return jnp.concatenate([b] * nheads, axis=0)

    def accumulate(pv):
        alpha = alpha_scr[...]
        acc_scr[...] = jnp.concatenate([alpha, alpha], axis=1) * acc_scr[...] + pv

    @pl.when(g == 0)
    def _():
        qv = q_ref[...]
        grp = _group_of_lane((t, BRANCH_W), log2_hd)
        for h in range(nheads):
            qa_scr[h * t:(h + 1) * t, :] = jnp.where(grp == h, qv, jnp.zeros_like(qv))
        m_scr[...] = jnp.full((rows, LANES), NEG, F32)
        l_scr[...] = jnp.zeros((rows, LANES), F32)
        acc_scr[...] = jnp.zeros((rows, BRANCH_W), F32)
        kn = jnp.concatenate([kn_ref[...], jnp.zeros((PAGE_SIZE - t, BRANCH_W), F32)], axis=0)
        vn = jnp.concatenate([vn_ref[...], jnp.zeros((PAGE_SIZE - t, BRANCH_W), BF16)], axis=0)
        if use_bias:
            bias = bias_new_ref[0]
        else:
            causal = lax.broadcasted_iota(I32, (t, LANES), 1) <= lax.broadcasted_iota(I32, (t, LANES), 0)
            bias = jnp.where(causal, 0.0, NEG)
        s_new = s_scr.at[:, 0:PAGE_SIZE]
        p_new = p_scr.at[:, 0:PAGE_SIZE]
        s_new[...] = jnp.dot(qa_scr[...], kn.T.astype(BF16), preferred_element_type=F32)
        _flash_rows(s_new, tile_rows(bias), m_scr, l_scr, alpha_scr, p_new, 0, rows)
        accumulate(jnp.dot(p_new[...], vn, preferred_element_type=F32))

    kt = jnp.concatenate([kp[...].astype(BF16) for kp in kpages], axis=1)
    vt = jnp.concatenate([vp[...].astype(BF16) for vp in vpages], axis=1)
    s_scr[...] = jnp.dot(qa_scr[...], kt, preferred_element_type=F32)
    bias = None
    if use_bias:
        bias = tile_rows(jnp.concatenate([bias_pg_ref[k] for k in range(npp)], axis=1))
    _flash_rows(s_scr, bias, m_scr, l_scr, alpha_scr, p_scr, 0, rows)
    accumulate(_nk_dot(p_scr[...], vt))

    @pl.when(g == last)
    def _():
        o_ref[...] = finish(acc_scr, l_scr, t, *extra)


def _dsa_finish(acc_ref, l_ref, t):
    grp = _group_of_lane((t, BRANCH_W), 6)
    out = jnp.zeros((t, BRANCH_W), F32)
    for h in range(DSA_HEADS):
        l_row = jnp.sum(l_ref[h * t:(h + 1) * t, :], axis=1, keepdims=True)
        out = out + jnp.where(grp == h, acc_ref[h * t:(h + 1) * t, :] / l_row, 0.0)
    return out


def _diff_sample_finish(acc_ref, l_ref, t, lam_ref, gsub_ref, *, lam_init):
    return _diff_finish(acc_ref, l_ref, t, _diff_lambda(lam_ref, lam_init), lam_init, gsub_ref[...])


def _paged_attn(q, kn, vn, cache_k, cache_v, layer, page_table, bs, t, *, nheads, log2_hd, bias=None,
                extra=(), finish, name):
    n_pages = page_table.shape[1]
    npp = math.gcd(n_pages, _PAGES_PER_STEP)
    row = lambda b, g, pt: (b, 0)
    const = lambda b, g, pt: (0, 0)
    in_specs = [pl.BlockSpec((t, 256), row), pl.BlockSpec((t, 256), row), pl.BlockSpec((t, 256), row)]
    args = [q, kn, vn]
    if bias is not None:
        in_specs += [pl.BlockSpec((None, 1, t, LANES), lambda b, g, pt: (b, n_pages, 0, 0)),
                     pl.BlockSpec((None, npp, t, LANES), lambda b, g, pt: (b, g, 0, 0))]
        args += [bias, bias]
    in_specs += _page_specs(layer, 256, npp) + _page_specs(layer, 256, npp)
    args += [cache_k] * npp + [cache_v] * npp
    for e in extra:
        in_specs.append(pl.BlockSpec(e.shape, const))
        args.append(e)
    rows = nheads * t
    grid_spec = pltpu.PrefetchScalarGridSpec(
        num_scalar_prefetch=1, grid=(bs, n_pages // npp), in_specs=in_specs,
        out_specs=pl.BlockSpec((t, 256), row),
        scratch_shapes=[pltpu.VMEM((rows, 256), BF16), pltpu.VMEM((rows, LANES), F32),
                        pltpu.VMEM((rows, LANES), F32), pltpu.VMEM((rows, LANES), F32),
                        pltpu.VMEM((rows, 256), F32), pltpu.VMEM((rows, npp * PAGE_SIZE), BF16),
                        pltpu.VMEM((rows, npp * PAGE_SIZE), F32)])
    return pl.pallas_call(
        functools.partial(_paged_attn_kernel, npp=npp, nheads=nheads, log2_hd=log2_hd,
                          use_bias=bias is not None, finish=finish),
        grid_spec=grid_spec,
        out_shape=jax.ShapeDtypeStruct((bs * t, BRANCH_W), F32),
        compiler_params=_cparams(2),
        name=name,
    )(page_table, *args)


def _rope_table(pos, half, reps, extra_cos=None):
    inv = ROPE_THETA ** (-jnp.arange(half, dtype=F32) / half)
    ang = pos.astype(F32)[:, None] * inv[None, :]
    cos, sin = jnp.cos(ang), jnp.sin(ang)
    cos_h = jnp.concatenate([cos, cos], axis=1)
    sin_h = jnp.concatenate([-sin, sin], axis=1)
    cos_t, sin_t = jnp.tile(cos_h, (1, reps)), jnp.tile(sin_h, (1, reps))
    if extra_cos is not None:
        n = pos.shape[0]
        cos_t = jnp.concatenate([cos_t, jnp.broadcast_to(extra_cos[None, :], (n, extra_cos.shape[0]))], axis=1)
        sin_t = jnp.concatenate([sin_t, jnp.zeros((n, extra_cos.shape[0]), F32)], axis=1)
    return jnp.concatenate([cos_t, sin_t], axis=1)


def _tables(pos):
    misc_scale = jnp.concatenate([jnp.full((IDX_HEADS,), IDX_HEADS ** -0.5, F32),
                                  jnp.ones((LANES - IDX_DIM - IDX_HEADS,), F32)])
    return (_rope_table(pos, DSA_HEAD_DIM // 2, 4), _rope_table(pos, DIFF_HEAD_DIM // 2, 8),
            _rope_table(pos, IDX_DIM // 2, 1, misc_scale))


def _pack_w_in(w):
    sizes = (256, 256, 256, 256, 256, 64, 4, 256, 256, 256, 256, 256, 512, 4, 256, 256, 4096)
    offs = [0]
    for s in sizes:
        offs.append(offs[-1] + s)
    (dq, dk, dv, dg, iq, ik, iw, fq, fk, fv, fg, sz, sxbc, sdt, pu, pg, mg) = [
        w[:, offs[i]:offs[i + 1]] for i in range(len(sizes))]
    d = w.shape[0]
    misc = jnp.concatenate([ik, iw, jnp.zeros((d, LANES - IDX_DIM - IDX_HEADS), w.dtype)], axis=1)
    dts = jnp.concatenate([sdt, jnp.zeros((d, LANES - SSD_HEADS), w.dtype)], axis=1)
    packed = jnp.concatenate([dq, dk, dv, iq, fq, fk, fv, dg, fg, sz, pg, sxbc, pu, misc, dts], axis=1)
    return packed.astype(BF16), mg.astype(BF16)


def _block_diag(w):
    g, n, _ = w.shape
    out = jnp.zeros((g * n, g * n), w.dtype)
    for i in range(g):
        out = out.at[i * n:(i + 1) * n, i * n:(i + 1) * n].set(w[i])
    return out


def kernel(x_prompt, x_sample, cache_dsa_k, cache_dsa_v, cache_idx_k, cache_diff_k, cache_diff_v, state_ssm,
           state_conv, state_pool, page_table, norm_g, w_in, dsa_qk_g, diff_qk_g, diff_lam, diff_subln,
           ssd_conv_w, ssd_conv_b, ssd_dt_bias, ssd_a_log, ssd_d, ssd_norm, pool_w, pool_scale, w_branch, w_out):
    bp, seq, d = x_prompt.shape
    bs, t, _ = x_sample.shape
    depth = norm_g.shape[0]
    n_pages = page_table.shape[1]
    past = n_pages * PAGE_SIZE
    n_pool = cache_dsa_k.shape[1]

    tab_p = _tables(jnp.arange(seq, dtype=I32))
    tab_s = tuple(jnp.tile(a, (bs, 1)) for a in _tables(past + jnp.arange(t, dtype=I32)))
    ck, cv, cik = _pages_view(cache_dsa_k), _pages_view(cache_dsa_v), _pages_view(cache_idx_k)
    cfk, cfv = _pages_view(cache_diff_k), _pages_view(cache_diff_v)

    tm = min(_KEY_CHUNK, seq)
    tq = min(_Q_BLOCK, seq)
    tm_s = min(256, bs * t)
    pool_t = min(512, seq)
    xp = x_prompt.reshape(bp * seq, d)
    xs = x_sample.reshape(bs * t, d)
    acc_p = {n: [] for n in ("dk", "dv", "ik", "fk", "fv", "ssm", "conv", "pool")}
    acc_s = {n: [] for n in acc_p}

    for l in range(depth):
        lam_init = 0.8 - 0.6 * math.exp(-0.3 * l)
        wp, wmg = _pack_w_in(w_in[l])
        ng = norm_g[l].reshape(1, d)
        gains = jnp.stack([jnp.tile(dsa_qk_g[l, 0], 4), jnp.tile(dsa_qk_g[l, 1], 4),
                           jnp.tile(diff_qk_g[l, 0], 8), jnp.tile(diff_qk_g[l, 1], 8)])
        gsub = jnp.tile(diff_subln[l], 4).reshape(1, BRANCH_W)
        lam_p = diff_lam[l]
        ssd_par = jnp.concatenate([ssd_dt_bias[l], ssd_a_log[l], ssd_d[l]]).astype(F32)
        cw, cb = ssd_conv_w[l], ssd_conv_b[l].reshape(1, SSD_CONV_DIM)
        sng = ssd_norm[l].reshape(1, BRANCH_W)
        wbd = _block_diag(pool_w[l]).astype(BF16)
        pscale = pool_scale[l].reshape(1, BRANCH_W)
        wbr = w_branch[l].astype(BF16)
        wout = w_out[l].astype(BF16)

        pp = _proj(xp, ng, wp, *tab_p, gains, tm, seq // tm)
        dsa_o = _dsa_prompt(pp, bp, seq, tq)
        diff_o = _diff_prompt(pp, lam_p, gsub, lam_init, bp, seq, tq)
        ssd_o, ssm_new, conv_new = _ssd(pp, ssd_par, jnp.zeros((bp, 8, SSD_CONV_DIM), F32),
                                        jnp.zeros((bp, SSD_HEADS, SSD_HEAD_DIM, SSD_STATE), F32), cw, cb, sng,
                                        bp, seq, SSD_CHUNK)
        pool_o, pool_new = _pool(pp["pu"], jnp.zeros((bp, 16, BRANCH_W), F32), wbd, pscale, bp, seq, pool_t, 0)
        xp = _merge(xp, ng, wmg, dsa_o, diff_o, ssd_o, pool_o, pp["gates"], wbr, wout, min(_MERGE_ROWS, seq))
        for n, v in zip(acc_p, (pp["dk"].reshape(bp, seq, DSA_HEADS, DSA_HEAD_DIM),
                                pp["dv"].reshape(bp, seq, DSA_HEADS, DSA_HEAD_DIM),
                                pp["misc"][:, :IDX_DIM].reshape(bp, seq, IDX_DIM),
                                pp["fk"].reshape(bp, seq, 2 * DIFF_HEADS, DIFF_HEAD_DIM),
                                pp["fv"].reshape(bp, seq, DIFF_HEADS, DIFF_V_DIM),
                                ssm_new, conv_new, pool_new)):
            acc_p[n].append(v)

        ps = _proj(xs, ng, wp, *tab_s, gains, tm_s, (bs * t) // tm_s)
        bias = _dsa_sidx(ps, cik, l, page_table, bs, t)
        dsa_o = _paged_attn(ps["dq"], ps["dk"], ps["dvb"], ck, cv, l, page_table, bs, t,
                            nheads=DSA_HEADS, log2_hd=6, bias=bias, finish=_dsa_finish, name="dsa_sample_attn")
        diff_o = _paged_attn(ps["fq"], ps["fk"], ps["fvb"], cfk, cfv, l, page_table, bs, t,
                             nheads=2 * DIFF_HEADS, log2_hd=5, extra=(lam_p, gsub),
                             finish=functools.partial(_diff_sample_finish, lam_init=lam_init),
                             name="diff_sample_attn")
        cprev8 = jnp.concatenate([jnp.zeros((bs, 8 - (SSD_CONV - 1), SSD_CONV_DIM), F32), state_conv[l]], axis=1)
        ssd_o, ssm_new, conv_new = _ssd(ps, ssd_par, cprev8, state_ssm[l], cw, cb, sng, bs, t, t)
        halo = jnp.concatenate([jnp.zeros((bs, 1, BRANCH_W), F32), state_pool[l]], axis=1)
        pool_o, pool_new = _pool(ps["pu"], halo, wbd, pscale, bs, t, t, past)
        xs = _merge(xs, ng, wmg, dsa_o, diff_o, ssd_o, pool_o, ps["gates"], wbr, wout, tm_s)
        for n, v in zip(acc_s, (ps["dk"].reshape(bs, t, DSA_HEADS, DSA_HEAD_DIM),
                                ps["dv"].reshape(bs, t, DSA_HEADS, DSA_HEAD_DIM),
                                ps["misc"][:, :IDX_DIM].reshape(bs, t, IDX_DIM),
                                ps["fk"].reshape(bs, t, 2 * DIFF_HEADS, DIFF_HEAD_DIM),
                                ps["fv"].reshape(bs, t, DIFF_HEADS, DIFF_V_DIM),
                                ssm_new, conv_new, pool_new)):
            acc_s[n].append(v)

    names = ("dk", "dv", "ik", "fk", "fv", "ssm", "conv", "pool")
    return ((xp.reshape(bp, seq, d), xs.reshape(bs, t, d))
            + tuple(jnp.stack(acc_p[n]) for n in names) + tuple(jnp.stack(acc_s[n]) for n in names))
```

```python
import functools
import math

import jax
import jax.numpy as jnp
from jax import lax
from jax.experimental import pallas as pl
from jax.experimental.pallas import tpu as pltpu

F32 = jnp.float32
BF16 = jnp.bfloat16
I32 = jnp.int32

BRANCH_W = 256
DSA_HEADS = 4
DSA_HEAD_DIM = 64
IDX_HEADS = 4
IDX_DIM = 64
IDX_TOPK_MAX = 256
DIFF_HEADS = 4
DIFF_HEAD_DIM = 32
DIFF_V_DIM = 64
SSD_HEADS = 4
SSD_HEAD_DIM = 64
SSD_STATE = 64
SSD_GROUPS = 2
SSD_CONV = 4
SSD_CHUNK = 128
SSD_CONV_DIM = 512
POOL_WINDOWS = (2, 4, 8, 16)
POOL_STATE = 15
PAGE_SIZE = 128
ROPE_THETA = 10000.0
EPS = 1e-6
N_BRANCH = 4

LANES = 128
SUBLANES = 8
VMEM_LIMIT = 56 * 1024 * 1024

LOG2E = math.log2(math.e)
NEG = -1e30
INT_MIN = -(2 ** 31)
HIGHEST = lax.Precision.HIGHEST

_SLABS = (("dq", 256), ("dk", 256), ("dv", 256), ("iq", 256), ("fq", 256), ("fk", 256), ("fv", 256),
          ("gates", 1024), ("sxbc", 512), ("pu", 256), ("misc", 128), ("dts", 128))
_OFF = {}
_o = 0
for _n, _w in _SLABS:
    _OFF[_n] = (_o, _o + _w)
    _o += _w
W_PACKED = _o


def _cparams(n_axes, vmem=VMEM_LIMIT):
    return pltpu.CompilerParams(dimension_semantics=("arbitrary",) * n_axes, vmem_limit_bytes=vmem)


def _nk_dot(a, b):
    return lax.dot_general(a, b, (((1,), (1,)), ((), ())), preferred_element_type=F32)


def _silu(x):
    return x * (1.0 / (1.0 + jnp.exp(-x)))


def _sigmoid(x):
    return 1.0 / (1.0 + jnp.exp(-x))


def _group_of_lane(shape, log2_width):
    return lax.broadcasted_iota(I32, shape, len(shape) - 1) >> log2_width


def _proj_kernel(x_ref, ng_ref, w_ref, t64_ref, t32_ref, tmisc_ref, gains_ref,
                 dq_ref, dk_ref, dv_ref, dvb_ref, iq_ref, misc_ref, miscb_ref, fq_ref, fk_ref, fv_ref, fvb_ref,
                 gates_ref, sxbc_ref, pu_ref, dts_ref, dkt_ref, fkt_ref):
    x = x_ref[...]
    h = x * lax.rsqrt(jnp.mean(x * x, axis=-1, keepdims=True) + EPS) * ng_ref[...]
    hb = h.astype(BF16)

    def mm(name):
        a, b = _OFF[name]
        return jnp.dot(hb, w_ref[:, a:b], preferred_element_type=F32)

    def head_norm(z, g, log2_hd):
        n = z.shape[1]
        r = lax.broadcasted_iota(I32, (n, n), 0) >> log2_hd
        c = lax.broadcasted_iota(I32, (n, n), 1) >> log2_hd
        bd = jnp.where(r == c, 1.0 / (1 << log2_hd), 0.0).astype(F32)
        ms = jnp.dot(z * z, bd, preferred_element_type=F32, precision=HIGHEST)
        return z * lax.rsqrt(ms + EPS) * g

    def rope(z, tab_ref, half):
        n = z.shape[1]
        cos = tab_ref[:, :n]
        sin = tab_ref[:, n:]
        lane = lax.broadcasted_iota(I32, z.shape, 1)
        first = (lane & (2 * half - 1)) < half
        partner = jnp.where(first, pltpu.roll(z, n - half, 1), pltpu.roll(z, half, 1))
        return z * cos + partner * sin

    dq = rope(head_norm(mm("dq"), gains_ref[0:1, :], 6), t64_ref, 32)
    dq_ref[...] = (dq * (DSA_HEAD_DIM ** -0.5 * LOG2E)).astype(BF16)
    dk = rope(head_norm(mm("dk"), gains_ref[1:2, :], 6), t64_ref, 32)
    dk_ref[...] = dk
    dkt_ref[...] = dk.T.astype(BF16)
    dv = mm("dv")
    dv_ref[...] = dv
    dvb_ref[...] = dv.astype(BF16)
    iq_ref[...] = (rope(mm("iq"), t64_ref, 32) * (IDX_DIM ** -0.5)).astype(BF16)
    misc = rope(mm("misc"), tmisc_ref, 32)
    misc_ref[...] = misc
    miscb_ref[...] = misc.astype(BF16)
    fq = rope(head_norm(mm("fq"), gains_ref[2:3, :], 5), t32_ref, 16)
    fq_ref[...] = (fq * (DIFF_HEAD_DIM ** -0.5 * LOG2E)).astype(BF16)
    fk = rope(head_norm(mm("fk"), gains_ref[3:4, :], 5), t32_ref, 16)
    fk_ref[...] = fk
    fkt_ref[...] = fk.T.astype(BF16)
    fv = mm("fv")
    fv_ref[...] = fv
    fvb_ref[...] = fv.astype(BF16)
    for j in range(4):
        a = _OFF["gates"][0] + 256 * j
        gates_ref[:, 256 * j:256 * (j + 1)] = jnp.dot(hb, w_ref[:, a:a + 256], preferred_element_type=F32)
    for j in range(2):
        a = _OFF["sxbc"][0] + 256 * j
        sxbc_ref[:, 256 * j:256 * (j + 1)] = jnp.dot(hb, w_ref[:, a:a + 256], preferred_element_type=F32)
    pu_ref[...] = mm("pu")
    dts_ref[...] = mm("dts")


_PROJ_OUTS = (("dq", 256, BF16), ("dk", 256, F32), ("dv", 256, F32), ("dvb", 256, BF16), ("iq", 256, BF16),
              ("misc", 128, F32), ("miscb", 128, BF16), ("fq", 256, BF16), ("fk", 256, F32), ("fv", 256, F32), ("fvb", 256, BF16),
              ("gates", 1024, F32), ("sxbc", 512, F32), ("pu", 256, F32), ("dts", 128, F32))
_PROJ_T_OUTS = (("dkT", 256), ("fkT", 256))


def _proj(x, ng, wp, t64, t32, tmisc, gains, tm, tab_blocks):
    n, d = x.shape
    row = lambda i: (i, 0)
    tab = lambda i: (i % tab_blocks, 0)
    const = lambda i: (0, 0)
    outs = _PROJ_OUTS
    res = pl.pallas_call(
        _proj_kernel,
        grid=(n // tm,),
        in_specs=[pl.BlockSpec((tm, d), row), pl.BlockSpec((1, d), const), pl.BlockSpec(wp.shape, const),
                  pl.BlockSpec((tm, 512), tab), pl.BlockSpec((tm, 512), tab), pl.BlockSpec((tm, 256), tab),
                  pl.BlockSpec((4, 256), const)],
        out_specs=([pl.BlockSpec((tm, w), row) for _, w, _ in outs]
                   + [pl.BlockSpec((None, w, tm), lambda i: (i, 0, 0)) for _, w in _PROJ_T_OUTS]),
        out_shape=([jax.ShapeDtypeStruct((n, w), dt) for _, w, dt in outs]
                   + [jax.ShapeDtypeStruct((n // tm, w, tm), BF16) for _, w in _PROJ_T_OUTS]),
        compiler_params=_cparams(1),
        name="proj",
    )(x, ng, wp, t64, t32, tmisc, gains)
    return dict(zip([o[0] for o in outs] + [o[0] for o in _PROJ_T_OUTS], res))


def _sortable(x):
    bits = pltpu.bitcast(x, I32)
    return bits ^ ((bits >> 31) & 0x7FFFFFFF)


def _flash_rows(s_ref, bias, m_ref, l_ref, alpha_ref, p_ref, r0, r1):
    c_sz = s_ref.shape[1]
    nq = c_sz // LANES
    parts = []
    for q in range(nq):
        v = s_ref[r0:r1, q * LANES:(q + 1) * LANES]
        parts.append(v if bias is None else v + bias[:, q * LANES:(q + 1) * LANES])
    mx = parts[0]
    for q in range(1, nq):
        mx = jnp.maximum(mx, parts[q])
    m_old = m_ref[r0:r1, :]
    m_new = jnp.maximum(m_old, jnp.max(mx, axis=1, keepdims=True))
    alpha = jnp.exp2(m_old - m_new)
    m_ref[r0:r1, :] = m_new
    alpha_ref[r0:r1, :] = alpha
    lsum = None
    for q in range(nq):
        p = jnp.exp2(parts[q] - m_new)
        p_ref[r0:r1, q * LANES:(q + 1) * LANES] = p.astype(BF16)
        lsum = p if lsum is None else lsum + p
    l_ref[r0:r1, :] = alpha * l_ref[r0:r1, :] + lsum


def _flash_loop(n, qa_ref, kt_ref, v_ref, bias_fn, nh, tq, s_scr, p_scr, alpha_scr, m_scr, l_scr, acc_scr):
    c_sz = kt_ref.shape[2]
    rows = nh * tq
    m_scr[...] = jnp.full((rows, LANES), NEG, F32)
    l_scr[...] = jnp.zeros((rows, LANES), F32)
    acc_scr[...] = jnp.zeros((rows, BRANCH_W), F32)

    def scores(c, slot):
        s_scr[slot] = jnp.dot(qa_ref[...], kt_ref[c], preferred_element_type=F32)

    def softmax(c, slot):
        bias = bias_fn(c)
        for h in range(nh):
            _flash_rows(s_scr.at[slot], bias, m_scr, l_scr, alpha_scr.at[slot], p_scr.at[slot],
                        h * tq, (h + 1) * tq)

    def values(c, slot):
        start = pl.multiple_of(c * c_sz, c_sz)
        pv = jnp.dot(p_scr[slot], v_ref[pl.ds(start, c_sz), :], preferred_element_type=F32)
        alpha = alpha_scr[slot]
        acc_scr[...] = jnp.concatenate([alpha, alpha], axis=1) * acc_scr[...] + pv

    scores(0, 0)
    scores(jnp.minimum(1, n - 1), 1)
    softmax(0, 0)

    def body(j, carry):
        c = 2 * j + 1
        scores(jnp.minimum(c + 1, n - 1), 0)
        values(c - 1, 0)
        softmax(c, 1)
        scores(jnp.minimum(c + 2, n - 1), 1)
        values(c, 1)
        softmax(c + 1, 0)
        return carry

    lax.fori_loop(0, (n - 1) // 2, body, 0)

    @pl.when((n - 1) % 2 == 1)
    def _():
        values(n - 2, 0)
        softmax(n - 1, 1)

    values(n - 1, (n - 1) & 1)


def _diff_lambda(lam_ref, lam_init):
    lp = lam_ref[...]
    s1 = jnp.sum(lp[0:1, :] * lp[1:2, :], axis=1, keepdims=True)
    s2 = jnp.sum(lp[2:3, :] * lp[3:4, :], axis=1, keepdims=True)
    return jnp.exp(s1) - jnp.exp(s2) + lam_init


def _diff_finish(acc_ref, l_ref, rows, lam, lam_init, gsub):
    grp = _group_of_lane((rows, BRANCH_W), 6)
    out = jnp.zeros((rows, BRANCH_W), F32)
    for j in range(DIFF_HEADS):
        r1, r2 = 2 * j * rows, (2 * j + 1) * rows
        a1 = acc_ref[r1:r1 + rows, :] / jnp.sum(l_ref[r1:r1 + rows, :], axis=1, keepdims=True)
        a2 = acc_ref[r2:r2 + rows, :] / jnp.sum(l_ref[r2:r2 + rows, :], axis=1, keepdims=True)
        o = a1 - lam * a2
        ms = jnp.sum(jnp.where(grp == j, o * o, 0.0), axis=1, keepdims=True) * (1.0 / DIFF_V_DIM)
        out = out + jnp.where(grp == j, o * lax.rsqrt(ms + EPS), 0.0)
    return out * gsub * (1.0 - lam_init)


def _dsa_prompt_kernel(iq_ref, misc_ref, dq_ref, ikn_ref, dk_ref, dv_ref, o_ref,
                       key_scr, st_scr, qit_scr, qa_scr, m_scr, l_scr, alpha_scr, acc_scr, p_scr, s_scr, *, k_top):
    tq = iq_ref.shape[0]
    c_sz = dk_ref.shape[2]
    seq = dk_ref.shape[0] * c_sz
    i = pl.program_id(1)
    nch = ((i + 1) * tq + c_sz - 1) // c_sz
    qpos = i * tq + lax.broadcasted_iota(I32, (1, tq), 1)
    sub_c = lax.broadcasted_iota(I32, (c_sz, tq), 0)

    dq = dq_ref[...]
    grp = _group_of_lane((tq, BRANCH_W), 6)
    for h in range(DSA_HEADS):
        qa_scr[h * tq:(h + 1) * tq, :] = jnp.where(grp == h, dq, jnp.zeros_like(dq))
    iq_t = iq_ref[...].astype(F32).T
    qit_scr[IDX_DIM:, :] = jnp.zeros((LANES - IDX_DIM, IDX_HEADS * tq), BF16)
    for h in range(IDX_HEADS):
        qit_scr[0:IDX_DIM, h * tq:(h + 1) * tq] = iq_t[IDX_DIM * h:IDX_DIM * (h + 1), :].astype(BF16)
    w_t = misc_ref[...].T

    def p1(c, carry):
        start = pl.multiple_of(c * c_sz, c_sz)
        st_scr[...] = jnp.dot(ikn_ref[pl.ds(start, c_sz), :], qit_scr[...], preferred_element_type=F32)
        acc = None
        for h in range(IDX_HEADS):
            t = w_t[IDX_DIM + h:IDX_DIM + h + 1, :] * jnp.maximum(st_scr[:, h * tq:(h + 1) * tq], 0.0)
            acc = t if acc is None else acc + t
        acc = jnp.where(acc == 0.0, 0.0, acc)
        key_scr[c] = jnp.where(c * c_sz + sub_c <= qpos, _sortable(acc), INT_MIN)
        return carry

    lax.fori_loop(0, nch, p1, 0)

    fold = 4 * SUBLANES

    def count(pred):
        def body(c, cnt):
            m = pred(key_scr[c], c)
            return cnt + jnp.sum(m.reshape(c_sz // fold, fold, tq), axis=0)
        cnt = lax.fori_loop(0, nch, body, jnp.zeros((fold, tq), F32))
        return jnp.sum(cnt, axis=0, keepdims=True)

    kf = float(k_top)
    short = qpos + 1 < k_top
    c0 = count(lambda k, c: jnp.where(k >= 0, 1.0, 0.0))
    cand0 = jnp.where(c0 >= kf, 0, INT_MIN).astype(I32)
    cnt0 = jnp.where(c0 >= kf, c0, 2.0 ** 30)

    def refine(shift, cand, cnt_c):
        trial = cand | jnp.left_shift(jnp.int32(1), shift)
        cnt = count(lambda k, c: jnp.where(k >= trial, 1.0, 0.0))
        return jnp.where(cnt >= kf, trial, cand), jnp.where(cnt >= kf, cnt, cnt_c)

    def unresolved(cnt_c):
        return (jnp.max(jnp.where(short, 0.0, jnp.where(cnt_c == kf, 0.0, 1.0))) > 0.0).astype(I32)

    cand0, cnt0 = refine(30, cand0, cnt0)

    def bits_cond(st):
        return jnp.logical_and(st[0] < 15, st[3] > 0)

    def bits_body(st):
        j, cand, cnt_c, _ = st
        cand, cnt_c = refine(29 - 2 * j, cand, cnt_c)
        cand, cnt_c = refine(28 - 2 * j, cand, cnt_c)
        return j + 1, cand, cnt_c, unresolved(cnt_c)

    _, thr, cnt_ge, _ = lax.while_loop(bits_cond, bits_body, (jnp.int32(0), cand0, cnt0, unresolved(cnt0)))
    need = jnp.where(short, 0.0, jnp.where(cnt_ge > kf, 1.0, 0.0))

    @pl.when(jnp.max(need) > 0.0)
    def _():
        rem = kf - count(lambda k, c: jnp.where(k > thr, 1.0, 0.0))
        nbits = max(1, (seq - 1).bit_length())

        def jb(b, pos):
            trial = pos | jnp.left_shift(jnp.int32(1), nbits - 1 - b)
            cnt = count(lambda k, c: jnp.where(k == thr, jnp.where(c * c_sz + sub_c < trial, 1.0, 0.0), 0.0))
            return jnp.where(cnt < rem, trial, pos)

        pos = lax.fori_loop(0, nbits, jb, jnp.zeros((1, tq), I32))
        cut = jnp.where(need > 0.0, pos, jnp.int32(2 ** 30))

        def demote(c, carry):
            k = key_scr[c]
            key_scr[c] = jnp.where(k == thr, jnp.where(c * c_sz + sub_c > cut, INT_MIN, k), k)
            return carry

        lax.fori_loop(0, nch, demote, 0)

    thr_sel = jnp.where(short, INT_MIN + 1, thr)

    def sel_bias(c):
        return jnp.where(key_scr[c] >= thr_sel, 0.0, NEG).T

    _flash_loop(nch, qa_scr, dk_ref, dv_ref, sel_bias, DSA_HEADS, tq, s_scr, p_scr, alpha_scr, m_scr, l_scr, acc_scr)
    o_ref[...] = _dsa_finish(acc_scr, l_scr, tq)


def _dsa_prompt(p, bp, seq, tq):
    c_sz = p["dkT"].shape[2]
    nq = seq // tq
    k_top = min(IDX_TOPK_MAX, seq // 4)
    qrow = lambda b, i: (b * nq + i, 0)
    kv = lambda b, i: (b, 0)
    kvt = lambda b, i: (b, 0, 0)
    rows = DSA_HEADS * tq
    return pl.pallas_call(
        functools.partial(_dsa_prompt_kernel, k_top=k_top),
        grid=(bp, nq),
        in_specs=[pl.BlockSpec((tq, 256), qrow), pl.BlockSpec((tq, 128), qrow), pl.BlockSpec((tq, 256), qrow),
                  pl.BlockSpec((seq, LANES), kv), pl.BlockSpec((seq // c_sz, 256, c_sz), kvt),
                  pl.BlockSpec((seq, 256), kv)],
        out_specs=pl.BlockSpec((tq, 256), qrow),
        out_shape=jax.ShapeDtypeStruct((bp * seq, BRANCH_W), F32),
        scratch_shapes=[pltpu.VMEM((seq // c_sz, c_sz, tq), I32), pltpu.VMEM((c_sz, IDX_HEADS * tq), F32),
                        pltpu.VMEM((LANES, IDX_HEADS * tq), BF16),
                        pltpu.VMEM((rows, 256), BF16), pltpu.VMEM((rows, LANES), F32),
                        pltpu.VMEM((rows, LANES), F32), pltpu.VMEM((2, rows, LANES), F32),
                        pltpu.VMEM((rows, 256), F32), pltpu.VMEM((2, rows, c_sz), BF16),
                        pltpu.VMEM((2, rows, c_sz), F32)],
        compiler_params=_cparams(2),
        name="dsa_prompt",
    )(p["iq"], p["misc"], p["dq"], p["miscb"], p["dkT"], p["dvb"])


def _diff_prompt_kernel(fq_ref, fk_ref, fv_ref, lam_ref, gsub_ref, o_ref,
                        qa_scr, m_scr, l_scr, alpha_scr, acc_scr, p_scr, s_scr, *, lam_init):
    tq = fq_ref.shape[0]
    c_sz = fk_ref.shape[2]
    nh = 2 * DIFF_HEADS
    i = pl.program_id(1)
    n_full = (i * tq + 1) // c_sz
    qpos = i * tq + lax.broadcasted_iota(I32, (tq, 1), 0)
    lane_c = lax.broadcasted_iota(I32, (tq, c_sz), 1)
    fq = fq_ref[...]
    grp = _group_of_lane((tq, BRANCH_W), 5)
    for h in range(nh):
        qa_scr[h * tq:(h + 1) * tq, :] = jnp.where(grp == h, fq, jnp.zeros_like(fq))

    def causal_bias(c):
        return jnp.where(c * c_sz + lane_c <= qpos, 0.0, NEG)

    _flash_loop(n_full + 1, qa_scr, fk_ref, fv_ref, causal_bias, nh, tq,
                s_scr, p_scr, alpha_scr, m_scr, l_scr, acc_scr)
    lam = _diff_lambda(lam_ref, lam_init)
    o_ref[...] = _diff_finish(acc_scr, l_scr, tq, lam, lam_init, gsub_ref[...])


def _diff_prompt(p, lam_p, gsub, lam_init, bp, seq, tq):
    c_sz = p["fkT"].shape[2]
    nq = seq // tq
    qrow = lambda b, i: (b * nq + i, 0)
    kv = lambda b, i: (b, 0)
    const = lambda b, i: (0, 0)
    rows = 2 * DIFF_HEADS * tq
    return pl.pallas_call(
        functools.partial(_diff_prompt_kernel, lam_init=lam_init),
        grid=(bp, nq),
        in_specs=[pl.BlockSpec((tq, 256), qrow),
                  pl.BlockSpec((seq // c_sz, 256, c_sz), lambda b, i: (b, 0, 0)), pl.BlockSpec((seq, 256), kv),
                  pl.BlockSpec((4, DIFF_HEAD_DIM), const), pl.BlockSpec((1, 256), const)],
        out_specs=pl.BlockSpec((tq, 256), qrow),
        out_shape=jax.ShapeDtypeStruct((bp * seq, BRANCH_W), F32),
        scratch_shapes=[pltpu.VMEM((rows, 256), BF16), pltpu.VMEM((rows, LANES), F32),
                        pltpu.VMEM((rows, LANES), F32), pltpu.VMEM((2, rows, LANES), F32),
                        pltpu.VMEM((rows, 256), F32), pltpu.VMEM((2, rows, c_sz), BF16),
                        pltpu.VMEM((2, rows, c_sz), F32)],
        compiler_params=_cparams(2),
        name="diff_prompt",
    )(p["fq"], p["fkT"], p["fvb"], lam_p, gsub)


def _ssd_kernel(par_ref, xbc_ref, dts_ref, z_ref, cprev_ref, hprev_ref, cw_ref, cb_ref, ng_ref,
                y_ref, hout_ref, cout_ref, e_scr, h_scr):
    qin = xbc_ref.shape[0]
    q = SSD_CHUNK
    c = pl.program_id(1)
    last = pl.num_programs(1) - 1

    @pl.when(c == 0)
    def _():
        e_scr[0:8, :] = cprev_ref[...]
        h_scr[...] = hprev_ref[...]
        if qin < q:
            e_scr[8 + qin:8 + q, :] = jnp.zeros((q - qin, SSD_CONV_DIM), F32)

    e_scr[8:8 + qin, :] = xbc_ref[...]
    conv = cb_ref[...] + jnp.zeros((q, SSD_CONV_DIM), F32)
    for k in range(SSD_CONV):
        conv = conv + e_scr[5 + k:5 + k + q, :] * cw_ref[k:k + 1, :]
    new_tail = e_scr[qin:qin + 8, :]
    u = _silu(conv)
    xs = u[:, :BRANCH_W]
    b_all = u[:, BRANCH_W:BRANCH_W + SSD_GROUPS * SSD_STATE]
    bm_t = b_all.T.astype(BF16)
    bm = b_all.astype(BF16)
    cm = u[:, BRANCH_W + SSD_GROUPS * SSD_STATE:].astype(BF16)

    dts = dts_ref[...]
    if qin < q:
        dts = jnp.concatenate([dts, jnp.zeros((q - qin, LANES), F32)], axis=0)
    rowi = lax.broadcasted_iota(I32, (q, LANES), 0)
    lanei = lax.broadcasted_iota(I32, (q, LANES), 1)
    bias_l = jnp.zeros((q, LANES), F32)
    alog_l = jnp.zeros((q, LANES), F32)
    for h in range(SSD_HEADS):
        bias_l = jnp.where(lanei == h, par_ref[h], bias_l)
        alog_l = jnp.where(lanei == h, par_ref[SSD_HEADS + h], alog_l)
    pre = dts + bias_l
    dt = jnp.maximum(pre, 0.0) + jnp.log1p(jnp.exp(-jnp.abs(pre)))
    dt = jnp.where(rowi < qin, jnp.where(lanei < SSD_HEADS, dt, 0.0), 0.0)
    adt = dt * (-jnp.exp(alog_l))
    r_qq = lax.broadcasted_iota(I32, (q, q), 0)
    c_qq = lax.broadcasted_iota(I32, (q, q), 1)
    causal = r_qq >= c_qq
    acs_col = jnp.dot(jnp.where(causal, 1.0, 0.0), adt, preferred_element_type=F32, precision=HIGHEST)
    acs_row = jnp.dot(adt.T[0:8, :], jnp.where(r_qq <= c_qq, 1.0, 0.0), preferred_element_type=F32,
                      precision=HIGHEST)

    grp = _group_of_lane((q, BRANCH_W), 6)
    grp1 = _group_of_lane((1, BRANCH_W), 6)
    dt_b = jnp.zeros((q, BRANCH_W), F32)
    a_b = jnp.zeros((q, BRANCH_W), F32)
    d_b = jnp.zeros((1, BRANCH_W), F32)
    for h in range(SSD_HEADS):
        dt_b = jnp.where(grp == h, dt[:, h:h + 1], dt_b)
        a_b = jnp.where(grp == h, acs_col[:, h:h + 1], a_b)
        d_b = jnp.where(grp1 == h, par_ref[2 * SSD_HEADS + h], d_b)
    xdt = xs * dt_b
    xdt_b = xdt.astype(BF16)
    a_last = a_b[q - 1:q, :]

    y_diag = jnp.zeros((q, BRANCH_W), F32)
    hpg = SSD_HEADS // SSD_GROUPS
    cbs = [_nk_dot(cm[:, 64 * g:64 * (g + 1)], bm[:, 64 * g:64 * (g + 1)]) for g in range(SSD_GROUPS)]
    for h in range(SSD_HEADS):
        lmat = jnp.exp(jnp.where(causal, acs_col[:, h:h + 1] - acs_row[h:h + 1, :], -jnp.inf))
        y_diag = y_diag + jnp.dot((cbs[h // hpg] * lmat).astype(BF16),
                                  jnp.where(grp == h, xdt_b, jnp.zeros_like(xdt_b)), preferred_element_type=F32)
    ht = h_scr[...]
    htb = ht.astype(BF16)
    xdec = (xdt * jnp.exp(a_last - a_b)).astype(BF16)
    st = [jnp.dot(bm_t[64 * g:64 * (g + 1), :], xdec, preferred_element_type=F32) for g in range(SSD_GROUPS)]
    yo = [jnp.dot(cm[:, 64 * g:64 * (g + 1)], htb, preferred_element_type=F32) for g in range(SSD_GROUPS)]
    grp_n = _group_of_lane((SSD_STATE, BRANCH_W), 6)
    h_scr[...] = ht * jnp.exp(a_last) + jnp.where(grp_n < hpg, st[0], st[1])
    y = y_diag + jnp.where(grp < hpg, yo[0], yo[1]) * jnp.exp(a_b) + xs * d_b
    gte = y * _silu(z_ref[...]) if qin == q else y[0:qin, :] * _silu(z_ref[...])
    half = BRANCH_W // SSD_GROUPS
    outs = []
    for g in range(SSD_GROUPS):
        gg = gte[:, half * g:half * (g + 1)]
        outs.append(gg * lax.rsqrt(jnp.mean(gg * gg, axis=-1, keepdims=True) + EPS))
    y_ref[...] = jnp.concatenate(outs, axis=1) * ng_ref[...]
    e_scr[0:8, :] = new_tail

    @pl.when(c == last)
    def _():
        hout_ref[...] = h_scr[...]
        cout_ref[...] = new_tail


def _ssd(p, par, cprev8, hprev, cw, cb, ng, nb, rows_per_b, qin):
    nc = rows_per_b // qin
    row = lambda b, c: (b * nc + c, 0)
    const = lambda b, c: (0, 0)
    y, hout, cout = pl.pallas_call(
        _ssd_kernel,
        grid=(nb, nc),
        in_specs=[pl.BlockSpec(memory_space=pltpu.SMEM),
                  pl.BlockSpec((qin, SSD_CONV_DIM), row), pl.BlockSpec((qin, LANES), row),
                  pl.BlockSpec((qin, 256), lambda b, c: (b * nc + c, 2)),
                  pl.BlockSpec((None, 8, SSD_CONV_DIM), lambda b, c: (b, 0, 0)),
                  pl.BlockSpec((None, SSD_STATE, BRANCH_W), lambda b, c: (b, 0, 0)),
                  pl.BlockSpec((SSD_CONV, SSD_CONV_DIM), const), pl.BlockSpec((1, SSD_CONV_DIM), const),
                  pl.BlockSpec((1, BRANCH_W), const)],
        out_specs=[pl.BlockSpec((qin, BRANCH_W), row),
                   pl.BlockSpec((None, SSD_STATE, BRANCH_W), lambda b, c: (b, 0, 0)),
                   pl.BlockSpec((None, 8, SSD_CONV_DIM), lambda b, c: (b, 0, 0))],
        out_shape=[jax.ShapeDtypeStruct((nb * rows_per_b, BRANCH_W), F32),
                   jax.ShapeDtypeStruct((nb, SSD_STATE, BRANCH_W), F32),
                   jax.ShapeDtypeStruct((nb, 8, SSD_CONV_DIM), F32)],
        scratch_shapes=[pltpu.VMEM((8 + SSD_CHUNK, SSD_CONV_DIM), F32),
                        pltpu.VMEM((SSD_STATE, BRANCH_W), F32)],
        compiler_params=_cparams(2),
        name="ssd",
    )(par, p["sxbc"], p["dts"], p["gates"], cprev8, _state_to_lanes(hprev), cw, cb, ng)
    return y, _state_from_lanes(hout), cout[:, 8 - (SSD_CONV - 1):, :]


def _state_to_lanes(h):
    b = h.shape[0]
    return h.reshape(b, SSD_HEADS * SSD_HEAD_DIM, SSD_STATE).transpose(0, 2, 1)


def _state_from_lanes(ht):
    b = ht.shape[0]
    return ht.transpose(0, 2, 1).reshape(b, SSD_HEADS, SSD_HEAD_DIM, SSD_STATE)


def _pool_kernel(u_ref, halo_ref, w_ref, scale_ref, o_ref, hout_ref, e_scr, *, start_pos):
    t = u_ref.shape[0]
    c = pl.program_id(1)
    last = pl.num_programs(1) - 1

    @pl.when(c == 0)
    def _():
        e_scr[0:8, :] = jnp.zeros((8, BRANCH_W), F32)
        e_scr[8:24, :] = halo_ref[...]

    u = u_ref[...]
    e_scr[24:24 + t, :] = u
    new_halo = e_scr[8 + t:24 + t, :]
    n = 16 + t
    cur = e_scr[8:8 + n, :]
    stages = []
    for k in (1, 2, 4, 8):
        cur = cur + e_scr[8 - k:8 - k + n, :]
        stages.append(cur[16:, :])
        e_scr[8:8 + n, :] = cur
    grp = _group_of_lane((t, BRANCH_W), 6)
    win = jnp.where(grp == 0, stages[0], jnp.where(grp == 1, stages[1], jnp.where(grp == 2, stages[2], stages[3])))
    wlen = jnp.where(grp == 0, 2.0, jnp.where(grp == 1, 4.0, jnp.where(grp == 2, 8.0, 16.0)))
    n_avail = (start_pos + c * t + 1 + lax.broadcasted_iota(I32, (t, BRANCH_W), 0)).astype(F32)
    d = win / jnp.minimum(wlen, n_avail) - u
    o_ref[...] = jnp.dot(d.astype(BF16), w_ref[...], preferred_element_type=F32) * scale_ref[...]
    e_scr[8:24, :] = new_halo

    @pl.when(c == last)
    def _():
        hout_ref[...] = new_halo


def _pool(u, halo16, wbd, scale, nb, rows_per_b, t, start_pos):
    nc = rows_per_b // t
    row = lambda b, c: (b * nc + c, 0)
    const = lambda b, c: (0, 0)
    o, hout = pl.pallas_call(
        functools.partial(_pool_kernel, start_pos=start_pos),
        grid=(nb, nc),
        in_specs=[pl.BlockSpec((t, BRANCH_W), row), pl.BlockSpec((None, 16, BRANCH_W), lambda b, c: (b, 0, 0)),
                  pl.BlockSpec((BRANCH_W, BRANCH_W), const), pl.BlockSpec((1, BRANCH_W), const)],
        out_specs=[pl.BlockSpec((t, BRANCH_W), row), pl.BlockSpec((None, 16, BRANCH_W), lambda b, c: (b, 0, 0))],
        out_shape=[jax.ShapeDtypeStruct((nb * rows_per_b, BRANCH_W), F32),
                   jax.ShapeDtypeStruct((nb, 16, BRANCH_W), F32)],
        scratch_shapes=[pltpu.VMEM((24 + t, BRANCH_W), F32)],
        compiler_params=_cparams(2),
        name="pool",
    )(u, halo16, wbd, scale)
    return o, hout[:, 1:, :]


def _merge_kernel(x_ref, ng_ref, wmg_ref, dsa_ref, diff_ref, ssd_ref, pool_ref, gates_ref, wbr_ref, wout_ref, y_ref):
    x = x_ref[...]
    h = x * lax.rsqrt(jnp.mean(x * x, axis=-1, keepdims=True) + EPS) * ng_ref[...]
    hb = h.astype(BF16)
    d = x.shape[1]
    br = (dsa_ref[...] * _silu(gates_ref[:, 0:256]),
          diff_ref[...] * _silu(gates_ref[:, 256:512]),
          ssd_ref[...],
          pool_ref[...] * _silu(gates_ref[:, 768:1024]))
    m = jnp.zeros(x.shape, F32)
    for n in range(N_BRANCH):
        mg = jnp.dot(hb, wmg_ref[:, n * d:(n + 1) * d], preferred_element_type=F32)
        up = jnp.dot(br[n].astype(BF16), wbr_ref[n], preferred_element_type=F32)
        m = m + _sigmoid(mg) * up
    y_ref[...] = x + jnp.dot(m.astype(BF16), wout_ref[...], preferred_element_type=F32)


def _merge(x, ng, wmg, dsa_o, diff_o, ssd_o, pool_o, gates, wbr, wout, tm):
    n, d = x.shape
    row = lambda i: (i, 0)
    const = lambda i: (0, 0)
    return pl.pallas_call(
        _merge_kernel,
        grid=(n // tm,),
        in_specs=[pl.BlockSpec((tm, d), row), pl.BlockSpec((1, d), const), pl.BlockSpec(wmg.shape, const),
                  pl.BlockSpec((tm, 256), row), pl.BlockSpec((tm, 256), row), pl.BlockSpec((tm, 256), row),
                  pl.BlockSpec((tm, 256), row), pl.BlockSpec((tm, 1024), row),
                  pl.BlockSpec(wbr.shape, lambda i: (0, 0, 0)), pl.BlockSpec(wout.shape, const)],
        out_specs=pl.BlockSpec((tm, d), row),
        out_shape=jax.ShapeDtypeStruct((n, d), F32),
        compiler_params=_cparams(1),
        name="merge",
    )(x, ng, wmg, dsa_o, diff_o, ssd_o, pool_o, gates, wbr, wout)


_PAGES_PER_STEP = 32
_KEY_CHUNK = 512
_Q_BLOCK = 128
_MERGE_ROWS = 256


def _page_specs(layer, features, npp):
    def spec(k):
        return pl.BlockSpec((None, None, features, PAGE_SIZE),
                            lambda b, g, pt, k=k: (layer, pt[b, g * npp + k], 0, 0))
    return [spec(k) for k in range(npp)]


def _pages_view(cache):
    depth, n_pool, page = cache.shape[:3]
    c = cache.reshape(depth, n_pool, page, -1)
    return jnp.swapaxes(c, 2, 3)


def _dsa_sidx_kernel(pt_ref, iq_ref, misc_ref, *rest, npp, n_pages, k_top):
    pages = rest[:npp]
    bias_ref = rest[npp]
    key_scr, qi_scr, jst_scr = rest[npp + 1:]
    t = iq_ref.shape[0]
    g = pl.program_id(1)
    last = pl.num_programs(1) - 1
    wts = misc_ref[:, 64:64 + IDX_HEADS]

    def scores(kt):
        s = jnp.dot(qi_scr[...], kt, preferred_element_type=F32)
        acc = None
        for h in range(IDX_HEADS):
            v = wts[:, h:h + 1] * jnp.maximum(s[h * t:(h + 1) * t, :], 0.0)
            acc = v if acc is None else acc + v
        return jnp.where(acc == 0.0, 0.0, acc)

    @pl.when(g == 0)
    def _():
        iq = iq_ref[...]
        for h in range(IDX_HEADS):
            qi_scr[h * t:(h + 1) * t, :] = iq[:, IDX_DIM * h:IDX_DIM * (h + 1)]
        knew = jnp.concatenate([misc_ref[...], jnp.zeros((PAGE_SIZE - t, LANES), F32)], axis=0)
        sc = scores(knew.T[0:IDX_DIM, :].astype(BF16))
        causal = lax.broadcasted_iota(I32, (t, LANES), 1) <= lax.broadcasted_iota(I32, (t, LANES), 0)
        key_scr[n_pages] = jnp.where(causal, _sortable(sc), INT_MIN)

    sc = scores(jnp.concatenate([pg[...].astype(BF16) for pg in pages], axis=1))
    for k in range(npp):
        key_scr[g * npp + k] = _sortable(sc[:, k * PAGE_SIZE:(k + 1) * PAGE_SIZE])

    @pl.when(g == last)
    def _():
        keys = key_scr[...]
        kpos = (lax.broadcasted_iota(I32, keys.shape, 0) * PAGE_SIZE + lax.broadcasted_iota(I32, keys.shape, 2))

        def count(m):
            return jnp.sum(jnp.sum(m, axis=0), axis=1, keepdims=True)

        kf = float(k_top)
        c0 = count(jnp.where(keys >= 0, 1.0, 0.0))
        cand0 = jnp.where(c0 >= kf, 0, INT_MIN).astype(I32)

        def bit_body(b, cand):
            trial = cand | jnp.left_shift(jnp.int32(1), 30 - b)
            cnt = count(jnp.where(keys >= trial[None], 1.0, 0.0))
            return jnp.where(cnt >= kf, trial, cand)

        thr = lax.fori_loop(0, 31, bit_body, cand0)
        cnt_gt = count(jnp.where(keys > thr[None], 1.0, 0.0))
        cnt_ge = count(jnp.where(keys >= thr[None], 1.0, 0.0))
        need = jnp.where(cnt_ge > kf, jnp.where(thr > INT_MIN, 1.0, 0.0), 0.0)
        rem = kf - cnt_gt
        big = jnp.int32(2 ** 30)
        jst_scr[...] = jnp.full((t, 1), big, I32)

        @pl.when(jnp.max(need) > 0.0)
        def _():
            nbits = ((n_pages + 1) * PAGE_SIZE - 1).bit_length()

            def jb(b, pos):
                trial = pos | jnp.left_shift(jnp.int32(1), nbits - 1 - b)
                cnt = count(jnp.where(keys == thr[None], jnp.where(kpos < trial[None], 1.0, 0.0), 0.0))
                return jnp.where(cnt < rem, trial, pos)

            pos = lax.fori_loop(0, nbits, jb, jnp.zeros((t, 1), I32))
            jst_scr[...] = jnp.where(need > 0.0, pos, big)

        jst = jst_scr[...]
        sel = jnp.where(keys > thr[None], 0.0,
                        jnp.where(keys == thr[None], jnp.where(kpos <= jst[None], 0.0, NEG), NEG))
        newc = lax.broadcasted_iota(I32, keys.shape, 0) == n_pages
        causal = lax.broadcasted_iota(I32, keys.shape, 2) <= lax.broadcasted_iota(I32, keys.shape, 1)
        bias_ref[...] = jnp.where(newc, jnp.where(causal, sel, NEG), sel)


def _dsa_sidx(ps, cache_ik, layer, page_table, bs, t):
    n_pages = page_table.shape[1]
    npp = math.gcd(n_pages, _PAGES_PER_STEP)
    k_top = min(IDX_TOPK_MAX, (n_pages * PAGE_SIZE + t) // 4)
    row = lambda b, g, pt: (b, 0)
    grid_spec = pltpu.PrefetchScalarGridSpec(
        num_scalar_prefetch=1, grid=(bs, n_pages // npp),
        in_specs=[pl.BlockSpec((t, 256), row), pl.BlockSpec((t, 128), row)] + _page_specs(layer, IDX_DIM, npp),
        out_specs=pl.BlockSpec((None, n_pages + 1, t, LANES), lambda b, g, pt: (b, 0, 0, 0)),
        scratch_shapes=[pltpu.VMEM((n_pages + 1, t, LANES), I32), pltpu.VMEM((IDX_HEADS * t, IDX_DIM), BF16),
                        pltpu.VMEM((t, 1), I32)])
    return pl.pallas_call(
        functools.partial(_dsa_sidx_kernel, npp=npp, n_pages=n_pages, k_top=k_top),
        grid_spec=grid_spec,
        out_shape=jax.ShapeDtypeStruct((bs, n_pages + 1, t, LANES), F32),
        compiler_params=_cparams(2),
        name="dsa_sample_index",
    )(page_table, ps["iq"], ps["misc"], *([cache_ik] * npp))


def _paged_attn_kernel(pt_ref, q_ref, kn_ref, vn_ref, *rest, npp, nheads, log2_hd, use_bias, finish):
    idx = 0
    if use_bias:
        bias_new_ref, bias_pg_ref = rest[0], rest[1]
        idx = 2
    kpages = rest[idx:idx + npp]
    vpages = rest[idx + npp:idx + 2 * npp]
    n_scr = 7
    extra = rest[idx + 2 * npp:-(n_scr + 1)]
    o_ref = rest[-(n_scr + 1)]
    qa_scr, m_scr, l_scr, alpha_scr, acc_scr, p_scr, s_scr = rest[-n_scr:]
    t = q_ref.shape[0]
    rows = nheads * t
    g = pl.program_id(1)
    last = pl.num_programs(1) - 1

    def tile_rows(b):
        return jnp.concatenate([b] * nheads, axis=0)

    def accumulate(pv):
        alpha = alpha_scr[...]
        acc_scr[...] = jnp.concatenate([alpha, alpha], axis=1) * acc_scr[...] + pv

    @pl.when(g == 0)
    def _():
        qv = q_ref[...]
        grp = _group_of_lane((t, BRANCH_W), log2_hd)
        for h in range(nheads):
            qa_scr[h * t:(h + 1) * t, :] = jnp.where(grp == h, qv, jnp.zeros_like(qv))
        m_scr[...] = jnp.full((rows, LANES), NEG, F32)
        l_scr[...] = jnp.zeros((rows, LANES), F32)
        acc_scr[...] = jnp.zeros((rows, BRANCH_W), F32)
        kn = jnp.concatenate([kn_ref[...], jnp.zeros((PAGE_SIZE - t, BRANCH_W), F32)], axis=0)
        vn = jnp.concatenate([vn_ref[...], jnp.zeros((PAGE_SIZE - t, BRANCH_W), BF16)], axis=0)
        if use_bias:
            bias = bias_new_ref[0]
        else:
            causal = lax.broadcasted_iota(I32, (t, LANES), 1) <= lax.broadcasted_iota(I32, (t, LANES), 0)
            bias = jnp.where(causal, 0.0, NEG)
        s_new = s_scr.at[:, 0:PAGE_SIZE]
        p_new = p_scr.at[:, 0:PAGE_SIZE]
        s_new[...] = jnp.dot(qa_scr[...], kn.T.astype(BF16), preferred_element_type=F32)
        _flash_rows(s_new, tile_rows(bias), m_scr, l_scr, alpha_scr, p_new, 0, rows)
        accumulate(jnp.dot(p_new[...], vn, preferred_element_type=F32))

    kt = jnp.concatenate([kp[...].astype(BF16) for kp in kpages], axis=1)
    vt = jnp.concatenate([vp[...].astype(BF16) for vp in vpages], axis=1)
    s_scr[...] = jnp.dot(qa_scr[...], kt, preferred_element_type=F32)
    bias = None
    if use_bias:
        bias = tile_rows(jnp.concatenate([bias_pg_ref[k] for k in range(npp)], axis=1))
    _flash_rows(s_scr, bias, m_scr, l_scr, alpha_scr, p_scr, 0, rows)
    accumulate(_nk_dot(p_scr[...], vt))

    @pl.when(g == last)
    def _():
        o_ref[...] = finish(acc_scr, l_scr, t, *extra)


def _dsa_finish(acc_ref, l_ref, t):
    grp = _group_of_lane((t, BRANCH_W), 6)
    out = jnp.zeros((t, BRANCH_W), F32)
    for h in range(DSA_HEADS):
        l_row = jnp.sum(l_ref[h * t:(h + 1) * t, :], axis=1, keepdims=True)
        out = out + jnp.where(grp == h, acc_ref[h * t:(h + 1) * t, :] / l_row, 0.0)
    return out


def _diff_sample_finish(acc_ref, l_ref, t, lam_ref, gsub_ref, *, lam_init):
    return _diff_finish(acc_ref, l_ref, t, _diff_lambda(lam_ref, lam_init), lam_init, gsub_ref[...])


def _paged_attn(q, kn, vn, cache_k, cache_v, layer, page_table, bs, t, *, nheads, log2_hd, bias=None,
                extra=(), finish, name):
    n_pages = page_table.shape[1]
    npp = math.gcd(n_pages, _PAGES_PER_STEP)
    row = lambda b, g, pt: (b, 0)
    const = lambda b, g, pt: (0, 0)
    in_specs = [pl.BlockSpec((t, 256), row), pl.BlockSpec((t, 256), row), pl.BlockSpec((t, 256), row)]
    args = [q, kn, vn]
    if bias is not None:
        in_specs += [pl.BlockSpec((None, 1, t, LANES), lambda b, g, pt: (b, n_pages, 0, 0)),
                     pl.BlockSpec((None, npp, t, LANES), lambda b, g, pt: (b, g, 0, 0))]
        args += [bias, bias]
    in_specs += _page_specs(layer, 256, npp) + _page_specs(layer, 256, npp)
    args += [cache_k] * npp + [cache_v] * npp
    for e in extra:
        in_specs.append(pl.BlockSpec(e.shape, const))
        args.append(e)
    rows = nheads * t
    grid_spec = pltpu.PrefetchScalarGridSpec(
        num_scalar_prefetch=1, grid=(bs, n_pages // npp), in_specs=in_specs,
        out_specs=pl.BlockSpec((t, 256), row),
        scratch_shapes=[pltpu.VMEM((rows, 256), BF16), pltpu.VMEM((rows, LANES), F32),
                        pltpu.VMEM((rows, LANES), F32), pltpu.VMEM((rows, LANES), F32),
                        pltpu.VMEM((rows, 256), F32), pltpu.VMEM((rows, npp * PAGE_SIZE), BF16),
                        pltpu.VMEM((rows, npp * PAGE_SIZE), F32)])
    return pl.pallas_call(
        functools.partial(_paged_attn_kernel, npp=npp, nheads=nheads, log2_hd=log2_hd,
                          use_bias=bias is not None, finish=finish),
        grid_spec=grid_spec,
        out_shape=jax.ShapeDtypeStruct((bs * t, BRANCH_W), F32),
        compiler_params=_cparams(2),
        name=name,
    )(page_table, *args)


def _rope_table(pos, half, reps, extra_cos=None):
    inv = ROPE_THETA ** (-jnp.arange(half, dtype=F32) / half)
    ang = pos.astype(F32)[:, None] * inv[None, :]
    cos, sin = jnp.cos(ang), jnp.sin(ang)
    cos_h = jnp.concatenate([cos, cos], axis=1)
    sin_h = jnp.concatenate([-sin, sin], axis=1)
    cos_t, sin_t = jnp.tile(cos_h, (1, reps)), jnp.tile(sin_h, (1, reps))
    if extra_cos is not None:
        n = pos.shape[0]
        cos_t = jnp.concatenate([cos_t, jnp.broadcast_to(extra_cos[None, :], (n, extra_cos.shape[0]))], axis=1)
        sin_t = jnp.concatenate([sin_t, jnp.zeros((n, extra_cos.shape[0]), F32)], axis=1)
    return jnp.concatenate([cos_t, sin_t], axis=1)


def _tables(pos):
    misc_scale = jnp.concatenate([jnp.full((IDX_HEADS,), IDX_HEADS ** -0.5, F32),
                                  jnp.ones((LANES - IDX_DIM - IDX_HEADS,), F32)])
    return (_rope_table(pos, DSA_HEAD_DIM // 2, 4), _rope_table(pos, DIFF_HEAD_DIM // 2, 8),
            _rope_table(pos, IDX_DIM // 2, 1, misc_scale))


def _pack_w_in(w):
    sizes = (256, 256, 256, 256, 256, 64, 4, 256, 256, 256, 256, 256, 512, 4, 256, 256, 4096)
    offs = [0]
    for s in sizes:
        offs.append(offs[-1] + s)
    (dq, dk, dv, dg, iq, ik, iw, fq, fk, fv, fg, sz, sxbc, sdt, pu, pg, mg) = [
        w[:, offs[i]:offs[i + 1]] for i in range(len(sizes))]
    d = w.shape[0]
    misc = jnp.concatenate([ik, iw, jnp.zeros((d, LANES - IDX_DIM - IDX_HEADS), w.dtype)], axis=1)
    dts = jnp.concatenate([sdt, jnp.zeros((d, LANES - SSD_HEADS), w.dtype)], axis=1)
    packed = jnp.concatenate([dq, dk, dv, iq, fq, fk, fv, dg, fg, sz, pg, sxbc, pu, misc, dts], axis=1)
    return packed.astype(BF16), mg.astype(BF16)


def _block_diag(w):
    g, n, _ = w.shape
    out = jnp.zeros((g * n, g * n), w.dtype)
    for i in range(g):
        out = out.at[i * n:(i + 1) * n, i * n:(i + 1) * n].set(w[i])
    return out


def kernel(x_prompt, x_sample, cache_dsa_k, cache_dsa_v, cache_idx_k, cache_diff_k, cache_diff_v, state_ssm,
           state_conv, state_pool, page_table, norm_g, w_in, dsa_qk_g, diff_qk_g, diff_lam, diff_subln,
           ssd_conv_w, ssd_conv_b, ssd_dt_bias, ssd_a_log, ssd_d, ssd_norm, pool_w, pool_scale, w_branch, w_out):
    bp, seq, d = x_prompt.shape
    bs, t, _ = x_sample.shape
    depth = norm_g.shape[0]
    n_pages = page_table.shape[1]
    past = n_pages * PAGE_SIZE
    n_pool = cache_dsa_k.shape[1]

    tab_p = _tables(jnp.arange(seq, dtype=I32))
    tab_s = tuple(jnp.tile(a, (bs, 1)) for a in _tables(past + jnp.arange(t, dtype=I32)))
    ck, cv, cik = _pages_view(cache_dsa_k), _pages_view(cache_dsa_v), _pages_view(cache_idx_k)
    cfk, cfv = _pages_view(cache_diff_k), _pages_view(cache_diff_v)

    tm = min(_KEY_CHUNK, seq)
    tq = min(_Q_BLOCK, seq)
    tm_s = min(256, bs * t)
    pool_t = min(512, seq)
    xp = x_prompt.reshape(bp * seq, d)
    xs = x_sample.reshape(bs * t, d)
    acc_p = {n: [] for n in ("dk", "dv", "ik", "fk", "fv", "ssm", "conv", "pool")}
    acc_s = {n: [] for n in acc_p}

    for l in range(depth):
        lam_init = 0.8 - 0.6 * math.exp(-0.3 * l)
        wp, wmg = _pack_w_in(w_in[l])
        ng = norm_g[l].reshape(1, d)
        gains = jnp.stack([jnp.tile(dsa_qk_g[l, 0], 4), jnp.tile(dsa_qk_g[l, 1], 4),
                           jnp.tile(diff_qk_g[l, 0], 8), jnp.tile(diff_qk_g[l, 1], 8)])
        gsub = jnp.tile(diff_subln[l], 4).reshape(1, BRANCH_W)
        lam_p = diff_lam[l]
        ssd_par = jnp.concatenate([ssd_dt_bias[l], ssd_a_log[l], ssd_d[l]]).astype(F32)
        cw, cb = ssd_conv_w[l], ssd_conv_b[l].reshape(1, SSD_CONV_DIM)
        sng = ssd_norm[l].reshape(1, BRANCH_W)
        wbd = _block_diag(pool_w[l]).astype(BF16)
        pscale = pool_scale[l].reshape(1, BRANCH_W)
        wbr = w_branch[l].astype(BF16)
        wout = w_out[l].astype(BF16)

        pp = _proj(xp, ng, wp, *tab_p, gains, tm, seq // tm)
        dsa_o = _dsa_prompt(pp, bp, seq, tq)
        diff_o = _diff_prompt(pp, lam_p, gsub, lam_init, bp, seq, tq)
        ssd_o, ssm_new, conv_new = _ssd(pp, ssd_par, jnp.zeros((bp, 8, SSD_CONV_DIM), F32),
                                        jnp.zeros((bp, SSD_HEADS, SSD_HEAD_DIM, SSD_STATE), F32), cw, cb, sng,
                                        bp, seq, SSD_CHUNK)
        pool_o, pool_new = _pool(pp["pu"], jnp.zeros((bp, 16, BRANCH_W), F32), wbd, pscale, bp, seq, pool_t, 0)
        xp = _merge(xp, ng, wmg, dsa_o, diff_o, ssd_o, pool_o, pp["gates"], wbr, wout, min(_MERGE_ROWS, seq))
        for n, v in zip(acc_p, (pp["dk"].reshape(bp, seq, DSA_HEADS, DSA_HEAD_DIM),
                                pp["dv"].reshape(bp, seq, DSA_HEADS, DSA_HEAD_DIM),
                                pp["misc"][:, :IDX_DIM].reshape(bp, seq, IDX_DIM),
                                pp["fk"].reshape(bp, seq, 2 * DIFF_HEADS, DIFF_HEAD_DIM),
                                pp["fv"].reshape(bp, seq, DIFF_HEADS, DIFF_V_DIM),
                                ssm_new, conv_new, pool_new)):
            acc_p[n].append(v)

        ps = _proj(xs, ng, wp, *tab_s, gains, tm_s, (bs * t) // tm_s)
        bias = _dsa_sidx(ps, cik, l, page_table, bs, t)
        dsa_o = _paged_attn(ps["dq"], ps["dk"], ps["dvb"], ck, cv, l, page_table, bs, t,
                            nheads=DSA_HEADS, log2_hd=6, bias=bias, finish=_dsa_finish, name="dsa_sample_attn")
        diff_o = _paged_attn(ps["fq"], ps["fk"], ps["fvb"], cfk, cfv, l, page_table, bs, t,
                             nheads=2 * DIFF_HEADS, log2_hd=5, extra=(lam_p, gsub),
                             finish=functools.partial(_diff_sample_finish, lam_init=lam_init),
                             name="diff_sample_attn")
        cprev8 = jnp.concatenate([jnp.zeros((bs, 8 - (SSD_CONV - 1), SSD_CONV_DIM), F32), state_conv[l]], axis=1)
        ssd_o, ssm_new, conv_new = _ssd(ps, ssd_par, cprev8, state_ssm[l], cw, cb, sng, bs, t, t)
        halo = jnp.concatenate([jnp.zeros((bs, 1, BRANCH_W), F32), state_pool[l]], axis=1)
        pool_o, pool_new = _pool(ps["pu"], halo, wbd, pscale, bs, t, t, past)
        xs = _merge(xs, ng, wmg, dsa_o, diff_o, ssd_o, pool_o, ps["gates"], wbr, wout, tm_s)
        for n, v in zip(acc_s, (ps["dk"].reshape(bs, t, DSA_HEADS, DSA_HEAD_DIM),
                                ps["dv"].reshape(bs, t, DSA_HEADS, DSA_HEAD_DIM),
                                ps["misc"][:, :IDX_DIM].reshape(bs, t, IDX_DIM),
                                ps["fk"].reshape(bs, t, 2 * DIFF_HEADS, DIFF_HEAD_DIM),
                                ps["fv"].reshape(bs, t, DIFF_HEADS, DIFF_V_DIM),
                                ssm_new, conv_new, pool_new)):
            acc_s[n].append(v)

    names = ("dk", "dv", "ik", "fk", "fv", "ssm", "conv", "pool")
    return ((xp.reshape(bp, seq, d), xs.reshape(bs, t, d))
            + tuple(jnp.stack(acc_p[n]) for n in names) + tuple(jnp.stack(acc_s[n]) for n in names))
```

```python
import functools
import math

import jax
import jax.numpy as jnp
from jax import lax
from jax.experimental import pallas as pl
from jax.experimental.pallas import tpu as pltpu

F32 = jnp.float32
BF16 = jnp.bfloat16
I32 = jnp.int32

BRANCH_W = 256
DSA_HEADS = 4
DSA_HEAD_DIM = 64
IDX_HEADS = 4
IDX_DIM = 64
IDX_TOPK_MAX = 256
DIFF_HEADS = 4
DIFF_HEAD_DIM = 32
DIFF_V_DIM = 64
SSD_HEADS = 4
SSD_HEAD_DIM = 64
SSD_STATE = 64
SSD_GROUPS = 2
SSD_CONV = 4
SSD_CHUNK = 128
SSD_CONV_DIM = 512
POOL_WINDOWS = (2, 4, 8, 16)
POOL_STATE = 15
PAGE_SIZE = 128
ROPE_THETA = 10000.0
EPS = 1e-6
N_BRANCH = 4

LANES = 128
SUBLANES = 8
VMEM_LIMIT = 56 * 1024 * 1024

LOG2E = math.log2(math.e)
NEG = -1e30
INT_MIN = -(2 ** 31)
HIGHEST = lax.Precision.HIGHEST

_SLABS = (("dq", 256), ("dk", 256), ("dv", 256), ("iq", 256), ("fq", 256), ("fk", 256), ("fv", 256),
          ("gates", 1024), ("sxbc", 512), ("pu", 256), ("misc", 128), ("dts", 128))
_OFF = {}
_o = 0
for _n, _w in _SLABS:
    _OFF[_n] = (_o, _o + _w)
    _o += _w
W_PACKED = _o


def _cparams(n_axes, vmem=VMEM_LIMIT):
    return pltpu.CompilerParams(dimension_semantics=("arbitrary",) * n_axes, vmem_limit_bytes=vmem)


def _nk_dot(a, b):
    return lax.dot_general(a, b, (((1,), (1,)), ((), ())), preferred_element_type=F32)


def _silu(x):
    return x * (1.0 / (1.0 + jnp.exp(-x)))


def _sigmoid(x):
    return 1.0 / (1.0 + jnp.exp(-x))


def _group_of_lane(shape, log2_width):
    return lax.broadcasted_iota(I32, shape, len(shape) - 1) >> log2_width


def _proj_kernel(x_ref, ng_ref, w_ref, t64_ref, t32_ref, tmisc_ref, gains_ref,
                 dq_ref, dk_ref, dv_ref, dvb_ref, iq_ref, misc_ref, miscb_ref, fq_ref, fk_ref, fv_ref, fvb_ref,
                 gates_ref, sxbc_ref, pu_ref, dts_ref, dkt_ref, fkt_ref):
    x = x_ref[...]
    h = x * lax.rsqrt(jnp.mean(x * x, axis=-1, keepdims=True) + EPS) * ng_ref[...]
    hb = h.astype(BF16)

    def mm(name):
        a, b = _OFF[name]
        return jnp.dot(hb, w_ref[:, a:b], preferred_element_type=F32)

    def head_norm(z, g, log2_hd):
        n = z.shape[1]
        r = lax.broadcasted_iota(I32, (n, n), 0) >> log2_hd
        c = lax.broadcasted_iota(I32, (n, n), 1) >> log2_hd
        bd = jnp.where(r == c, 1.0 / (1 << log2_hd), 0.0).astype(F32)
        ms = jnp.dot(z * z, bd, preferred_element_type=F32, precision=HIGHEST)
        return z * lax.rsqrt(ms + EPS) * g

    def rope(z, tab_ref, half):
        n = z.shape[1]
        cos = tab_ref[:, :n]
        sin = tab_ref[:, n:]
        lane = lax.broadcasted_iota(I32, z.shape, 1)
        first = (lane & (2 * half - 1)) < half
        partner = jnp.where(first, pltpu.roll(z, n - half, 1), pltpu.roll(z, half, 1))
        return z * cos + partner * sin

    dq = rope(head_norm(mm("dq"), gains_ref[0:1, :], 6), t64_ref, 32)
    dq_ref[...] = (dq * (DSA_HEAD_DIM ** -0.5 * LOG2E)).astype(BF16)
    dk = rope(head_norm(mm("dk"), gains_ref[1:2, :], 6), t64_ref, 32)
    dk_ref[...] = dk
    dkt_ref[...] = dk.T.astype(BF16)
    dv = mm("dv")
    dv_ref[...] = dv
    dvb_ref[...] = dv.astype(BF16)
    iq_ref[...] = (rope(mm("iq"), t64_ref, 32) * (IDX_DIM ** -0.5)).astype(BF16)
    misc = rope(mm("misc"), tmisc_ref, 32)
    misc_ref[...] = misc
    miscb_ref[...] = misc.astype(BF16)
    fq = rope(head_norm(mm("fq"), gains_ref[2:3, :], 5), t32_ref, 16)
    fq_ref[...] = (fq * (DIFF_HEAD_DIM ** -0.5 * LOG2E)).astype(BF16)
    fk = rope(head_norm(mm("fk"), gains_ref[3:4, :], 5), t32_ref, 16)
    fk_ref[...] = fk
    fkt_ref[...] = fk.T.astype(BF16)
    fv = mm("fv")
    fv_ref[...] = fv
    fvb_ref[...] = fv.astype(BF16)
    for j in range(4):
        a = _OFF["gates"][0] + 256 * j
        gates_ref[:, 256 * j:256 * (j + 1)] = jnp.dot(hb, w_ref[:, a:a + 256], preferred_element_type=F32)
    for j in range(2):
        a = _OFF["sxbc"][0] + 256 * j
        sxbc_ref[:, 256 * j:256 * (j + 1)] = jnp.dot(hb, w_ref[:, a:a + 256], preferred_element_type=F32)
    pu_ref[...] = mm("pu")
    dts_ref[...] = mm("dts")


_PROJ_OUTS = (("dq", 256, BF16), ("dk", 256, F32), ("dv", 256, F32), ("dvb", 256, BF16), ("iq", 256, BF16),
              ("misc", 128, F32), ("miscb", 128, BF16), ("fq", 256, BF16), ("fk", 256, F32), ("fv", 256, F32), ("fvb", 256, BF16),
              ("gates", 1024, F32), ("sxbc", 512, F32), ("pu", 256, F32), ("dts", 128, F32))
_PROJ_T_OUTS = (("dkT", 256), ("fkT", 256))


def _proj(x, ng, wp, t64, t32, tmisc, gains, tm, tab_blocks):
    n, d = x.shape
    row = lambda i: (i, 0)
    tab = lambda i: (i % tab_blocks, 0)
    const = lambda i: (0, 0)
    outs = _PROJ_OUTS
    res = pl.pallas_call(
        _proj_kernel,
        grid=(n // tm,),
        in_specs=[pl.BlockSpec((tm, d), row), pl.BlockSpec((1, d), const), pl.BlockSpec(wp.shape, const),
                  pl.BlockSpec((tm, 512), tab), pl.BlockSpec((tm, 512), tab), pl.BlockSpec((tm, 256), tab),
                  pl.BlockSpec((4, 256), const)],
        out_specs=([pl.BlockSpec((tm, w), row) for _, w, _ in outs]
                   + [pl.BlockSpec((None, w, tm), lambda i: (i, 0, 0)) for _, w in _PROJ_T_OUTS]),
        out_shape=([jax.ShapeDtypeStruct((n, w), dt) for _, w, dt in outs]
                   + [jax.ShapeDtypeStruct((n // tm, w, tm), BF16) for _, w in _PROJ_T_OUTS]),
        compiler_params=_cparams(1),
        name="proj",
    )(x, ng, wp, t64, t32, tmisc, gains)
    return dict(zip([o[0] for o in outs] + [o[0] for o in _PROJ_T_OUTS], res))


def _sortable(x):
    bits = pltpu.bitcast(x, I32)
    return bits ^ ((bits >> 31) & 0x7FFFFFFF)


def _flash_rows(s_ref, bias, m_ref, l_ref, alpha_ref, p_ref, r0, r1):
    c_sz = s_ref.shape[1]
    nq = c_sz // LANES
    parts = []
    for q in range(nq):
        v = s_ref[r0:r1, q * LANES:(q + 1) * LANES]
        parts.append(v if bias is None else v + bias[:, q * LANES:(q + 1) * LANES])
    mx = parts[0]
    for q in range(1, nq):
        mx = jnp.maximum(mx, parts[q])
    m_old = m_ref[r0:r1, :]
    m_new = jnp.maximum(m_old, jnp.max(mx, axis=1, keepdims=True))
    alpha = jnp.exp2(m_old - m_new)
    m_ref[r0:r1, :] = m_new
    alpha_ref[r0:r1, :] = alpha
    lsum = None
    for q in range(nq):
        p = jnp.exp2(parts[q] - m_new)
        p_ref[r0:r1, q * LANES:(q + 1) * LANES] = p.astype(BF16)
        lsum = p if lsum is None else lsum + p
    l_ref[r0:r1, :] = alpha * l_ref[r0:r1, :] + lsum


def _flash_loop(n, qa_ref, kt_ref, v_ref, bias_fn, nh, tq, s_scr, p_scr, alpha_scr, m_scr, l_scr, acc_scr):
    c_sz = kt_ref.shape[2]
    rows = nh * tq
    m_scr[...] = jnp.full((rows, LANES), NEG, F32)
    l_scr[...] = jnp.zeros((rows, LANES), F32)
    acc_scr[...] = jnp.zeros((rows, BRANCH_W), F32)

    def scores(c, slot):
        s_scr[slot] = jnp.dot(qa_ref[...], kt_ref[c], preferred_element_type=F32)

    def softmax(c, slot):
        bias = bias_fn(c)
        for h in range(nh):
            _flash_rows(s_scr.at[slot], bias, m_scr, l_scr, alpha_scr.at[slot], p_scr.at[slot],
                        h * tq, (h + 1) * tq)

    def values(c, slot):
        start = pl.multiple_of(c * c_sz, c_sz)
        pv = jnp.dot(p_scr[slot], v_ref[pl.ds(start, c_sz), :], preferred_element_type=F32)
        alpha = alpha_scr[slot]
        acc_scr[...] = jnp.concatenate([alpha, alpha], axis=1) * acc_scr[...] + pv

    scores(0, 0)
    scores(jnp.minimum(1, n - 1), 1)
    softmax(0, 0)

    def body(j, carry):
        c = 2 * j + 1
        scores(jnp.minimum(c + 1, n - 1), 0)
        values(c - 1, 0)
        softmax(c, 1)
        scores(jnp.minimum(c + 2, n - 1), 1)
        values(c, 1)
        softmax(c + 1, 0)
        return carry

    lax.fori_loop(0, (n - 1) // 2, body, 0)

    @pl.when((n - 1) % 2 == 1)
    def _():
        values(n - 2, 0)
        softmax(n - 1, 1)

    values(n - 1, (n - 1) & 1)


def _diff_lambda(lam_ref, lam_init):
    lp = lam_ref[...]
    s1 = jnp.sum(lp[0:1, :] * lp[1:2, :], axis=1, keepdims=True)
    s2 = jnp.sum(lp[2:3, :] * lp[3:4, :], axis=1, keepdims=True)
    return jnp.exp(s1) - jnp.exp(s2) + lam_init


def _diff_finish(acc_ref, l_ref, rows, lam, lam_init, gsub):
    grp = _group_of_lane((rows, BRANCH_W), 6)
    out = jnp.zeros((rows, BRANCH_W), F32)
    for j in range(DIFF_HEADS):
        r1, r2 = 2 * j * rows, (2 * j + 1) * rows
        a1 = acc_ref[r1:r1 + rows, :] / jnp.sum(l_ref[r1:r1 + rows, :], axis=1, keepdims=True)
        a2 = acc_ref[r2:r2 + rows, :] / jnp.sum(l_ref[r2:r2 + rows, :], axis=1, keepdims=True)
        o = a1 - lam * a2
        ms = jnp.sum(jnp.where(grp == j, o * o, 0.0), axis=1, keepdims=True) * (1.0 / DIFF_V_DIM)
        out = out + jnp.where(grp == j, o * lax.rsqrt(ms + EPS), 0.0)
    return out * gsub * (1.0 - lam_init)


def _dsa_prompt_kernel(iq_ref, misc_ref, dq_ref, ikn_ref, dk_ref, dv_ref, tril_ref, o_ref,
                       key_scr, st_scr, qit_scr, qa_scr, m_scr, l_scr, alpha_scr, acc_scr, p_scr, s_scr, *, k_top):
    tq = iq_ref.shape[0]
    c_sz = dk_ref.shape[2]
    seq = dk_ref.shape[0] * c_sz
    i = pl.program_id(1)
    nch = ((i + 1) * tq + c_sz - 1) // c_sz
    qpos = i * tq + lax.broadcasted_iota(I32, (1, tq), 1)
    sub_c = lax.broadcasted_iota(I32, (c_sz, tq), 0)

    dq = dq_ref[...]
    grp = _group_of_lane((tq, BRANCH_W), 6)
    for h in range(DSA_HEADS):
        qa_scr[h * tq:(h + 1) * tq, :] = jnp.where(grp == h, dq, jnp.zeros_like(dq))
    iq_t = iq_ref[...].astype(F32).T
    qit_scr[IDX_DIM:, :] = jnp.zeros((LANES - IDX_DIM, IDX_HEADS * tq), BF16)
    for h in range(IDX_HEADS):
        qit_scr[0:IDX_DIM, h * tq:(h + 1) * tq] = iq_t[IDX_DIM * h:IDX_DIM * (h + 1), :].astype(BF16)
    w_t = misc_ref[...].T

    def p1(c, carry):
        start = pl.multiple_of(c * c_sz, c_sz)
        st_scr[...] = jnp.dot(ikn_ref[pl.ds(start, c_sz), :], qit_scr[...], preferred_element_type=F32)
        acc = None
        for h in range(IDX_HEADS):
            t = w_t[IDX_DIM + h:IDX_DIM + h + 1, :] * jnp.maximum(st_scr[:, h * tq:(h + 1) * tq], 0.0)
            acc = t if acc is None else acc + t
        acc = jnp.where(acc == 0.0, 0.0, acc)
        key_scr[c] = jnp.where(c * c_sz + sub_c <= qpos, _sortable(acc), INT_MIN)
        return carry

    lax.fori_loop(0, nch, p1, 0)

    fold = 4 * SUBLANES

    def count(pred):
        def body(c, cnt):
            m = pred(key_scr[c], c)
            return cnt + jnp.sum(m.reshape(c_sz // fold, fold, tq), axis=0)
        cnt = lax.fori_loop(0, nch, body, jnp.zeros((fold, tq), F32))
        return jnp.sum(cnt, axis=0, keepdims=True)

    kf = float(k_top)
    short = qpos + 1 < k_top
    c0 = count(lambda k, c: jnp.where(k >= 0, 1.0, 0.0))
    cand0 = jnp.where(c0 >= kf, 0, INT_MIN).astype(I32)
    cnt0 = jnp.where(c0 >= kf, c0, 2.0 ** 30)

    def refine(shift, cand, cnt_c):
        trial = cand | jnp.left_shift(jnp.int32(1), shift)
        cnt = count(lambda k, c: jnp.where(k >= trial, 1.0, 0.0))
        return jnp.where(cnt >= kf, trial, cand), jnp.where(cnt >= kf, cnt, cnt_c)

    cand0, cnt0 = refine(30, cand0, cnt0)
    probe = cand0 + 1
    tied = jnp.where(count(lambda k, c: jnp.where(k >= probe, 1.0, 0.0)) < kf, 1.0, 0.0)
    settled = jnp.where(short, 1.0, tied)

    def unresolved(cnt_c):
        return (jnp.max(jnp.where(settled > 0.0, 0.0, jnp.where(cnt_c == kf, 0.0, 1.0))) > 0.0).astype(I32)

    def bits_cond(st):
        return jnp.logical_and(st[0] < 15, st[3] > 0)

    def bits_body(st):
        j, cand, cnt_c, _ = st
        cand, cnt_c = refine(29 - 2 * j, cand, cnt_c)
        cand, cnt_c = refine(28 - 2 * j, cand, cnt_c)
        return j + 1, cand, cnt_c, unresolved(cnt_c)

    _, thr, cnt_ge, _ = lax.while_loop(bits_cond, bits_body, (jnp.int32(0), cand0, cnt0, unresolved(cnt0)))
    need = jnp.where(short, 0.0, jnp.where(cnt_ge > kf, 1.0, 0.0))

    @pl.when(jnp.max(need) > 0.0)
    def _():
        rem = kf - count(lambda k, c: jnp.where(k > thr, 1.0, 0.0))
        rem = jnp.where(need > 0.0, rem, 2.0 ** 30)

        def demote(c, base):
            k = key_scr[c]
            tie = jnp.where(k == thr, 1.0, 0.0)
            rank = base + jnp.dot(tril_ref[...], tie.astype(BF16), preferred_element_type=F32)
            key_scr[c] = jnp.where(k == thr, jnp.where(rank > rem, INT_MIN, k), k)
            return rank[c_sz - 1:c_sz, :]

        lax.fori_loop(0, nch, demote, jnp.zeros((1, tq), F32))

    thr_sel = jnp.where(short, INT_MIN + 1, thr)

    def sel_bias(c):
        return jnp.where(key_scr[c] >= thr_sel, 0.0, NEG).T

    _flash_loop(nch, qa_scr, dk_ref, dv_ref, sel_bias, DSA_HEADS, tq, s_scr, p_scr, alpha_scr, m_scr, l_scr, acc_scr)
    o_ref[...] = _dsa_finish(acc_scr, l_scr, tq)


def _dsa_prompt(p, bp, seq, tq):
    c_sz = p["dkT"].shape[2]
    nq = seq // tq
    k_top = min(IDX_TOPK_MAX, seq // 4)
    qrow = lambda b, i: (b * nq + i, 0)
    kv = lambda b, i: (b, 0)
    kvt = lambda b, i: (b, 0, 0)
    rows = DSA_HEADS * tq
    return pl.pallas_call(
        functools.partial(_dsa_prompt_kernel, k_top=k_top),
        grid=(bp, nq),
        in_specs=[pl.BlockSpec((tq, 256), qrow), pl.BlockSpec((tq, 128), qrow), pl.BlockSpec((tq, 256), qrow),
                  pl.BlockSpec((seq, LANES), kv), pl.BlockSpec((seq // c_sz, 256, c_sz), kvt),
                  pl.BlockSpec((seq, 256), kv), pl.BlockSpec((c_sz, c_sz), lambda b, i: (0, 0))],
        out_specs=pl.BlockSpec((tq, 256), qrow),
        out_shape=jax.ShapeDtypeStruct((bp * seq, BRANCH_W), F32),
        scratch_shapes=[pltpu.VMEM((seq // c_sz, c_sz, tq), I32), pltpu.VMEM((c_sz, IDX_HEADS * tq), F32),
                        pltpu.VMEM((LANES, IDX_HEADS * tq), BF16),
                        pltpu.VMEM((rows, 256), BF16), pltpu.VMEM((rows, LANES), F32),
                        pltpu.VMEM((rows, LANES), F32), pltpu.VMEM((2, rows, LANES), F32),
                        pltpu.VMEM((rows, 256), F32), pltpu.VMEM((2, rows, c_sz), BF16),
                        pltpu.VMEM((2, rows, c_sz), F32)],
        compiler_params=_cparams(2),
        name="dsa_prompt",
    )(p["iq"], p["misc"], p["dq"], p["miscb"], p["dkT"], p["dvb"], jnp.tril(jnp.ones((c_sz, c_sz), BF16)))


def _diff_prompt_kernel(fq_ref, fk_ref, fv_ref, lam_ref, gsub_ref, o_ref,
                        qa_scr, m_scr, l_scr, alpha_scr, acc_scr, p_scr, s_scr, *, lam_init):
    tq = fq_ref.shape[0]
    c_sz = fk_ref.shape[2]
    nh = 2 * DIFF_HEADS
    i = pl.program_id(1)
    n_full = (i * tq + 1) // c_sz
    qpos = i * tq + lax.broadcasted_iota(I32, (tq, 1), 0)
    lane_c = lax.broadcasted_iota(I32, (tq, c_sz), 1)
    fq = fq_ref[...]
    grp = _group_of_lane((tq, BRANCH_W), 5)
    for h in range(nh):
        qa_scr[h * tq:(h + 1) * tq, :] = jnp.where(grp == h, fq, jnp.zeros_like(fq))

    def causal_bias(c):
        return jnp.where(c * c_sz + lane_c <= qpos, 0.0, NEG)

    _flash_loop(n_full + 1, qa_scr, fk_ref, fv_ref, causal_bias, nh, tq,
                s_scr, p_scr, alpha_scr, m_scr, l_scr, acc_scr)
    lam = _diff_lambda(lam_ref, lam_init)
    o_ref[...] = _diff_finish(acc_scr, l_scr, tq, lam, lam_init, gsub_ref[...])


def _diff_prompt(p, lam_p, gsub, lam_init, bp, seq, tq):
    c_sz = p["fkT"].shape[2]
    nq = seq // tq
    qrow = lambda b, i: (b * nq + i, 0)
    kv = lambda b, i: (b, 0)
    const = lambda b, i: (0, 0)
    rows = 2 * DIFF_HEADS * tq
    return pl.pallas_call(
        functools.partial(_diff_prompt_kernel, lam_init=lam_init),
        grid=(bp, nq),
        in_specs=[pl.BlockSpec((tq, 256), qrow),
                  pl.BlockSpec((seq // c_sz, 256, c_sz), lambda b, i: (b, 0, 0)), pl.BlockSpec((seq, 256), kv),
                  pl.BlockSpec((4, DIFF_HEAD_DIM), const), pl.BlockSpec((1, 256), const)],
        out_specs=pl.BlockSpec((tq, 256), qrow),
        out_shape=jax.ShapeDtypeStruct((bp * seq, BRANCH_W), F32),
        scratch_shapes=[pltpu.VMEM((rows, 256), BF16), pltpu.VMEM((rows, LANES), F32),
                        pltpu.VMEM((rows, LANES), F32), pltpu.VMEM((2, rows, LANES), F32),
                        pltpu.VMEM((rows, 256), F32), pltpu.VMEM((2, rows, c_sz), BF16),
                        pltpu.VMEM((2, rows, c_sz), F32)],
        compiler_params=_cparams(2),
        name="diff_prompt",
    )(p["fq"], p["fkT"], p["fvb"], lam_p, gsub)


def _ssd_kernel(par_ref, xbc_ref, dts_ref, z_ref, cprev_ref, hprev_ref, cw_ref, cb_ref, ng_ref,
                y_ref, hout_ref, cout_ref, e_scr, h_scr):
    qin = xbc_ref.shape[0]
    q = SSD_CHUNK
    c = pl.program_id(1)
    last = pl.num_programs(1) - 1

    @pl.when(c == 0)
    def _():
        e_scr[0:8, :] = cprev_ref[...]
        h_scr[...] = hprev_ref[...]
        if qin < q:
            e_scr[8 + qin:8 + q, :] = jnp.zeros((q - qin, SSD_CONV_DIM), F32)

    e_scr[8:8 + qin, :] = xbc_ref[...]
    conv = cb_ref[...] + jnp.zeros((q, SSD_CONV_DIM), F32)
    for k in range(SSD_CONV):
        conv = conv + e_scr[5 + k:5 + k + q, :] * cw_ref[k:k + 1, :]
    new_tail = e_scr[qin:qin + 8, :]
    u = _silu(conv)
    xs = u[:, :BRANCH_W]
    b_all = u[:, BRANCH_W:BRANCH_W + SSD_GROUPS * SSD_STATE]
    bm_t = b_all.T.astype(BF16)
    bm = b_all.astype(BF16)
    cm = u[:, BRANCH_W + SSD_GROUPS * SSD_STATE:].astype(BF16)

    dts = dts_ref[...]
    if qin < q:
        dts = jnp.concatenate([dts, jnp.zeros((q - qin, LANES), F32)], axis=0)
    rowi = lax.broadcasted_iota(I32, (q, LANES), 0)
    lanei = lax.broadcasted_iota(I32, (q, LANES), 1)
    bias_l = jnp.zeros((q, LANES), F32)
    alog_l = jnp.zeros((q, LANES), F32)
    for h in range(SSD_HEADS):
        bias_l = jnp.where(lanei == h, par_ref[h], bias_l)
        alog_l = jnp.where(lanei == h, par_ref[SSD_HEADS + h], alog_l)
    pre = dts + bias_l
    dt = jnp.maximum(pre, 0.0) + jnp.log1p(jnp.exp(-jnp.abs(pre)))
    dt = jnp.where(rowi < qin, jnp.where(lanei < SSD_HEADS, dt, 0.0), 0.0)
    adt = dt * (-jnp.exp(alog_l))
    r_qq = lax.broadcasted_iota(I32, (q, q), 0)
    c_qq = lax.broadcasted_iota(I32, (q, q), 1)
    causal = r_qq >= c_qq
    acs_col = jnp.dot(jnp.where(causal, 1.0, 0.0), adt, preferred_element_type=F32, precision=HIGHEST)
    acs_row = jnp.dot(adt.T[0:8, :], jnp.where(r_qq <= c_qq, 1.0, 0.0), preferred_element_type=F32,
                      precision=HIGHEST)

    grp = _group_of_lane((q, BRANCH_W), 6)
    grp1 = _group_of_lane((1, BRANCH_W), 6)
    dt_b = jnp.zeros((q, BRANCH_W), F32)
    a_b = jnp.zeros((q, BRANCH_W), F32)
    d_b = jnp.zeros((1, BRANCH_W), F32)
    for h in range(SSD_HEADS):
        dt_b = jnp.where(grp == h, dt[:, h:h + 1], dt_b)
        a_b = jnp.where(grp == h, acs_col[:, h:h + 1], a_b)
        d_b = jnp.where(grp1 == h, par_ref[2 * SSD_HEADS + h], d_b)
    xdt = xs * dt_b
    xdt_b = xdt.astype(BF16)
    a_last = a_b[q - 1:q, :]

    y_diag = jnp.zeros((q, BRANCH_W), F32)
    hpg = SSD_HEADS // SSD_GROUPS
    cbs = [_nk_dot(cm[:, 64 * g:64 * (g + 1)], bm[:, 64 * g:64 * (g + 1)]) for g in range(SSD_GROUPS)]
    for h in range(SSD_HEADS):
        lmat = jnp.exp(jnp.where(causal, acs_col[:, h:h + 1] - acs_row[h:h + 1, :], -jnp.inf))
        y_diag = y_diag + jnp.dot((cbs[h // hpg] * lmat).astype(BF16),
                                  jnp.where(grp == h, xdt_b, jnp.zeros_like(xdt_b)), preferred_element_type=F32)
    ht = h_scr[...]
    htb = ht.astype(BF16)
    xdec = (xdt * jnp.exp(a_last - a_b)).astype(BF16)
    st = [jnp.dot(bm_t[64 * g:64 * (g + 1), :], xdec, preferred_element_type=F32) for g in range(SSD_GROUPS)]
    yo = [jnp.dot(cm[:, 64 * g:64 * (g + 1)], htb, preferred_element_type=F32) for g in range(SSD_GROUPS)]
    grp_n = _group_of_lane((SSD_STATE, BRANCH_W), 6)
    h_scr[...] = ht * jnp.exp(a_last) + jnp.where(grp_n < hpg, st[0], st[1])
    y = y_diag + jnp.where(grp < hpg, yo[0], yo[1]) * jnp.exp(a_b) + xs * d_b
    gte = y * _silu(z_ref[...]) if qin == q else y[0:qin, :] * _silu(z_ref[...])
    half = BRANCH_W // SSD_GROUPS
    outs = []
    for g in range(SSD_GROUPS):
        gg = gte[:, half * g:half * (g + 1)]
        outs.append(gg * lax.rsqrt(jnp.mean(gg * gg, axis=-1, keepdims=True) + EPS))
    y_ref[...] = jnp.concatenate(outs, axis=1) * ng_ref[...]
    e_scr[0:8, :] = new_tail

    @pl.when(c == last)
    def _():
        hout_ref[...] = h_scr[...]
        cout_ref[...] = new_tail


def _ssd(p, par, cprev8, hprev, cw, cb, ng, nb, rows_per_b, qin):
    nc = rows_per_b // qin
    row = lambda b, c: (b * nc + c, 0)
    const = lambda b, c: (0, 0)
    y, hout, cout = pl.pallas_call(
        _ssd_kernel,
        grid=(nb, nc),
        in_specs=[pl.BlockSpec(memory_space=pltpu.SMEM),
                  pl.BlockSpec((qin, SSD_CONV_DIM), row), pl.BlockSpec((qin, LANES), row),
                  pl.BlockSpec((qin, 256), lambda b, c: (b * nc + c, 2)),
                  pl.BlockSpec((None, 8, SSD_CONV_DIM), lambda b, c: (b, 0, 0)),
                  pl.BlockSpec((None, SSD_STATE, BRANCH_W), lambda b, c: (b, 0, 0)),
                  pl.BlockSpec((SSD_CONV, SSD_CONV_DIM), const), pl.BlockSpec((1, SSD_CONV_DIM), const),
                  pl.BlockSpec((1, BRANCH_W), const)],
        out_specs=[pl.BlockSpec((qin, BRANCH_W), row),
                   pl.BlockSpec((None, SSD_STATE, BRANCH_W), lambda b, c: (b, 0, 0)),
                   pl.BlockSpec((None, 8, SSD_CONV_DIM), lambda b, c: (b, 0, 0))],
        out_shape=[jax.ShapeDtypeStruct((nb * rows_per_b, BRANCH_W), F32),
                   jax.ShapeDtypeStruct((nb, SSD_STATE, BRANCH_W), F32),
                   jax.ShapeDtypeStruct((nb, 8, SSD_CONV_DIM), F32)],
        scratch_shapes=[pltpu.VMEM((8 + SSD_CHUNK, SSD_CONV_DIM), F32),
                        pltpu.VMEM((SSD_STATE, BRANCH_W), F32)],
        compiler_params=_cparams(2),
        name="ssd",
    )(par, p["sxbc"], p["dts"], p["gates"], cprev8, _state_to_lanes(hprev), cw, cb, ng)
    return y, _state_from_lanes(hout), cout[:, 8 - (SSD_CONV - 1):, :]


def _state_to_lanes(h):
    b = h.shape[0]
    return h.reshape(b, SSD_HEADS * SSD_HEAD_DIM, SSD_STATE).transpose(0, 2, 1)


def _state_from_lanes(ht):
    b = ht.shape[0]
    return ht.transpose(0, 2, 1).reshape(b, SSD_HEADS, SSD_HEAD_DIM, SSD_STATE)


def _pool_kernel(u_ref, halo_ref, w_ref, scale_ref, o_ref, hout_ref, e_scr, *, start_pos):
    t = u_ref.shape[0]
    c = pl.program_id(1)
    last = pl.num_programs(1) - 1

    @pl.when(c == 0)
    def _():
        e_scr[0:8, :] = jnp.zeros((8, BRANCH_W), F32)
        e_scr[8:24, :] = halo_ref[...]

    u = u_ref[...]
    e_scr[24:24 + t, :] = u
    new_halo = e_scr[8 + t:24 + t, :]
    n = 16 + t
    cur = e_scr[8:8 + n, :]
    stages = []
    for k in (1, 2, 4, 8):
        cur = cur + e_scr[8 - k:8 - k + n, :]
        stages.append(cur[16:, :])
        e_scr[8:8 + n, :] = cur
    grp = _group_of_lane((t, BRANCH_W), 6)
    win = jnp.where(grp == 0, stages[0], jnp.where(grp == 1, stages[1], jnp.where(grp == 2, stages[2], stages[3])))
    wlen = jnp.where(grp == 0, 2.0, jnp.where(grp == 1, 4.0, jnp.where(grp == 2, 8.0, 16.0)))
    n_avail = (start_pos + c * t + 1 + lax.broadcasted_iota(I32, (t, BRANCH_W), 0)).astype(F32)
    d = win / jnp.minimum(wlen, n_avail) - u
    o_ref[...] = jnp.dot(d.astype(BF16), w_ref[...], preferred_element_type=F32) * scale_ref[...]
    e_scr[8:24, :] = new_halo

    @pl.when(c == last)
    def _():
        hout_ref[...] = new_halo


def _pool(u, halo16, wbd, scale, nb, rows_per_b, t, start_pos):
    nc = rows_per_b // t
    row = lambda b, c: (b * nc + c, 0)
    const = lambda b, c: (0, 0)
    o, hout = pl.pallas_call(
        functools.partial(_pool_kernel, start_pos=start_pos),
        grid=(nb, nc),
        in_specs=[pl.BlockSpec((t, BRANCH_W), row), pl.BlockSpec((None, 16, BRANCH_W), lambda b, c: (b, 0, 0)),
                  pl.BlockSpec((BRANCH_W, BRANCH_W), const), pl.BlockSpec((1, BRANCH_W), const)],
        out_specs=[pl.BlockSpec((t, BRANCH_W), row), pl.BlockSpec((None, 16, BRANCH_W), lambda b, c: (b, 0, 0))],
        out_shape=[jax.ShapeDtypeStruct((nb * rows_per_b, BRANCH_W), F32),
                   jax.ShapeDtypeStruct((nb, 16, BRANCH_W), F32)],
        scratch_shapes=[pltpu.VMEM((24 + t, BRANCH_W), F32)],
        compiler_params=_cparams(2),
        name="pool",
    )(u, halo16, wbd, scale)
    return o, hout[:, 1:, :]


def _merge_kernel(x_ref, ng_ref, wmg_ref, dsa_ref, diff_ref, ssd_ref, pool_ref, gates_ref, wbr_ref, wout_ref, y_ref):
    x = x_ref[...]
    h = x * lax.rsqrt(jnp.mean(x * x, axis=-1, keepdims=True) + EPS) * ng_ref[...]
    hb = h.astype(BF16)
    d = x.shape[1]
    br = (dsa_ref[...] * _silu(gates_ref[:, 0:256]),
          diff_ref[...] * _silu(gates_ref[:, 256:512]),
          ssd_ref[...],
          pool_ref[...] * _silu(gates_ref[:, 768:1024]))
    m = jnp.zeros(x.shape, F32)
    for n in range(N_BRANCH):
        mg = jnp.dot(hb, wmg_ref[:, n * d:(n + 1) * d], preferred_element_type=F32)
        up = jnp.dot(br[n].astype(BF16), wbr_ref[n], preferred_element_type=F32)
        m = m + _sigmoid(mg) * up
    y_ref[...] = x + jnp.dot(m.astype(BF16), wout_ref[...], preferred_element_type=F32)


def _merge(x, ng, wmg, dsa_o, diff_o, ssd_o, pool_o, gates, wbr, wout, tm):
    n, d = x.shape
    row = lambda i: (i, 0)
    const = lambda i: (0, 0)
    return pl.pallas_call(
        _merge_kernel,
        grid=(n // tm,),
        in_specs=[pl.BlockSpec((tm, d), row), pl.BlockSpec((1, d), const), pl.BlockSpec(wmg.shape, const),
                  pl.BlockSpec((tm, 256), row), pl.BlockSpec((tm, 256), row), pl.BlockSpec((tm, 256), row),
                  pl.BlockSpec((tm, 256), row), pl.BlockSpec((tm, 1024), row),
                  pl.BlockSpec(wbr.shape, lambda i: (0, 0, 0)), pl.BlockSpec(wout.shape, const)],
        out_specs=pl.BlockSpec((tm, d), row),
        out_shape=jax.ShapeDtypeStruct((n, d), F32),
        compiler_params=_cparams(1),
        name="merge",
    )(x, ng, wmg, dsa_o, diff_o, ssd_o, pool_o, gates, wbr, wout)


_PAGES_PER_STEP = 32
_KEY_CHUNK = 512
_Q_BLOCK = 128
_MERGE_ROWS = 256


def _page_specs(layer, features, npp):
    def spec(k):
        return pl.BlockSpec((None, None, features, PAGE_SIZE),
                            lambda b, g, pt, k=k: (layer, pt[b, g * npp + k], 0, 0))
    return [spec(k) for k in range(npp)]


def _pages_view(cache):
    depth, n_pool, page = cache.shape[:3]
    c = cache.reshape(depth, n_pool, page, -1)
    return jnp.swapaxes(c, 2, 3)


def _dsa_sidx_kernel(pt_ref, iq_ref, misc_ref, *rest, npp, n_pages, k_top):
    pages = rest[:npp]
    bias_ref = rest[npp]
    key_scr, qi_scr, jst_scr = rest[npp + 1:]
    t = iq_ref.shape[0]
    g = pl.program_id(1)
    last = pl.num_programs(1) - 1
    wts = misc_ref[:, 64:64 + IDX_HEADS]

    def scores(kt):
        s = jnp.dot(qi_scr[...], kt, preferred_element_type=F32)
        acc = None
        for h in range(IDX_HEADS):
            v = wts[:, h:h + 1] * jnp.maximum(s[h * t:(h + 1) * t, :], 0.0)
            acc = v if acc is None else acc + v
        return jnp.where(acc == 0.0, 0.0, acc)

    @pl.when(g == 0)
    def _():
        iq = iq_ref[...]
        for h in range(IDX_HEADS):
            qi_scr[h * t:(h + 1) * t, :] = iq[:, IDX_DIM * h:IDX_DIM * (h + 1)]
        knew = jnp.concatenate([misc_ref[...], jnp.zeros((PAGE_SIZE - t, LANES), F32)], axis=0)
        sc = scores(knew.T[0:IDX_DIM, :].astype(BF16))
        causal = lax.broadcasted_iota(I32, (t, LANES), 1) <= lax.broadcasted_iota(I32, (t, LANES), 0)
        key_scr[n_pages] = jnp.where(causal, _sortable(sc), INT_MIN)

    sc = scores(jnp.concatenate([pg[...].astype(BF16) for pg in pages], axis=1))
    for k in range(npp):
        key_scr[g * npp + k] = _sortable(sc[:, k * PAGE_SIZE:(k + 1) * PAGE_SIZE])

    @pl.when(g == last)
    def _():
        keys = key_scr[...]
        kpos = (lax.broadcasted_iota(I32, keys.shape, 0) * PAGE_SIZE + lax.broadcasted_iota(I32, keys.shape, 2))

        def count(m):
            return jnp.sum(jnp.sum(m, axis=0), axis=1, keepdims=True)

        kf = float(k_top)
        c0 = count(jnp.where(keys >= 0, 1.0, 0.0))
        cand0 = jnp.where(c0 >= kf, 0, INT_MIN).astype(I32)

        def bit_body(b, cand):
            trial = cand | jnp.left_shift(jnp.int32(1), 30 - b)
            cnt = count(jnp.where(keys >= trial[None], 1.0, 0.0))
            return jnp.where(cnt >= kf, trial, cand)

        thr = lax.fori_loop(0, 31, bit_body, cand0)
        cnt_gt = count(jnp.where(keys > thr[None], 1.0, 0.0))
        cnt_ge = count(jnp.where(keys >= thr[None], 1.0, 0.0))
        need = jnp.where(cnt_ge > kf, jnp.where(thr > INT_MIN, 1.0, 0.0), 0.0)
        rem = kf - cnt_gt
        big = jnp.int32(2 ** 30)
        jst_scr[...] = jnp.full((t, 1), big, I32)

        @pl.when(jnp.max(need) > 0.0)
        def _():
            nbits = ((n_pages + 1) * PAGE_SIZE - 1).bit_length()

            def jb(b, pos):
                trial = pos | jnp.left_shift(jnp.int32(1), nbits - 1 - b)
                cnt = count(jnp.where(keys == thr[None], jnp.where(kpos < trial[None], 1.0, 0.0), 0.0))
                return jnp.where(cnt < rem, trial, pos)

            pos = lax.fori_loop(0, nbits, jb, jnp.zeros((t, 1), I32))
            jst_scr[...] = jnp.where(need > 0.0, pos, big)

        jst = jst_scr[...]
        sel = jnp.where(keys > thr[None], 0.0,
                        jnp.where(keys == thr[None], jnp.where(kpos <= jst[None], 0.0, NEG), NEG))
        newc = lax.broadcasted_iota(I32, keys.shape, 0) == n_pages
        causal = lax.broadcasted_iota(I32, keys.shape, 2) <= lax.broadcasted_iota(I32, keys.shape, 1)
        bias_ref[...] = jnp.where(newc, jnp.where(causal, sel, NEG), sel)


def _dsa_sidx(ps, cache_ik, layer, page_table, bs, t):
    n_pages = page_table.shape[1]
    npp = math.gcd(n_pages, _PAGES_PER_STEP)
    k_top = min(IDX_TOPK_MAX, (n_pages * PAGE_SIZE + t) // 4)
    row = lambda b, g, pt: (b, 0)
    grid_spec = pltpu.PrefetchScalarGridSpec(
        num_scalar_prefetch=1, grid=(bs, n_pages // npp),
        in_specs=[pl.BlockSpec((t, 256), row), pl.BlockSpec((t, 128), row)] + _page_specs(layer, IDX_DIM, npp),
        out_specs=pl.BlockSpec((None, n_pages + 1, t, LANES), lambda b, g, pt: (b, 0, 0, 0)),
        scratch_shapes=[pltpu.VMEM((n_pages + 1, t, LANES), I32), pltpu.VMEM((IDX_HEADS * t, IDX_DIM), BF16),
                        pltpu.VMEM((t, 1), I32)])
    return pl.pallas_call(
        functools.partial(_dsa_sidx_kernel, npp=npp, n_pages=n_pages, k_top=k_top),
        grid_spec=grid_spec,
        out_shape=jax.ShapeDtypeStruct((bs, n_pages + 1, t, LANES), F32),
        compiler_params=_cparams(2),
        name="dsa_sample_index",
    )(page_table, ps["iq"], ps["misc"], *([cache_ik] * npp))


def _paged_attn_kernel(pt_ref, q_ref, kn_ref, vn_ref, *rest, npp, nheads, log2_hd, use_bias, finish):
    idx = 0
    if use_bias:
        bias_new_ref, bias_pg_ref = rest[0], rest[1]
        idx = 2
    kpages = rest[idx:idx + npp]
    vpages = rest[idx + npp:idx + 2 * npp]
    n_scr = 7
    extra = rest[idx + 2 * npp:-(n_scr + 1)]
    o_ref = rest[-(n_scr + 1)]
    qa_scr, m_scr, l_scr, alpha_scr, acc_scr, p_scr, s_scr = rest[-n_scr:]
    t = q_ref.shape[0]
    rows = nheads * t
    g = pl.program_id(1)
    last = pl.num_programs(1) - 1

    def tile_rows(b):
        return jnp.concatenate([b] * nheads, axis=0)

    def accumulate(pv):
        alpha = alpha_scr[...]
        acc_scr[...] = jnp.concatenate([alpha, alpha], axis=1) * acc_scr[...] + pv

    @pl.when(g == 0)
    def _():
        qv = q_ref[...]
        grp = _group_of_lane((t, BRANCH_W), log2_hd)
        for h in range(nheads):
            qa_scr[h * t:(h + 1) * t, :] = jnp.where(grp == h, qv, jnp.zeros_like(qv))
        m_scr[...] = jnp.full((rows, LANES), NEG, F32)
        l_scr[...] = jnp.zeros((rows, LANES), F32)
        acc_scr[...] = jnp.zeros((rows, BRANCH_W), F32)
        kn = jnp.concatenate([kn_ref[...], jnp.zeros((PAGE_SIZE - t, BRANCH_W), F32)], axis=0)
        vn = jnp.concatenate([vn_ref[...], jnp.zeros((PAGE_SIZE - t, BRANCH_W), BF16)], axis=0)
        if use_bias:
            bias = bias_new_ref[0]
        else:
            causal = lax.broadcasted_iota(I32, (t, LANES), 1) <= lax.broadcasted_iota(I32, (t, LANES), 0)
            bias = jnp.where(causal, 0.0, NEG)
        s_new = s_scr.at[:, 0:PAGE_SIZE]
        p_new = p_scr.at[:, 0:PAGE_SIZE]
        s_new[...] = jnp.dot(qa_scr[...], kn.T.astype(BF16), preferred_element_type=F32)
        _flash_rows(s_new, tile_rows(bias), m_scr, l_scr, alpha_scr, p_new, 0, rows)
        accumulate(jnp.dot(p_new[...], vn, preferred_element_type=F32))

    kt = jnp.concatenate([kp[...].astype(BF16) for kp in kpages], axis=1)
    vt = jnp.concatenate([vp[...].astype(BF16) for vp in vpages], axis=1)
    s_scr[...] = jnp.dot(qa_scr[...], kt, preferred_element_type=F32)
    bias = None
    if use_bias:
        bias = tile_rows(jnp.concatenate([bias_pg_ref[k] for k in range(npp)], axis=1))
    _flash_rows(s_scr, bias, m_scr, l_scr, alpha_scr, p_scr, 0, rows)
    accumulate(_nk_dot(p_scr[...], vt))

    @pl.when(g == last)
    def _():
        o_ref[...] = finish(acc_scr, l_scr, t, *extra)


def _dsa_finish(acc_ref, l_ref, t):
    grp = _group_of_lane((t, BRANCH_W), 6)
    out = jnp.zeros((t, BRANCH_W), F32)
    for h in range(DSA_HEADS):
        l_row = jnp.sum(l_ref[h * t:(h + 1) * t, :], axis=1, keepdims=True)
        out = out + jnp.where(grp == h, acc_ref[h * t:(h + 1) * t, :] / l_row, 0.0)
    return out


def _diff_sample_finish(acc_ref, l_ref, t, lam_ref, gsub_ref, *, lam_init):
    return _diff_finish(acc_ref, l_ref, t, _diff_lambda(lam_ref, lam_init), lam_init, gsub_ref[...])


def _paged_attn(q, kn, vn, cache_k, cache_v, layer, page_table, bs, t, *, nheads, log2_hd, bias=None,
                extra=(), finish, name):
    n_pages = page_table.shape[1]
    npp = math.gcd(n_pages, _PAGES_PER_STEP)
    row = lambda b, g, pt: (b, 0)
    const = lambda b, g, pt: (0, 0)
    in_specs = [pl.BlockSpec((t, 256), row), pl.BlockSpec((t, 256), row), pl.BlockSpec((t, 256), row)]
    args = [q, kn, vn]
    if bias is not None:
        in_specs += [pl.BlockSpec((None, 1, t, LANES), lambda b, g, pt: (b, n_pages, 0, 0)),
                     pl.BlockSpec((None, npp, t, LANES), lambda b, g, pt: (b, g, 0, 0))]
        args += [bias, bias]
    in_specs += _page_specs(layer, 256, npp) + _page_specs(layer, 256, npp)
    args += [cache_k] * npp + [cache_v] * npp
    for e in extra:
        in_specs.append(pl.BlockSpec(e.shape, const))
        args.append(e)
    rows = nheads * t
    grid_spec = pltpu.PrefetchScalarGridSpec(
        num_scalar_prefetch=1, grid=(bs, n_pages // npp), in_specs=in_specs,
        out_specs=pl.BlockSpec((t, 256), row),
        scratch_shapes=[pltpu.VMEM((rows, 256), BF16), pltpu.VMEM((rows, LANES), F32),
                        pltpu.VMEM((rows, LANES), F32), pltpu.VMEM((rows, LANES), F32),
                        pltpu.VMEM((rows, 256), F32), pltpu.VMEM((rows, npp * PAGE_SIZE), BF16),
                        pltpu.VMEM((rows, npp * PAGE_SIZE), F32)])
    return pl.pallas_call(
        functools.partial(_paged_attn_kernel, npp=npp, nheads=nheads, log2_hd=log2_hd,
                          use_bias=bias is not None, finish=finish),
        grid_spec=grid_spec,
        out_shape=jax.ShapeDtypeStruct((bs * t, BRANCH_W), F32),
        compiler_params=_cparams(2),
        name=name,
    )(page_table, *args)


def _rope_table(pos, half, reps, extra_cos=None):
    inv = ROPE_THETA ** (-jnp.arange(half, dtype=F32) / half)
    ang = pos.astype(F32)[:, None] * inv[None, :]
    cos, sin = jnp.cos(ang), jnp.sin(ang)
    cos_h = jnp.concatenate([cos, cos], axis=1)
    sin_h = jnp.concatenate([-sin, sin], axis=1)
    cos_t, sin_t = jnp.tile(cos_h, (1, reps)), jnp.tile(sin_h, (1, reps))
    if extra_cos is not None:
        n = pos.shape[0]
        cos_t = jnp.concatenate([cos_t, jnp.broadcast_to(extra_cos[None, :], (n, extra_cos.shape[0]))], axis=1)
        sin_t = jnp.concatenate([sin_t, jnp.zeros((n, extra_cos.shape[0]), F32)], axis=1)
    return jnp.concatenate([cos_t, sin_t], axis=1)


def _tables(pos):
    misc_scale = jnp.concatenate([jnp.full((IDX_HEADS,), IDX_HEADS ** -0.5, F32),
                                  jnp.ones((LANES - IDX_DIM - IDX_HEADS,), F32)])
    return (_rope_table(pos, DSA_HEAD_DIM // 2, 4), _rope_table(pos, DIFF_HEAD_DIM // 2, 8),
            _rope_table(pos, IDX_DIM // 2, 1, misc_scale))


def _pack_w_in(w):
    sizes = (256, 256, 256, 256, 256, 64, 4, 256, 256, 256, 256, 256, 512, 4, 256, 256, 4096)
    offs = [0]
    for s in sizes:
        offs.append(offs[-1] + s)
    (dq, dk, dv, dg, iq, ik, iw, fq, fk, fv, fg, sz, sxbc, sdt, pu, pg, mg) = [
        w[:, offs[i]:offs[i + 1]] for i in range(len(sizes))]
    d = w.shape[0]
    misc = jnp.concatenate([ik, iw, jnp.zeros((d, LANES - IDX_DIM - IDX_HEADS), w.dtype)], axis=1)
    dts = jnp.concatenate([sdt, jnp.zeros((d, LANES - SSD_HEADS), w.dtype)], axis=1)
    packed = jnp.concatenate([dq, dk, dv, iq, fq, fk, fv, dg, fg, sz, pg, sxbc, pu, misc, dts], axis=1)
    return packed.astype(BF16), mg.astype(BF16)


def _block_diag(w):
    g, n, _ = w.shape
    out = jnp.zeros((g * n, g * n), w.dtype)
    for i in range(g):
        out = out.at[i * n:(i + 1) * n, i * n:(i + 1) * n].set(w[i])
    return out


def kernel(x_prompt, x_sample, cache_dsa_k, cache_dsa_v, cache_idx_k, cache_diff_k, cache_diff_v, state_ssm,
           state_conv, state_pool, page_table, norm_g, w_in, dsa_qk_g, diff_qk_g, diff_lam, diff_subln,
           ssd_conv_w, ssd_conv_b, ssd_dt_bias, ssd_a_log, ssd_d, ssd_norm, pool_w, pool_scale, w_branch, w_out):
    bp, seq, d = x_prompt.shape
    bs, t, _ = x_sample.shape
    depth = norm_g.shape[0]
    n_pages = page_table.shape[1]
    past = n_pages * PAGE_SIZE
    n_pool = cache_dsa_k.shape[1]

    tab_p = _tables(jnp.arange(seq, dtype=I32))
    tab_s = tuple(jnp.tile(a, (bs, 1)) for a in _tables(past + jnp.arange(t, dtype=I32)))
    ck, cv, cik = _pages_view(cache_dsa_k), _pages_view(cache_dsa_v), _pages_view(cache_idx_k)
    cfk, cfv = _pages_view(cache_diff_k), _pages_view(cache_diff_v)

    tm = min(_KEY_CHUNK, seq)
    tq = min(_Q_BLOCK, seq)
    tm_s = min(256, bs * t)
    pool_t = min(512, seq)
    xp = x_prompt.reshape(bp * seq, d)
    xs = x_sample.reshape(bs * t, d)
    acc_p = {n: [] for n in ("dk", "dv", "ik", "fk", "fv", "ssm", "conv", "pool")}
    acc_s = {n: [] for n in acc_p}

    for l in range(depth):
        lam_init = 0.8 - 0.6 * math.exp(-0.3 * l)
        wp, wmg = _pack_w_in(w_in[l])
        ng = norm_g[l].reshape(1, d)
        gains = jnp.stack([jnp.tile(dsa_qk_g[l, 0], 4), jnp.tile(dsa_qk_g[l, 1], 4),
                           jnp.tile(diff_qk_g[l, 0], 8), jnp.tile(diff_qk_g[l, 1], 8)])
        gsub = jnp.tile(diff_subln[l], 4).reshape(1, BRANCH_W)
        lam_p = diff_lam[l]
        ssd_par = jnp.concatenate([ssd_dt_bias[l], ssd_a_log[l], ssd_d[l]]).astype(F32)
        cw, cb = ssd_conv_w[l], ssd_conv_b[l].reshape(1, SSD_CONV_DIM)
        sng = ssd_norm[l].reshape(1, BRANCH_W)
        wbd = _block_diag(pool_w[l]).astype(BF16)
        pscale = pool_scale[l].reshape(1, BRANCH_W)
        wbr = w_branch[l].astype(BF16)
        wout = w_out[l].astype(BF16)

        pp = _proj(xp, ng, wp, *tab_p, gains, tm, seq // tm)
        dsa_o = _dsa_prompt(pp, bp, seq, tq)
        diff_o = _diff_prompt(pp, lam_p, gsub, lam_init, bp, seq, tq)
        ssd_o, ssm_new, conv_new = _ssd(pp, ssd_par, jnp.zeros((bp, 8, SSD_CONV_DIM), F32),
                                        jnp.zeros((bp, SSD_HEADS, SSD_HEAD_DIM, SSD_STATE), F32), cw, cb, sng,
                                        bp, seq, SSD_CHUNK)
        pool_o, pool_new = _pool(pp["pu"], jnp.zeros((bp, 16, BRANCH_W), F32), wbd, pscale, bp, seq, pool_t, 0)
        xp = _merge(xp, ng, wmg, dsa_o, diff_o, ssd_o, pool_o, pp["gates"], wbr, wout, min(_MERGE_ROWS, seq))
        for n, v in zip(acc_p, (pp["dk"].reshape(bp, seq, DSA_HEADS, DSA_HEAD_DIM),
                                pp["dv"].reshape(bp, seq, DSA_HEADS, DSA_HEAD_DIM),
                                pp["misc"][:, :IDX_DIM].reshape(bp, seq, IDX_DIM),
                                pp["fk"].reshape(bp, seq, 2 * DIFF_HEADS, DIFF_HEAD_DIM),
                                pp["fv"].reshape(bp, seq, DIFF_HEADS, DIFF_V_DIM),
                                ssm_new, conv_new, pool_new)):
            acc_p[n].append(v)

        ps = _proj(xs, ng, wp, *tab_s, gains, tm_s, (bs * t) // tm_s)
        bias = _dsa_sidx(ps, cik, l, page_table, bs, t)
        dsa_o = _paged_attn(ps["dq"], ps["dk"], ps["dvb"], ck, cv, l, page_table, bs, t,
                            nheads=DSA_HEADS, log2_hd=6, bias=bias, finish=_dsa_finish, name="dsa_sample_attn")
        diff_o = _paged_attn(ps["fq"], ps["fk"], ps["fvb"], cfk, cfv, l, page_table, bs, t,
                             nheads=2 * DIFF_HEADS, log2_hd=5, extra=(lam_p, gsub),
                             finish=functools.partial(_diff_sample_finish, lam_init=lam_init),
                             name="diff_sample_attn")
        cprev8 = jnp.concatenate([jnp.zeros((bs, 8 - (SSD_CONV - 1), SSD_CONV_DIM), F32), state_conv[l]], axis=1)
        ssd_o, ssm_new, conv_new = _ssd(ps, ssd_par, cprev8, state_ssm[l], cw, cb, sng, bs, t, t)
        halo = jnp.concatenate([jnp.zeros((bs, 1, BRANCH_W), F32), state_pool[l]], axis=1)
        pool_o, pool_new = _pool(ps["pu"], halo, wbd, pscale, bs, t, t, past)
        xs = _merge(xs, ng, wmg, dsa_o, diff_o, ssd_o, pool_o, ps["gates"], wbr, wout, tm_s)
        for n, v in zip(acc_s, (ps["dk"].reshape(bs, t, DSA_HEADS, DSA_HEAD_DIM),
                                ps["dv"].reshape(bs, t, DSA_HEADS, DSA_HEAD_DIM),
                                ps["misc"][:, :IDX_DIM].reshape(bs, t, IDX_DIM),
                                ps["fk"].reshape(bs, t, 2 * DIFF_HEADS, DIFF_HEAD_DIM),
                                ps["fv"].reshape(bs, t, DIFF_HEADS, DIFF_V_DIM),
                                ssm_new, conv_new, pool_new)):
            acc_s[n].append(v)

    names = ("dk", "dv", "ik", "fk", "fv", "ssm", "conv", "pool")
    return ((xp.reshape(bp, seq, d), xs.reshape(bs, t, d))
            + tuple(jnp.stack(acc_p[n]) for n in names) + tuple(jnp.stack(acc_s[n]) for n in names))
```

```python
import functools
import math

import jax
import jax.numpy as jnp
from jax import lax
from jax.experimental import pallas as pl
from jax.experimental.pallas import tpu as pltpu

F32 = jnp.float32
BF16 = jnp.bfloat16
I32 = jnp.int32

BRANCH_W = 256
DSA_HEADS = 4
DSA_HEAD_DIM = 64
IDX_HEADS = 4
IDX_DIM = 64
IDX_TOPK_MAX = 256
DIFF_HEADS = 4
DIFF_HEAD_DIM = 32
DIFF_V_DIM = 64
SSD_HEADS = 4
SSD_HEAD_DIM = 64
SSD_STATE = 64
SSD_GROUPS = 2
SSD_CONV = 4
SSD_CHUNK = 128
SSD_CONV_DIM = 512
POOL_WINDOWS = (2, 4, 8, 16)
POOL_STATE = 15
PAGE_SIZE = 128
ROPE_THETA = 10000.0
EPS = 1e-6
N_BRANCH = 4

LANES = 128
SUBLANES = 8
VMEM_LIMIT = 56 * 1024 * 1024

LOG2E = math.log2(math.e)
NEG = -1e30
INT_MIN = -(2 ** 31)
HIGHEST = lax.Precision.HIGHEST

_SLABS = (("dq", 256), ("dk", 256), ("dv", 256), ("iq", 256), ("fq", 256), ("fk", 256), ("fv", 256),
          ("gates", 1024), ("sxbc", 512), ("pu", 256), ("misc", 128), ("dts", 128))
_OFF = {}
_o = 0
for _n, _w in _SLABS:
    _OFF[_n] = (_o, _o + _w)
    _o += _w
W_PACKED = _o


def _cparams(n_axes, vmem=VMEM_LIMIT):
    return pltpu.CompilerParams(dimension_semantics=("arbitrary",) * n_axes, vmem_limit_bytes=vmem)


def _nk_dot(a, b):
    return lax.dot_general(a, b, (((1,), (1,)), ((), ())), preferred_element_type=F32)


def _silu(x):
    return x * (1.0 / (1.0 + jnp.exp(-x)))


def _sigmoid(x):
    return 1.0 / (1.0 + jnp.exp(-x))


def _group_of_lane(shape, log2_width):
    return lax.broadcasted_iota(I32, shape, len(shape) - 1) >> log2_width


def _proj_kernel(x_ref, ng_ref, w_ref, t64_ref, t32_ref, tmisc_ref, gains_ref,
                 dq_ref, dk_ref, dv_ref, dvb_ref, iq_ref, misc_ref, miscb_ref, fq_ref, fk_ref, fv_ref, fvb_ref,
                 gates_ref, sxbc_ref, pu_ref, dts_ref, dkt_ref, fkt_ref,
                 dktf_ref, dvtf_ref, iktf_ref, fktf_ref, fvtf_ref):
    x = x_ref[...]
    h = x * lax.rsqrt(jnp.mean(x * x, axis=-1, keepdims=True) + EPS) * ng_ref[...]
    hb = h.astype(BF16)

    def mm(name):
        a, b = _OFF[name]
        return jnp.dot(hb, w_ref[:, a:b], preferred_element_type=F32)

    def head_norm(z, g, log2_hd):
        n = z.shape[1]
        r = lax.broadcasted_iota(I32, (n, n), 0) >> log2_hd
        c = lax.broadcasted_iota(I32, (n, n), 1) >> log2_hd
        bd = jnp.where(r == c, 1.0 / (1 << log2_hd), 0.0).astype(F32)
        ms = jnp.dot(z * z, bd, preferred_element_type=F32, precision=HIGHEST)
        return z * lax.rsqrt(ms + EPS) * g

    def rope(z, tab_ref, half):
        n = z.shape[1]
        cos = tab_ref[:, :n]
        sin = tab_ref[:, n:]
        lane = lax.broadcasted_iota(I32, z.shape, 1)
        first = (lane & (2 * half - 1)) < half
        partner = jnp.where(first, pltpu.roll(z, n - half, 1), pltpu.roll(z, half, 1))
        return z * cos + partner * sin

    dq = rope(head_norm(mm("dq"), gains_ref[0:1, :], 6), t64_ref, 32)
    dq_ref[...] = (dq * (DSA_HEAD_DIM ** -0.5 * LOG2E)).astype(BF16)
    dk = rope(head_norm(mm("dk"), gains_ref[1:2, :], 6), t64_ref, 32)
    dk_ref[...] = dk
    dk_t = dk.T
    dktf_ref[...] = dk_t
    dkt_ref[...] = dk_t.astype(BF16)
    dv = mm("dv")
    dv_ref[...] = dv
    dvtf_ref[...] = dv.T
    dvb_ref[...] = dv.astype(BF16)
    iq_ref[...] = (rope(mm("iq"), t64_ref, 32) * (IDX_DIM ** -0.5)).astype(BF16)
    misc = rope(mm("misc"), tmisc_ref, 32)
    misc_ref[...] = misc
    iktf_ref[...] = misc.T[0:IDX_DIM, :]
    miscb_ref[...] = misc.astype(BF16)
    fq = rope(head_norm(mm("fq"), gains_ref[2:3, :], 5), t32_ref, 16)
    fq_ref[...] = (fq * (DIFF_HEAD_DIM ** -0.5 * LOG2E)).astype(BF16)
    fk = rope(head_norm(mm("fk"), gains_ref[3:4, :], 5), t32_ref, 16)
    fk_ref[...] = fk
    fk_t = fk.T
    fktf_ref[...] = fk_t
    fkt_ref[...] = fk_t.astype(BF16)
    fv = mm("fv")
    fv_ref[...] = fv
    fvtf_ref[...] = fv.T
    fvb_ref[...] = fv.astype(BF16)
    for j in range(4):
        a = _OFF["gates"][0] + 256 * j
        gates_ref[:, 256 * j:256 * (j + 1)] = jnp.dot(hb, w_ref[:, a:a + 256], preferred_element_type=F32)
    for j in range(2):
        a = _OFF["sxbc"][0] + 256 * j
        sxbc_ref[:, 256 * j:256 * (j + 1)] = jnp.dot(hb, w_ref[:, a:a + 256], preferred_element_type=F32)
    pu_ref[...] = mm("pu")
    dts_ref[...] = mm("dts")


_PROJ_OUTS = (("dq", 256, BF16), ("dk", 256, F32), ("dv", 256, F32), ("dvb", 256, BF16), ("iq", 256, BF16),
              ("misc", 128, F32), ("miscb", 128, BF16), ("fq", 256, BF16), ("fk", 256, F32), ("fv", 256, F32), ("fvb", 256, BF16),
              ("gates", 1024, F32), ("sxbc", 512, F32), ("pu", 256, F32), ("dts", 128, F32))
_PROJ_T_OUTS = (("dkT", 256), ("fkT", 256))
_PROJ_TF_OUTS = (("dkTf", 256), ("dvTf", 256), ("ikTf", IDX_DIM), ("fkTf", 256), ("fvTf", 256))


def _proj(x, ng, wp, t64, t32, tmisc, gains, tm, tab_blocks):
    n, d = x.shape
    row = lambda i: (i, 0)
    tab = lambda i: (i % tab_blocks, 0)
    const = lambda i: (0, 0)
    run = lambda i: (i // tab_blocks, 0, i % tab_blocks)
    n_runs = n // (tm * tab_blocks)
    outs = _PROJ_OUTS
    res = pl.pallas_call(
        _proj_kernel,
        grid=(n // tm,),
        in_specs=[pl.BlockSpec((tm, d), row), pl.BlockSpec((1, d), const), pl.BlockSpec(wp.shape, const),
                  pl.BlockSpec((tm, 512), tab), pl.BlockSpec((tm, 512), tab), pl.BlockSpec((tm, 256), tab),
                  pl.BlockSpec((4, 256), const)],
        out_specs=([pl.BlockSpec((tm, w), row) for _, w, _ in outs]
                   + [pl.BlockSpec((None, w, tm), lambda i: (i, 0, 0)) for _, w in _PROJ_T_OUTS]
                   + [pl.BlockSpec((None, w, tm), run) for _, w in _PROJ_TF_OUTS]),
        out_shape=([jax.ShapeDtypeStruct((n, w), dt) for _, w, dt in outs]
                   + [jax.ShapeDtypeStruct((n // tm, w, tm), BF16) for _, w in _PROJ_T_OUTS]
                   + [jax.ShapeDtypeStruct((n_runs, w, tm * tab_blocks), F32) for _, w in _PROJ_TF_OUTS]),
        compiler_params=_cparams(1),
        name="proj",
    )(x, ng, wp, t64, t32, tmisc, gains)
    return dict(zip([o[0] for o in outs] + [o[0] for o in _PROJ_T_OUTS] + [o[0] for o in _PROJ_TF_OUTS], res))


def _sortable(x):
    bits = pltpu.bitcast(x, I32)
    return bits ^ ((bits >> 31) & 0x7FFFFFFF)


def _flash_rows(s_ref, bias, m_ref, l_ref, alpha_ref, p_ref, r0, r1):
    c_sz = s_ref.shape[1]
    nq = c_sz // LANES
    parts = []
    for q in range(nq):
        v = s_ref[r0:r1, q * LANES:(q + 1) * LANES]
        parts.append(v if bias is None else v + bias[:, q * LANES:(q + 1) * LANES])
    mx = parts[0]
    for q in range(1, nq):
        mx = jnp.maximum(mx, parts[q])
    m_old = m_ref[r0:r1, :]
    m_new = jnp.maximum(m_old, jnp.max(mx, axis=1, keepdims=True))
    alpha = jnp.exp2(m_old - m_new)
    m_ref[r0:r1, :] = m_new
    alpha_ref[r0:r1, :] = alpha
    lsum = None
    for q in range(nq):
        p = jnp.exp2(parts[q] - m_new)
        p_ref[r0:r1, q * LANES:(q + 1) * LANES] = p.astype(BF16)
        lsum = p if lsum is None else lsum + p
    l_ref[r0:r1, :] = alpha * l_ref[r0:r1, :] + lsum


def _flash_loop(n, qa_ref, kt_ref, v_ref, bias_fn, nh, tq, s_scr, p_scr, alpha_scr, m_scr, l_scr, acc_scr):
    c_sz = kt_ref.shape[2]
    rows = nh * tq
    m_scr[...] = jnp.full((rows, LANES), NEG, F32)
    l_scr[...] = jnp.zeros((rows, LANES), F32)
    acc_scr[...] = jnp.zeros((rows, BRANCH_W), F32)

    def scores(c, slot):
        s_scr[slot] = jnp.dot(qa_ref[...], kt_ref[c], preferred_element_type=F32)

    def softmax(c, slot):
        bias = bias_fn(c)
        for h in range(nh):
            _flash_rows(s_scr.at[slot], bias, m_scr, l_scr, alpha_scr.at[slot], p_scr.at[slot],
                        h * tq, (h + 1) * tq)

    def values(c, slot):
        start = pl.multiple_of(c * c_sz, c_sz)
        pv = jnp.dot(p_scr[slot], v_ref[pl.ds(start, c_sz), :], preferred_element_type=F32)
        alpha = alpha_scr[slot]
        acc_scr[...] = jnp.concatenate([alpha, alpha], axis=1) * acc_scr[...] + pv

    scores(0, 0)
    scores(jnp.minimum(1, n - 1), 1)
    softmax(0, 0)

    def body(j, carry):
        c = 2 * j + 1
        scores(jnp.minimum(c + 1, n - 1), 0)
        values(c - 1, 0)
        softmax(c, 1)
        scores(jnp.minimum(c + 2, n - 1), 1)
        values(c, 1)
        softmax(c + 1, 0)
        return carry

    lax.fori_loop(0, (n - 1) // 2, body, 0)

    @pl.when((n - 1) % 2 == 1)
    def _():
        values(n - 2, 0)
        softmax(n - 1, 1)

    values(n - 1, (n - 1) & 1)


def _diff_lambda(lam_ref, lam_init):
    lp = lam_ref[...]
    s1 = jnp.sum(lp[0:1, :] * lp[1:2, :], axis=1, keepdims=True)
    s2 = jnp.sum(lp[2:3, :] * lp[3:4, :], axis=1, keepdims=True)
    return jnp.exp(s1) - jnp.exp(s2) + lam_init


def _diff_finish(acc_ref, l_ref, rows, lam, lam_init, gsub):
    grp = _group_of_lane((rows, BRANCH_W), 6)
    out = jnp.zeros((rows, BRANCH_W), F32)
    for j in range(DIFF_HEADS):
        r1, r2 = 2 * j * rows, (2 * j + 1) * rows
        a1 = acc_ref[r1:r1 + rows, :] / jnp.sum(l_ref[r1:r1 + rows, :], axis=1, keepdims=True)
        a2 = acc_ref[r2:r2 + rows, :] / jnp.sum(l_ref[r2:r2 + rows, :], axis=1, keepdims=True)
        o = a1 - lam * a2
        ms = jnp.sum(jnp.where(grp == j, o * o, 0.0), axis=1, keepdims=True) * (1.0 / DIFF_V_DIM)
        out = out + jnp.where(grp == j, o * lax.rsqrt(ms + EPS), 0.0)
    return out * gsub * (1.0 - lam_init)


def _dsa_prompt_kernel(iq_ref, misc_ref, dq_ref, ikn_ref, dk_ref, dv_ref, tril_ref, o_ref,
                       key_scr, st_scr, qit_scr, qa_scr, m_scr, l_scr, alpha_scr, acc_scr, p_scr, s_scr, *, k_top):
    tq = iq_ref.shape[0]
    c_sz = dk_ref.shape[2]
    seq = dk_ref.shape[0] * c_sz
    i = pl.program_id(1)
    nch = ((i + 1) * tq + c_sz - 1) // c_sz
    qpos = i * tq + lax.broadcasted_iota(I32, (1, tq), 1)
    sub_c = lax.broadcasted_iota(I32, (c_sz, tq), 0)

    dq = dq_ref[...]
    grp = _group_of_lane((tq, BRANCH_W), 6)
    for h in range(DSA_HEADS):
        qa_scr[h * tq:(h + 1) * tq, :] = jnp.where(grp == h, dq, jnp.zeros_like(dq))
    iq_t = iq_ref[...].astype(F32).T
    qit_scr[IDX_DIM:, :] = jnp.zeros((LANES - IDX_DIM, IDX_HEADS * tq), BF16)
    for h in range(IDX_HEADS):
        qit_scr[0:IDX_DIM, h * tq:(h + 1) * tq] = iq_t[IDX_DIM * h:IDX_DIM * (h + 1), :].astype(BF16)
    w_t = misc_ref[...].T

    def p1(c, carry):
        start = pl.multiple_of(c * c_sz, c_sz)
        st_scr[...] = jnp.dot(ikn_ref[pl.ds(start, c_sz), :], qit_scr[...], preferred_element_type=F32)
        acc = None
        for h in range(IDX_HEADS):
            t = w_t[IDX_DIM + h:IDX_DIM + h + 1, :] * jnp.maximum(st_scr[:, h * tq:(h + 1) * tq], 0.0)
            acc = t if acc is None else acc + t
        acc = jnp.where(acc == 0.0, 0.0, acc)
        key_scr[c] = jnp.where(c * c_sz + sub_c <= qpos, _sortable(acc), INT_MIN)
        return carry

    lax.fori_loop(0, nch, p1, 0)

    fold = 4 * SUBLANES

    def count(pred):
        def body(c, cnt):
            m = pred(key_scr[c], c)
            return cnt + jnp.sum(m.reshape(c_sz // fold, fold, tq), axis=0)
        cnt = lax.fori_loop(0, nch, body, jnp.zeros((fold, tq), F32))
        return jnp.sum(cnt, axis=0, keepdims=True)

    kf = float(k_top)
    short = qpos + 1 < k_top
    c0 = count(lambda k, c: jnp.where(k >= 0, 1.0, 0.0))
    cand0 = jnp.where(c0 >= kf, 0, INT_MIN).astype(I32)
    cnt0 = jnp.where(c0 >= kf, c0, 2.0 ** 30)

    def refine(shift, cand, cnt_c):
        trial = cand | jnp.left_shift(jnp.int32(1), shift)
        cnt = count(lambda k, c: jnp.where(k >= trial, 1.0, 0.0))
        return jnp.where(cnt >= kf, trial, cand), jnp.where(cnt >= kf, cnt, cnt_c)

    cand0, cnt0 = refine(30, cand0, cnt0)
    probe = cand0 + 1
    tied = jnp.where(count(lambda k, c: jnp.where(k >= probe, 1.0, 0.0)) < kf, 1.0, 0.0)
    settled = jnp.where(short, 1.0, tied)

    def unresolved(cnt_c):
        return (jnp.max(jnp.where(settled > 0.0, 0.0, jnp.where(cnt_c == kf, 0.0, 1.0))) > 0.0).astype(I32)

    def bits_cond(st):
        return jnp.logical_and(st[0] < 15, st[3] > 0)

    def bits_body(st):
        j, cand, cnt_c, _ = st
        cand, cnt_c = refine(29 - 2 * j, cand, cnt_c)
        cand, cnt_c = refine(28 - 2 * j, cand, cnt_c)
        return j + 1, cand, cnt_c, unresolved(cnt_c)

    _, thr, cnt_ge, _ = lax.while_loop(bits_cond, bits_body, (jnp.int32(0), cand0, cnt0, unresolved(cnt0)))
    need = jnp.where(short, 0.0, jnp.where(cnt_ge > kf, 1.0, 0.0))

    @pl.when(jnp.max(need) > 0.0)
    def _():
        rem = kf - count(lambda k, c: jnp.where(k > thr, 1.0, 0.0))
        rem = jnp.where(need > 0.0, rem, 2.0 ** 30)

        def demote(c, base):
            k = key_scr[c]
            tie = jnp.where(k == thr, 1.0, 0.0)
            rank = base + jnp.dot(tril_ref[...], tie.astype(BF16), preferred_element_type=F32)
            key_scr[c] = jnp.where(k == thr, jnp.where(rank > rem, INT_MIN, k), k)
            return rank[c_sz - 1:c_sz, :]

        lax.fori_loop(0, nch, demote, jnp.zeros((1, tq), F32))

    thr_sel = jnp.where(short, INT_MIN + 1, thr)

    def sel_bias(c):
        return jnp.where(key_scr[c] >= thr_sel, 0.0, NEG).T

    _flash_loop(nch, qa_scr, dk_ref, dv_ref, sel_bias, DSA_HEADS, tq, s_scr, p_scr, alpha_scr, m_scr, l_scr, acc_scr)
    o_ref[...] = _dsa_finish(acc_scr, l_scr, tq)


def _dsa_prompt(p, bp, seq, tq):
    c_sz = p["dkT"].shape[2]
    nq = seq // tq
    k_top = min(IDX_TOPK_MAX, seq // 4)
    qrow = lambda b, i: (b * nq + i, 0)
    kv = lambda b, i: (b, 0)
    kvt = lambda b, i: (b, 0, 0)
    rows = DSA_HEADS * tq
    return pl.pallas_call(
        functools.partial(_dsa_prompt_kernel, k_top=k_top),
        grid=(bp, nq),
        in_specs=[pl.BlockSpec((tq, 256), qrow), pl.BlockSpec((tq, 128), qrow), pl.BlockSpec((tq, 256), qrow),
                  pl.BlockSpec((seq, LANES), kv), pl.BlockSpec((seq // c_sz, 256, c_sz), kvt),
                  pl.BlockSpec((seq, 256), kv), pl.BlockSpec((c_sz, c_sz), lambda b, i: (0, 0))],
        out_specs=pl.BlockSpec((tq, 256), qrow),
        out_shape=jax.ShapeDtypeStruct((bp * seq, BRANCH_W), F32),
        scratch_shapes=[pltpu.VMEM((seq // c_sz, c_sz, tq), I32), pltpu.VMEM((c_sz, IDX_HEADS * tq), F32),
                        pltpu.VMEM((LANES, IDX_HEADS * tq), BF16),
                        pltpu.VMEM((rows, 256), BF16), pltpu.VMEM((rows, LANES), F32),
                        pltpu.VMEM((rows, LANES), F32), pltpu.VMEM((2, rows, LANES), F32),
                        pltpu.VMEM((rows, 256), F32), pltpu.VMEM((2, rows, c_sz), BF16),
                        pltpu.VMEM((2, rows, c_sz), F32)],
        compiler_params=_cparams(2),
        name="dsa_prompt",
    )(p["iq"], p["misc"], p["dq"], p["miscb"], p["dkT"], p["dvb"], jnp.tril(jnp.ones((c_sz, c_sz), BF16)))


def _diff_prompt_kernel(fq_ref, fk_ref, fv_ref, lam_ref, gsub_ref, o_ref,
                        qa_scr, m_scr, l_scr, alpha_scr, acc_scr, p_scr, s_scr, *, lam_init):
    tq = fq_ref.shape[0]
    c_sz = fk_ref.shape[2]
    nh = 2 * DIFF_HEADS
    i = pl.program_id(1)
    n_full = (i * tq + 1) // c_sz
    qpos = i * tq + lax.broadcasted_iota(I32, (tq, 1), 0)
    lane_c = lax.broadcasted_iota(I32, (tq, c_sz), 1)
    fq = fq_ref[...]
    grp = _group_of_lane((tq, BRANCH_W), 5)
    for h in range(nh):
        qa_scr[h * tq:(h + 1) * tq, :] = jnp.where(grp == h, fq, jnp.zeros_like(fq))

    def causal_bias(c):
        return jnp.where(c * c_sz + lane_c <= qpos, 0.0, NEG)

    _flash_loop(n_full + 1, qa_scr, fk_ref, fv_ref, causal_bias, nh, tq,
                s_scr, p_scr, alpha_scr, m_scr, l_scr, acc_scr)
    lam = _diff_lambda(lam_ref, lam_init)
    o_ref[...] = _diff_finish(acc_scr, l_scr, tq, lam, lam_init, gsub_ref[...])


def _diff_prompt(p, lam_p, gsub, lam_init, bp, seq, tq):
    c_sz = p["fkT"].shape[2]
    nq = seq // tq
    qrow = lambda b, i: (b * nq + i, 0)
    kv = lambda b, i: (b, 0)
    const = lambda b, i: (0, 0)
    rows = 2 * DIFF_HEADS * tq
    return pl.pallas_call(
        functools.partial(_diff_prompt_kernel, lam_init=lam_init),
        grid=(bp, nq),
        in_specs=[pl.BlockSpec((tq, 256), qrow),
                  pl.BlockSpec((seq // c_sz, 256, c_sz), lambda b, i: (b, 0, 0)), pl.BlockSpec((seq, 256), kv),
                  pl.BlockSpec((4, DIFF_HEAD_DIM), const), pl.BlockSpec((1, 256), const)],
        out_specs=pl.BlockSpec((tq, 256), qrow),
        out_shape=jax.ShapeDtypeStruct((bp * seq, BRANCH_W), F32),
        scratch_shapes=[pltpu.VMEM((rows, 256), BF16), pltpu.VMEM((rows, LANES), F32),
                        pltpu.VMEM((rows, LANES), F32), pltpu.VMEM((2, rows, LANES), F32),
                        pltpu.VMEM((rows, 256), F32), pltpu.VMEM((2, rows, c_sz), BF16),
                        pltpu.VMEM((2, rows, c_sz), F32)],
        compiler_params=_cparams(2),
        name="diff_prompt",
    )(p["fq"], p["fkT"], p["fvb"], lam_p, gsub)


def _ssd_kernel(par_ref, xbc_ref, dts_ref, z_ref, cprev_ref, hprev_ref, cw_ref, cb_ref, ng_ref,
                y_ref, hout_ref, cout_ref, e_scr, h_scr):
    qin = xbc_ref.shape[0]
    q = SSD_CHUNK
    c = pl.program_id(1)
    last = pl.num_programs(1) - 1

    @pl.when(c == 0)
    def _():
        e_scr[0:8, :] = cprev_ref[...]
        h_scr[...] = hprev_ref[...]
        if qin < q:
            e_scr[8 + qin:8 + q, :] = jnp.zeros((q - qin, SSD_CONV_DIM), F32)

    e_scr[8:8 + qin, :] = xbc_ref[...]
    conv = cb_ref[...] + jnp.zeros((q, SSD_CONV_DIM), F32)
    for k in range(SSD_CONV):
        conv = conv + e_scr[5 + k:5 + k + q, :] * cw_ref[k:k + 1, :]
    new_tail = e_scr[qin:qin + 8, :]
    u = _silu(conv)
    xs = u[:, :BRANCH_W]
    b_all = u[:, BRANCH_W:BRANCH_W + SSD_GROUPS * SSD_STATE]
    bm_t = b_all.T.astype(BF16)
    bm = b_all.astype(BF16)
    cm = u[:, BRANCH_W + SSD_GROUPS * SSD_STATE:].astype(BF16)

    dts = dts_ref[...]
    if qin < q:
        dts = jnp.concatenate([dts, jnp.zeros((q - qin, LANES), F32)], axis=0)
    rowi = lax.broadcasted_iota(I32, (q, LANES), 0)
    lanei = lax.broadcasted_iota(I32, (q, LANES), 1)
    bias_l = jnp.zeros((q, LANES), F32)
    alog_l = jnp.zeros((q, LANES), F32)
    for h in range(SSD_HEADS):
        bias_l = jnp.where(lanei == h, par_ref[h], bias_l)
        alog_l = jnp.where(lanei == h, par_ref[SSD_HEADS + h], alog_l)
    pre = dts + bias_l
    dt = jnp.maximum(pre, 0.0) + jnp.log1p(jnp.exp(-jnp.abs(pre)))
    dt = jnp.where(rowi < qin, jnp.where(lanei < SSD_HEADS, dt, 0.0), 0.0)
    adt = dt * (-jnp.exp(alog_l))
    r_qq = lax.broadcasted_iota(I32, (q, q), 0)
    c_qq = lax.broadcasted_iota(I32, (q, q), 1)
    causal = r_qq >= c_qq
    acs_col = jnp.dot(jnp.where(causal, 1.0, 0.0), adt, preferred_element_type=F32, precision=HIGHEST)
    acs_row = jnp.dot(adt.T[0:8, :], jnp.where(r_qq <= c_qq, 1.0, 0.0), preferred_element_type=F32,
                      precision=HIGHEST)

    grp = _group_of_lane((q, BRANCH_W), 6)
    grp1 = _group_of_lane((1, BRANCH_W), 6)
    dt_b = jnp.zeros((q, BRANCH_W), F32)
    a_b = jnp.zeros((q, BRANCH_W), F32)
    d_b = jnp.zeros((1, BRANCH_W), F32)
    for h in range(SSD_HEADS):
        dt_b = jnp.where(grp == h, dt[:, h:h + 1], dt_b)
        a_b = jnp.where(grp == h, acs_col[:, h:h + 1], a_b)
        d_b = jnp.where(grp1 == h, par_ref[2 * SSD_HEADS + h], d_b)
    xdt = xs * dt_b
    xdt_b = xdt.astype(BF16)
    a_last = a_b[q - 1:q, :]

    y_diag = jnp.zeros((q, BRANCH_W), F32)
    hpg = SSD_HEADS // SSD_GROUPS
    cbs = [_nk_dot(cm[:, 64 * g:64 * (g + 1)], bm[:, 64 * g:64 * (g + 1)]) for g in range(SSD_GROUPS)]
    for h in range(SSD_HEADS):
        lmat = jnp.exp(jnp.where(causal, acs_col[:, h:h + 1] - acs_row[h:h + 1, :], -jnp.inf))
        y_diag = y_diag + jnp.dot((cbs[h // hpg] * lmat).astype(BF16),
                                  jnp.where(grp == h, xdt_b, jnp.zeros_like(xdt_b)), preferred_element_type=F32)
    ht = h_scr[...]
    htb = ht.astype(BF16)
    xdec = (xdt * jnp.exp(a_last - a_b)).astype(BF16)
    st = [jnp.dot(bm_t[64 * g:64 * (g + 1), :], xdec, preferred_element_type=F32) for g in range(SSD_GROUPS)]
    yo = [jnp.dot(cm[:, 64 * g:64 * (g + 1)], htb, preferred_element_type=F32) for g in range(SSD_GROUPS)]
    grp_n = _group_of_lane((SSD_STATE, BRANCH_W), 6)
    h_scr[...] = ht * jnp.exp(a_last) + jnp.where(grp_n < hpg, st[0], st[1])
    y = y_diag + jnp.where(grp < hpg, yo[0], yo[1]) * jnp.exp(a_b) + xs * d_b
    gte = y * _silu(z_ref[...]) if qin == q else y[0:qin, :] * _silu(z_ref[...])
    half = BRANCH_W // SSD_GROUPS
    outs = []
    for g in range(SSD_GROUPS):
        gg = gte[:, half * g:half * (g + 1)]
        outs.append(gg * lax.rsqrt(jnp.mean(gg * gg, axis=-1, keepdims=True) + EPS))
    y_ref[...] = jnp.concatenate(outs, axis=1) * ng_ref[...]
    e_scr[0:8, :] = new_tail

    @pl.when(c == last)
    def _():
        hout_ref[...] = h_scr[...]
        cout_ref[...] = new_tail


def _ssd(p, par, cprev8, hprev, cw, cb, ng, nb, rows_per_b, qin):
    nc = rows_per_b // qin
    row = lambda b, c: (b * nc + c, 0)
    const = lambda b, c: (0, 0)
    y, hout, cout = pl.pallas_call(
        _ssd_kernel,
        grid=(nb, nc),
        in_specs=[pl.BlockSpec(memory_space=pltpu.SMEM),
                  pl.BlockSpec((qin, SSD_CONV_DIM), row), pl.BlockSpec((qin, LANES), row),
                  pl.BlockSpec((qin, 256), lambda b, c: (b * nc + c, 2)),
                  pl.BlockSpec((None, 8, SSD_CONV_DIM), lambda b, c: (b, 0, 0)),
                  pl.BlockSpec((None, SSD_STATE, BRANCH_W), lambda b, c: (b, 0, 0)),
                  pl.BlockSpec((SSD_CONV, SSD_CONV_DIM), const), pl.BlockSpec((1, SSD_CONV_DIM), const),
                  pl.BlockSpec((1, BRANCH_W), const)],
        out_specs=[pl.BlockSpec((qin, BRANCH_W), row),
                   pl.BlockSpec((None, SSD_STATE, BRANCH_W), lambda b, c: (b, 0, 0)),
                   pl.BlockSpec((None, 8, SSD_CONV_DIM), lambda b, c: (b, 0, 0))],
        out_shape=[jax.ShapeDtypeStruct((nb * rows_per_b, BRANCH_W), F32),
                   jax.ShapeDtypeStruct((nb, SSD_STATE, BRANCH_W), F32),
                   jax.ShapeDtypeStruct((nb, 8, SSD_CONV_DIM), F32)],
        scratch_shapes=[pltpu.VMEM((8 + SSD_CHUNK, SSD_CONV_DIM), F32),
                        pltpu.VMEM((SSD_STATE, BRANCH_W), F32)],
        compiler_params=_cparams(2),
        name="ssd",
    )(par, p["sxbc"], p["dts"], p["gates"], cprev8, _state_to_lanes(hprev), cw, cb, ng)
    return y, _state_from_lanes(hout), cout[:, 8 - (SSD_CONV - 1):, :]


def _state_to_lanes(h):
    b = h.shape[0]
    return h.reshape(b, SSD_HEADS * SSD_HEAD_DIM, SSD_STATE).transpose(0, 2, 1)


def _state_from_lanes(ht):
    b = ht.shape[0]
    return ht.transpose(0, 2, 1).reshape(b, SSD_HEADS, SSD_HEAD_DIM, SSD_STATE)


def _pool_kernel(u_ref, halo_ref, w_ref, scale_ref, o_ref, hout_ref, e_scr, *, start_pos):
    t = u_ref.shape[0]
    c = pl.program_id(1)
    last = pl.num_programs(1) - 1

    @pl.when(c == 0)
    def _():
        e_scr[0:8, :] = jnp.zeros((8, BRANCH_W), F32)
        e_scr[8:24, :] = halo_ref[...]

    u = u_ref[...]
    e_scr[24:24 + t, :] = u
    new_halo = e_scr[8 + t:24 + t, :]
    n = 16 + t
    cur = e_scr[8:8 + n, :]
    stages = []
    for k in (1, 2, 4, 8):
        cur = cur + e_scr[8 - k:8 - k + n, :]
        stages.append(cur[16:, :])
        e_scr[8:8 + n, :] = cur
    grp = _group_of_lane((t, BRANCH_W), 6)
    win = jnp.where(grp == 0, stages[0], jnp.where(grp == 1, stages[1], jnp.where(grp == 2, stages[2], stages[3])))
    wlen = jnp.where(grp == 0, 2.0, jnp.where(grp == 1, 4.0, jnp.where(grp == 2, 8.0, 16.0)))
    n_avail = (start_pos + c * t + 1 + lax.broadcasted_iota(I32, (t, BRANCH_W), 0)).astype(F32)
    d = win / jnp.minimum(wlen, n_avail) - u
    o_ref[...] = jnp.dot(d.astype(BF16), w_ref[...], preferred_element_type=F32) * scale_ref[...]
    e_scr[8:24, :] = new_halo

    @pl.when(c == last)
    def _():
        hout_ref[...] = new_halo


def _pool(u, halo16, wbd, scale, nb, rows_per_b, t, start_pos):
    nc = rows_per_b // t
    row = lambda b, c: (b * nc + c, 0)
    const = lambda b, c: (0, 0)
    o, hout = pl.pallas_call(
        functools.partial(_pool_kernel, start_pos=start_pos),
        grid=(nb, nc),
        in_specs=[pl.BlockSpec((t, BRANCH_W), row), pl.BlockSpec((None, 16, BRANCH_W), lambda b, c: (b, 0, 0)),
                  pl.BlockSpec((BRANCH_W, BRANCH_W), const), pl.BlockSpec((1, BRANCH_W), const)],
        out_specs=[pl.BlockSpec((t, BRANCH_W), row), pl.BlockSpec((None, 16, BRANCH_W), lambda b, c: (b, 0, 0))],
        out_shape=[jax.ShapeDtypeStruct((nb * rows_per_b, BRANCH_W), F32),
                   jax.ShapeDtypeStruct((nb, 16, BRANCH_W), F32)],
        scratch_shapes=[pltpu.VMEM((24 + t, BRANCH_W), F32)],
        compiler_params=_cparams(2),
        name="pool",
    )(u, halo16, wbd, scale)
    return o, hout[:, 1:, :]


def _merge_kernel(x_ref, ng_ref, wmg_ref, dsa_ref, diff_ref, ssd_ref, pool_ref, gates_ref, wbr_ref, wout_ref, y_ref):
    x = x_ref[...]
    h = x * lax.rsqrt(jnp.mean(x * x, axis=-1, keepdims=True) + EPS) * ng_ref[...]
    hb = h.astype(BF16)
    d = x.shape[1]
    br = (dsa_ref[...] * _silu(gates_ref[:, 0:256]),
          diff_ref[...] * _silu(gates_ref[:, 256:512]),
          ssd_ref[...],
          pool_ref[...] * _silu(gates_ref[:, 768:1024]))
    m = jnp.zeros(x.shape, F32)
    for n in range(N_BRANCH):
        mg = jnp.dot(hb, wmg_ref[:, n * d:(n + 1) * d], preferred_element_type=F32)
        up = jnp.dot(br[n].astype(BF16), wbr_ref[n], preferred_element_type=F32)
        m = m + _sigmoid(mg) * up
    y_ref[...] = x + jnp.dot(m.astype(BF16), wout_ref[...], preferred_element_type=F32)


def _merge(x, ng, wmg, dsa_o, diff_o, ssd_o, pool_o, gates, wbr, wout, tm):
    n, d = x.shape
    row = lambda i: (i, 0)
    const = lambda i: (0, 0)
    return pl.pallas_call(
        _merge_kernel,
        grid=(n // tm,),
        in_specs=[pl.BlockSpec((tm, d), row), pl.BlockSpec((1, d), const), pl.BlockSpec(wmg.shape, const),
                  pl.BlockSpec((tm, 256), row), pl.BlockSpec((tm, 256), row), pl.BlockSpec((tm, 256), row),
                  pl.BlockSpec((tm, 256), row), pl.BlockSpec((tm, 1024), row),
                  pl.BlockSpec(wbr.shape, lambda i: (0, 0, 0)), pl.BlockSpec(wout.shape, const)],
        out_specs=pl.BlockSpec((tm, d), row),
        out_shape=jax.ShapeDtypeStruct((n, d), F32),
        compiler_params=_cparams(1),
        name="merge",
    )(x, ng, wmg, dsa_o, diff_o, ssd_o, pool_o, gates, wbr, wout)


_PAGES_PER_STEP = 32
_KEY_CHUNK = 512
_Q_BLOCK = 128
_MERGE_ROWS = 256


def _page_specs(layer, features, npp):
    def spec(k):
        return pl.BlockSpec((None, None, features, PAGE_SIZE),
                            lambda b, g, pt, k=k: (layer, pt[b, g * npp + k], 0, 0))
    return [spec(k) for k in range(npp)]


def _pages_view(cache):
    depth, n_pool, page = cache.shape[:3]
    c = cache.reshape(depth, n_pool, page, -1)
    return jnp.swapaxes(c, 2, 3)


def _dsa_sidx_kernel(pt_ref, iq_ref, misc_ref, *rest, npp, n_pages, k_top):
    pages = rest[:npp]
    bias_ref = rest[npp]
    key_scr, qi_scr, jst_scr = rest[npp + 1:]
    t = iq_ref.shape[0]
    g = pl.program_id(1)
    last = pl.num_programs(1) - 1
    wts = misc_ref[:, 64:64 + IDX_HEADS]

    def scores(kt):
        s = jnp.dot(qi_scr[...], kt, preferred_element_type=F32)
        acc = None
        for h in range(IDX_HEADS):
            v = wts[:, h:h + 1] * jnp.maximum(s[h * t:(h + 1) * t, :], 0.0)
            acc = v if acc is None else acc + v
        return jnp.where(acc == 0.0, 0.0, acc)

    @pl.when(g == 0)
    def _():
        iq = iq_ref[...]
        for h in range(IDX_HEADS):
            qi_scr[h * t:(h + 1) * t, :] = iq[:, IDX_DIM * h:IDX_DIM * (h + 1)]
        knew = jnp.concatenate([misc_ref[...], jnp.zeros((PAGE_SIZE - t, LANES), F32)], axis=0)
        sc = scores(knew.T[0:IDX_DIM, :].astype(BF16))
        causal = lax.broadcasted_iota(I32, (t, LANES), 1) <= lax.broadcasted_iota(I32, (t, LANES), 0)
        key_scr[n_pages] = jnp.where(causal, _sortable(sc), INT_MIN)

    sc = scores(jnp.concatenate([pg[...].astype(BF16) for pg in pages], axis=1))
    for k in range(npp):
        key_scr[g * npp + k] = _sortable(sc[:, k * PAGE_SIZE:(k + 1) * PAGE_SIZE])

    @pl.when(g == last)
    def _():
        keys = key_scr[...]
        kpos = (lax.broadcasted_iota(I32, keys.shape, 0) * PAGE_SIZE + lax.broadcasted_iota(I32, keys.shape, 2))

        def count(m):
            return jnp.sum(jnp.sum(m, axis=0), axis=1, keepdims=True)

        kf = float(k_top)
        c0 = count(jnp.where(keys >= 0, 1.0, 0.0))
        cand0 = jnp.where(c0 >= kf, 0, INT_MIN).astype(I32)

        def bit_body(b, cand):
            trial = cand | jnp.left_shift(jnp.int32(1), 30 - b)
            cnt = count(jnp.where(keys >= trial[None], 1.0, 0.0))
            return jnp.where(cnt >= kf, trial, cand)

        thr = lax.fori_loop(0, 31, bit_body, cand0)
        cnt_gt = count(jnp.where(keys > thr[None], 1.0, 0.0))
        cnt_ge = count(jnp.where(keys >= thr[None], 1.0, 0.0))
        need = jnp.where(cnt_ge > kf, jnp.where(thr > INT_MIN, 1.0, 0.0), 0.0)
        rem = kf - cnt_gt
        big = jnp.int32(2 ** 30)
        jst_scr[...] = jnp.full((t, 1), big, I32)

        @pl.when(jnp.max(need) > 0.0)
        def _():
            nbits = ((n_pages + 1) * PAGE_SIZE - 1).bit_length()

            def jb(b, pos):
                trial = pos | jnp.left_shift(jnp.int32(1), nbits - 1 - b)
                cnt = count(jnp.where(keys == thr[None], jnp.where(kpos < trial[None], 1.0, 0.0), 0.0))
                return jnp.where(cnt < rem, trial, pos)

            pos = lax.fori_loop(0, nbits, jb, jnp.zeros((t, 1), I32))
            jst_scr[...] = jnp.where(need > 0.0, pos, big)

        jst = jst_scr[...]
        sel = jnp.where(keys > thr[None], 0.0,
                        jnp.where(keys == thr[None], jnp.where(kpos <= jst[None], 0.0, NEG), NEG))
        newc = lax.broadcasted_iota(I32, keys.shape, 0) == n_pages
        causal = lax.broadcasted_iota(I32, keys.shape, 2) <= lax.broadcasted_iota(I32, keys.shape, 1)
        bias_ref[...] = jnp.where(newc, jnp.where(causal, sel, NEG), sel)


def _dsa_sidx(ps, cache_ik, layer, page_table, bs, t):
    n_pages = page_table.shape[1]
    npp = math.gcd(n_pages, _PAGES_PER_STEP)
    k_top = min(IDX_TOPK_MAX, (n_pages * PAGE_SIZE + t) // 4)
    row = lambda b, g, pt: (b, 0)
    grid_spec = pltpu.PrefetchScalarGridSpec(
        num_scalar_prefetch=1, grid=(bs, n_pages // npp),
        in_specs=[pl.BlockSpec((t, 256), row), pl.BlockSpec((t, 128), row)] + _page_specs(layer, IDX_DIM, npp),
        out_specs=pl.BlockSpec((None, n_pages + 1, t, LANES), lambda b, g, pt: (b, 0, 0, 0)),
        scratch_shapes=[pltpu.VMEM((n_pages + 1, t, LANES), I32), pltpu.VMEM((IDX_HEADS * t, IDX_DIM), BF16),
                        pltpu.VMEM((t, 1), I32)])
    return pl.pallas_call(
        functools.partial(_dsa_sidx_kernel, npp=npp, n_pages=n_pages, k_top=k_top),
        grid_spec=grid_spec,
        out_shape=jax.ShapeDtypeStruct((bs, n_pages + 1, t, LANES), F32),
        compiler_params=_cparams(2),
        name="dsa_sample_index",
    )(page_table, ps["iq"], ps["misc"], *([cache_ik] * npp))


def _paged_attn_kernel(pt_ref, q_ref, kn_ref, vn_ref, *rest, npp, nheads, log2_hd, use_bias, finish):
    idx = 0
    if use_bias:
        bias_new_ref, bias_pg_ref = rest[0], rest[1]
        idx = 2
    kpages = rest[idx:idx + npp]
    vpages = rest[idx + npp:idx + 2 * npp]
    n_scr = 7
    extra = rest[idx + 2 * npp:-(n_scr + 1)]
    o_ref = rest[-(n_scr + 1)]
    qa_scr, m_scr, l_scr, alpha_scr, acc_scr, p_scr, s_scr = rest[-n_scr:]
    t = q_ref.shape[0]
    rows = nheads * t
    g = pl.program_id(1)
    last = pl.num_programs(1) - 1

    def tile_rows(b):
        return jnp.concatenate([b] * nheads, axis=0)

    def accumulate(pv):
        alpha = alpha_scr[...]
        acc_scr[...] = jnp.concatenate([alpha, alpha], axis=1) * acc_scr[...] + pv

    @pl.when(g == 0)
    def _():
        qv = q_ref[...]
        grp = _group_of_lane((t, BRANCH_W), log2_hd)
        for h in range(nheads):
            qa_scr[h * t:(h + 1) * t, :] = jnp.where(grp == h, qv, jnp.zeros_like(qv))
        m_scr[...] = jnp.full((rows, LANES), NEG, F32)
        l_scr[...] = jnp.zeros((rows, LANES), F32)
        acc_scr[...] = jnp.zeros((rows, BRANCH_W), F32)
        kn = jnp.concatenate([kn_ref[...], jnp.zeros((PAGE_SIZE - t, BRANCH_W), F32)], axis=0)
        vn = jnp.concatenate([vn_ref[...], jnp.zeros((PAGE_SIZE - t, BRANCH_W), BF16)], axis=0)
        if use_bias:
            bias = bias_new_ref[0]
        else:
            causal = lax.broadcasted_iota(I32, (t, LANES), 1) <= lax.broadcasted_iota(I32, (t, LANES), 0)
            bias = jnp.where(causal, 0.0, NEG)
        s_new = s_scr.at[:, 0:PAGE_SIZE]
        p_new = p_scr.at[:, 0:PAGE_SIZE]
        s_new[...] = jnp.dot(qa_scr[...], kn.T.astype(BF16), preferred_element_type=F32)
        _flash_rows(s_new, tile_rows(bias), m_scr, l_scr, alpha_scr, p_new, 0, rows)
        accumulate(jnp.dot(p_new[...], vn, preferred_element_type=F32))

    kt = jnp.concatenate([kp[...].astype(BF16) for kp in kpages], axis=1)
    vt = jnp.concatenate([vp[...].astype(BF16) for vp in vpages], axis=1)
    s_scr[...] = jnp.dot(qa_scr[...], kt, preferred_element_type=F32)
    bias = None
    if use_bias:
        bias = tile_rows(jnp.concatenate([bias_pg_ref[k] for k in range(npp)], axis=1))
    _flash_rows(s_scr, bias, m_scr, l_scr, alpha_scr, p_scr, 0, rows)
    accumulate(_nk_dot(p_scr[...], vt))

    @pl.when(g == last)
    def _():
        o_ref[...] = finish(acc_scr, l_scr, t, *extra)


def _dsa_finish(acc_ref, l_ref, t):
    grp = _group_of_lane((t, BRANCH_W), 6)
    out = jnp.zeros((t, BRANCH_W), F32)
    for h in range(DSA_HEADS):
        l_row = jnp.sum(l_ref[h * t:(h + 1) * t, :], axis=1, keepdims=True)
        out = out + jnp.where(grp == h, acc_ref[h * t:(h + 1) * t, :] / l_row, 0.0)
    return out


def _diff_sample_finish(acc_ref, l_ref, t, lam_ref, gsub_ref, *, lam_init):
    return _diff_finish(acc_ref, l_ref, t, _diff_lambda(lam_ref, lam_init), lam_init, gsub_ref[...])


def _paged_attn(q, kn, vn, cache_k, cache_v, layer, page_table, bs, t, *, nheads, log2_hd, bias=None,
                extra=(), finish, name):
    n_pages = page_table.shape[1]
    npp = math.gcd(n_pages, _PAGES_PER_STEP)
    row = lambda b, g, pt: (b, 0)
    const = lambda b, g, pt: (0, 0)
    in_specs = [pl.BlockSpec((t, 256), row), pl.BlockSpec((t, 256), row), pl.BlockSpec((t, 256), row)]
    args = [q, kn, vn]
    if bias is not None:
        in_specs += [pl.BlockSpec((None, 1, t, LANES), lambda b, g, pt: (b, n_pages, 0, 0)),
                     pl.BlockSpec((None, npp, t, LANES), lambda b, g, pt: (b, g, 0, 0))]
        args += [bias, bias]
    in_specs += _page_specs(layer, 256, npp) + _page_specs(layer, 256, npp)
    args += [cache_k] * npp + [cache_v] * npp
    for e in extra:
        in_specs.append(pl.BlockSpec(e.shape, const))
        args.append(e)
    rows = nheads * t
    grid_spec = pltpu.PrefetchScalarGridSpec(
        num_scalar_prefetch=1, grid=(bs, n_pages // npp), in_specs=in_specs,
        out_specs=pl.BlockSpec((t, 256), row),
        scratch_shapes=[pltpu.VMEM((rows, 256), BF16), pltpu.VMEM((rows, LANES), F32),
                        pltpu.VMEM((rows, LANES), F32), pltpu.VMEM((rows, LANES), F32),
                        pltpu.VMEM((rows, 256), F32), pltpu.VMEM((rows, npp * PAGE_SIZE), BF16),
                        pltpu.VMEM((rows, npp * PAGE_SIZE), F32)])
    return pl.pallas_call(
        functools.partial(_paged_attn_kernel, npp=npp, nheads=nheads, log2_hd=log2_hd,
                          use_bias=bias is not None, finish=finish),
        grid_spec=grid_spec,
        out_shape=jax.ShapeDtypeStruct((bs * t, BRANCH_W), F32),
        compiler_params=_cparams(2),
        name=name,
    )(page_table, *args)


def _rope_table(pos, half, reps, extra_cos=None):
    inv = ROPE_THETA ** (-jnp.arange(half, dtype=F32) / half)
    ang = pos.astype(F32)[:, None] * inv[None, :]
    cos, sin = jnp.cos(ang), jnp.sin(ang)
    cos_h = jnp.concatenate([cos, cos], axis=1)
    sin_h = jnp.concatenate([-sin, sin], axis=1)
    cos_t, sin_t = jnp.tile(cos_h, (1, reps)), jnp.tile(sin_h, (1, reps))
    if extra_cos is not None:
        n = pos.shape[0]
        cos_t = jnp.concatenate([cos_t, jnp.broadcast_to(extra_cos[None, :], (n, extra_cos.shape[0]))], axis=1)
        sin_t = jnp.concatenate([sin_t, jnp.zeros((n, extra_cos.shape[0]), F32)], axis=1)
    return jnp.concatenate([cos_t, sin_t], axis=1)


def _tables(pos):
    misc_scale = jnp.concatenate([jnp.full((IDX_HEADS,), IDX_HEADS ** -0.5, F32),
                                  jnp.ones((LANES - IDX_DIM - IDX_HEADS,), F32)])
    return (_rope_table(pos, DSA_HEAD_DIM // 2, 4), _rope_table(pos, DIFF_HEAD_DIM // 2, 8),
            _rope_table(pos, IDX_DIM // 2, 1, misc_scale))


def _pack_w_in(w):
    sizes = (256, 256, 256, 256, 256, 64, 4, 256, 256, 256, 256, 256, 512, 4, 256, 256, 4096)
    offs = [0]
    for s in sizes:
        offs.append(offs[-1] + s)
    (dq, dk, dv, dg, iq, ik, iw, fq, fk, fv, fg, sz, sxbc, sdt, pu, pg, mg) = [
        w[:, offs[i]:offs[i + 1]] for i in range(len(sizes))]
    d = w.shape[0]
    misc = jnp.concatenate([ik, iw, jnp.zeros((d, LANES - IDX_DIM - IDX_HEADS), w.dtype)], axis=1)
    dts = jnp.concatenate([sdt, jnp.zeros((d, LANES - SSD_HEADS), w.dtype)], axis=1)
    packed = jnp.concatenate([dq, dk, dv, iq, fq, fk, fv, dg, fg, sz, pg, sxbc, pu, misc, dts], axis=1)
    return packed.astype(BF16), mg.astype(BF16)


def _block_diag(w):
    g, n, _ = w.shape
    out = jnp.zeros((g * n, g * n), w.dtype)
    for i in range(g):
        out = out.at[i * n:(i + 1) * n, i * n:(i + 1) * n].set(w[i])
    return out


def kernel(x_prompt, x_sample, cache_dsa_k, cache_dsa_v, cache_idx_k, cache_diff_k, cache_diff_v, state_ssm,
           state_conv, state_pool, page_table, norm_g, w_in, dsa_qk_g, diff_qk_g, diff_lam, diff_subln,
           ssd_conv_w, ssd_conv_b, ssd_dt_bias, ssd_a_log, ssd_d, ssd_norm, pool_w, pool_scale, w_branch, w_out):
    bp, seq, d = x_prompt.shape
    bs, t, _ = x_sample.shape
    depth = norm_g.shape[0]
    n_pages = page_table.shape[1]
    past = n_pages * PAGE_SIZE
    n_pool = cache_dsa_k.shape[1]

    tab_p = _tables(jnp.arange(seq, dtype=I32))
    tab_s = tuple(jnp.tile(a, (bs, 1)) for a in _tables(past + jnp.arange(t, dtype=I32)))
    ck, cv, cik = _pages_view(cache_dsa_k), _pages_view(cache_dsa_v), _pages_view(cache_idx_k)
    cfk, cfv = _pages_view(cache_diff_k), _pages_view(cache_diff_v)

    tm = min(_KEY_CHUNK, seq)
    tq = min(_Q_BLOCK, seq)
    tm_s = min(256, bs * t)
    pool_t = min(512, seq)
    xp = x_prompt.reshape(bp * seq, d)
    xs = x_sample.reshape(bs * t, d)
    acc_p = {n: [] for n in ("dk", "dv", "ik", "fk", "fv", "ssm", "conv", "pool")}
    acc_s = {n: [] for n in acc_p}

    for l in range(depth):
        lam_init = 0.8 - 0.6 * math.exp(-0.3 * l)
        wp, wmg = _pack_w_in(w_in[l])
        ng = norm_g[l].reshape(1, d)
        gains = jnp.stack([jnp.tile(dsa_qk_g[l, 0], 4), jnp.tile(dsa_qk_g[l, 1], 4),
                           jnp.tile(diff_qk_g[l, 0], 8), jnp.tile(diff_qk_g[l, 1], 8)])
        gsub = jnp.tile(diff_subln[l], 4).reshape(1, BRANCH_W)
        lam_p = diff_lam[l]
        ssd_par = jnp.concatenate([ssd_dt_bias[l], ssd_a_log[l], ssd_d[l]]).astype(F32)
        cw, cb = ssd_conv_w[l], ssd_conv_b[l].reshape(1, SSD_CONV_DIM)
        sng = ssd_norm[l].reshape(1, BRANCH_W)
        wbd = _block_diag(pool_w[l]).astype(BF16)
        pscale = pool_scale[l].reshape(1, BRANCH_W)
        wbr = w_branch[l].astype(BF16)
        wout = w_out[l].astype(BF16)

        pp = _proj(xp, ng, wp, *tab_p, gains, tm, seq // tm)
        dsa_o = _dsa_prompt(pp, bp, seq, tq)
        diff_o = _diff_prompt(pp, lam_p, gsub, lam_init, bp, seq, tq)
        ssd_o, ssm_new, conv_new = _ssd(pp, ssd_par, jnp.zeros((bp, 8, SSD_CONV_DIM), F32),
                                        jnp.zeros((bp, SSD_HEADS, SSD_HEAD_DIM, SSD_STATE), F32), cw, cb, sng,
                                        bp, seq, SSD_CHUNK)
        pool_o, pool_new = _pool(pp["pu"], jnp.zeros((bp, 16, BRANCH_W), F32), wbd, pscale, bp, seq, pool_t, 0)
        xp = _merge(xp, ng, wmg, dsa_o, diff_o, ssd_o, pool_o, pp["gates"], wbr, wout, min(_MERGE_ROWS, seq))
        def heads_out(a, nh, hd):
            return a.reshape(bp, nh, hd, seq).transpose(0, 3, 1, 2)

        for n, v in zip(acc_p, (heads_out(pp["dkTf"], DSA_HEADS, DSA_HEAD_DIM),
                                heads_out(pp["dvTf"], DSA_HEADS, DSA_HEAD_DIM),
                                pp["ikTf"].transpose(0, 2, 1),
                                heads_out(pp["fkTf"], 2 * DIFF_HEADS, DIFF_HEAD_DIM),
                                heads_out(pp["fvTf"], DIFF_HEADS, DIFF_V_DIM),
                                ssm_new, conv_new, pool_new)):
            acc_p[n].append(v)

        ps = _proj(xs, ng, wp, *tab_s, gains, tm_s, (bs * t) // tm_s)
        bias = _dsa_sidx(ps, cik, l, page_table, bs, t)
        dsa_o = _paged_attn(ps["dq"], ps["dk"], ps["dvb"], ck, cv, l, page_table, bs, t,
                            nheads=DSA_HEADS, log2_hd=6, bias=bias, finish=_dsa_finish, name="dsa_sample_attn")
        diff_o = _paged_attn(ps["fq"], ps["fk"], ps["fvb"], cfk, cfv, l, page_table, bs, t,
                             nheads=2 * DIFF_HEADS, log2_hd=5, extra=(lam_p, gsub),
                             finish=functools.partial(_diff_sample_finish, lam_init=lam_init),
                             name="diff_sample_attn")
        cprev8 = jnp.concatenate([jnp.zeros((bs, 8 - (SSD_CONV - 1), SSD_CONV_DIM), F32), state_conv[l]], axis=1)
        ssd_o, ssm_new, conv_new = _ssd(ps, ssd_par, cprev8, state_ssm[l], cw, cb, sng, bs, t, t)
        halo = jnp.concatenate([jnp.zeros((bs, 1, BRANCH_W), F32), state_pool[l]], axis=1)
        pool_o, pool_new = _pool(ps["pu"], halo, wbd, pscale, bs, t, t, past)
        xs = _merge(xs, ng, wmg, dsa_o, diff_o, ssd_o, pool_o, ps["gates"], wbr, wout, tm_s)
        for n, v in zip(acc_s, (ps["dk"].reshape(bs, t, DSA_HEADS, DSA_HEAD_DIM),
                                ps["dv"].reshape(bs, t, DSA_HEADS, DSA_HEAD_DIM),
                                ps["misc"][:, :IDX_DIM].reshape(bs, t, IDX_DIM),
                                ps["fk"].reshape(bs, t, 2 * DIFF_HEADS, DIFF_HEAD_DIM),
                                ps["fv"].reshape(bs, t, DIFF_HEADS, DIFF_V_DIM),
                                ssm_new, conv_new, pool_new)):
            acc_s[n].append(v)

    names = ("dk", "dv", "ik", "fk", "fv", "ssm", "conv", "pool")
    return ((xp.reshape(bp, seq, d), xs.reshape(bs, t, d))
            + tuple(jnp.stack(acc_p[n]) for n in names) + tuple(jnp.stack(acc_s[n]) for n in names))
```

```python
import functools
import math

import jax
import jax.numpy as jnp
from jax import lax
from jax.experimental import pallas as pl
from jax.experimental.pallas import tpu as pltpu

F32 = jnp.float32
BF16 = jnp.bfloat16
I32 = jnp.int32

BRANCH_W = 256
DSA_HEADS = 4
DSA_HEAD_DIM = 64
IDX_HEADS = 4
IDX_DIM = 64
IDX_TOPK_MAX = 256
DIFF_HEADS = 4
DIFF_HEAD_DIM = 32
DIFF_V_DIM = 64
SSD_HEADS = 4
SSD_HEAD_DIM = 64
SSD_STATE = 64
SSD_GROUPS = 2
SSD_CONV = 4
SSD_CHUNK = 128
SSD_CONV_DIM = 512
POOL_WINDOWS = (2, 4, 8, 16)
POOL_STATE = 15
PAGE_SIZE = 128
ROPE_THETA = 10000.0
EPS = 1e-6
N_BRANCH = 4

LANES = 128
SUBLANES = 8
VMEM_LIMIT = 56 * 1024 * 1024

LOG2E = math.log2(math.e)
NEG = -1e30
INT_MIN = -(2 ** 31)
HIGHEST = lax.Precision.HIGHEST

_SLABS = (("dq", 256), ("dk", 256), ("dv", 256), ("iq", 256), ("fq", 256), ("fk", 256), ("fv", 256),
          ("gates", 1024), ("sxbc", 512), ("pu", 256), ("misc", 128), ("dts", 128))
_OFF = {}
_o = 0
for _n, _w in _SLABS:
    _OFF[_n] = (_o, _o + _w)
    _o += _w
W_PACKED = _o


def _cparams(n_axes, vmem=VMEM_LIMIT):
    return pltpu.CompilerParams(dimension_semantics=("arbitrary",) * n_axes, vmem_limit_bytes=vmem)


def _nk_dot(a, b):
    return lax.dot_general(a, b, (((1,), (1,)), ((), ())), preferred_element_type=F32)


def _silu(x):
    return x * (1.0 / (1.0 + jnp.exp(-x)))


def _sigmoid(x):
    return 1.0 / (1.0 + jnp.exp(-x))


def _group_of_lane(shape, log2_width):
    return lax.broadcasted_iota(I32, shape, len(shape) - 1) >> log2_width


def _proj_kernel(x_ref, ng_ref, w_ref, t64_ref, t32_ref, tmisc_ref, gains_ref,
                 dq_ref, dk_ref, dv_ref, dvb_ref, iq_ref, misc_ref, miscb_ref, fq_ref, fk_ref, fv_ref, fvb_ref,
                 gates_ref, sxbc_ref, pu_ref, dts_ref, dkt_ref, fkt_ref,
                 dktf_ref, dvtf_ref, iktf_ref, fktf_ref, fvtf_ref):
    x = x_ref[...]
    h = x * lax.rsqrt(jnp.mean(x * x, axis=-1, keepdims=True) + EPS) * ng_ref[...]
    hb = h.astype(BF16)

    def mm(name):
        a, b = _OFF[name]
        return jnp.dot(hb, w_ref[:, a:b], preferred_element_type=F32)

    def head_norm(z, g, log2_hd):
        n = z.shape[1]
        r = lax.broadcasted_iota(I32, (n, n), 0) >> log2_hd
        c = lax.broadcasted_iota(I32, (n, n), 1) >> log2_hd
        bd = jnp.where(r == c, 1.0 / (1 << log2_hd), 0.0).astype(F32)
        ms = jnp.dot(z * z, bd, preferred_element_type=F32, precision=HIGHEST)
        return z * lax.rsqrt(ms + EPS) * g

    def rope(z, tab_ref, half):
        n = z.shape[1]
        cos = tab_ref[:, :n]
        sin = tab_ref[:, n:]
        lane = lax.broadcasted_iota(I32, z.shape, 1)
        first = (lane & (2 * half - 1)) < half
        partner = jnp.where(first, pltpu.roll(z, n - half, 1), pltpu.roll(z, half, 1))
        return z * cos + partner * sin

    dq = rope(head_norm(mm("dq"), gains_ref[0:1, :], 6), t64_ref, 32)
    dq_ref[...] = (dq * (DSA_HEAD_DIM ** -0.5 * LOG2E)).astype(BF16)
    dk = rope(head_norm(mm("dk"), gains_ref[1:2, :], 6), t64_ref, 32)
    dk_ref[...] = dk
    dk_t = dk.T
    dktf_ref[...] = dk_t
    dkt_ref[...] = dk_t.astype(BF16)
    dv = mm("dv")
    dv_ref[...] = dv
    dvtf_ref[...] = dv.T
    dvb_ref[...] = dv.astype(BF16)
    iq_ref[...] = (rope(mm("iq"), t64_ref, 32) * (IDX_DIM ** -0.5)).astype(BF16)
    misc = rope(mm("misc"), tmisc_ref, 32)
    misc_ref[...] = misc
    iktf_ref[...] = misc.T[0:IDX_DIM, :]
    miscb_ref[...] = misc.astype(BF16)
    fq = rope(head_norm(mm("fq"), gains_ref[2:3, :], 5), t32_ref, 16)
    fq_ref[...] = (fq * (DIFF_HEAD_DIM ** -0.5 * LOG2E)).astype(BF16)
    fk = rope(head_norm(mm("fk"), gains_ref[3:4, :], 5), t32_ref, 16)
    fk_ref[...] = fk
    fk_t = fk.T
    fktf_ref[...] = fk_t
    fkt_ref[...] = fk_t.astype(BF16)
    fv = mm("fv")
    fv_ref[...] = fv
    fvtf_ref[...] = fv.T
    fvb_ref[...] = fv.astype(BF16)
    for j in range(4):
        a = _OFF["gates"][0] + 256 * j
        gates_ref[:, 256 * j:256 * (j + 1)] = jnp.dot(hb, w_ref[:, a:a + 256], preferred_element_type=F32)
    for j in range(2):
        a = _OFF["sxbc"][0] + 256 * j
        sxbc_ref[:, 256 * j:256 * (j + 1)] = jnp.dot(hb, w_ref[:, a:a + 256], preferred_element_type=F32)
    pu_ref[...] = mm("pu")
    dts_ref[...] = mm("dts")


_PROJ_OUTS = (("dq", 256, BF16), ("dk", 256, F32), ("dv", 256, F32), ("dvb", 256, BF16), ("iq", 256, BF16),
              ("misc", 128, F32), ("miscb", 128, BF16), ("fq", 256, BF16), ("fk", 256, F32), ("fv", 256, F32), ("fvb", 256, BF16),
              ("gates", 1024, F32), ("sxbc", 512, F32), ("pu", 256, F32), ("dts", 128, F32))
_PROJ_T_OUTS = (("dkT", 256), ("fkT", 256))
_PROJ_TF_OUTS = (("dkTf", 256), ("dvTf", 256), ("ikTf", IDX_DIM), ("fkTf", 256), ("fvTf", 256))


def _proj(x, ng, wp, t64, t32, tmisc, gains, tm, tab_blocks):
    n, d = x.shape
    row = lambda i: (i, 0)
    tab = lambda i: (i % tab_blocks, 0)
    const = lambda i: (0, 0)
    run = lambda i: (i // tab_blocks, 0, i % tab_blocks)
    n_runs = n // (tm * tab_blocks)
    outs = _PROJ_OUTS
    res = pl.pallas_call(
        _proj_kernel,
        grid=(n // tm,),
        in_specs=[pl.BlockSpec((tm, d), row), pl.BlockSpec((1, d), const), pl.BlockSpec(wp.shape, const),
                  pl.BlockSpec((tm, 512), tab), pl.BlockSpec((tm, 512), tab), pl.BlockSpec((tm, 256), tab),
                  pl.BlockSpec((4, 256), const)],
        out_specs=([pl.BlockSpec((tm, w), row) for _, w, _ in outs]
                   + [pl.BlockSpec((None, w, tm), lambda i: (i, 0, 0)) for _, w in _PROJ_T_OUTS]
                   + [pl.BlockSpec((None, w, tm), run) for _, w in _PROJ_TF_OUTS]),
        out_shape=([jax.ShapeDtypeStruct((n, w), dt) for _, w, dt in outs]
                   + [jax.ShapeDtypeStruct((n // tm, w, tm), BF16) for _, w in _PROJ_T_OUTS]
                   + [jax.ShapeDtypeStruct((n_runs, w, tm * tab_blocks), F32) for _, w in _PROJ_TF_OUTS]),
        compiler_params=_cparams(1),
        name="proj",
    )(x, ng, wp, t64, t32, tmisc, gains)
    return dict(zip([o[0] for o in outs] + [o[0] for o in _PROJ_T_OUTS] + [o[0] for o in _PROJ_TF_OUTS], res))


def _sortable(x):
    bits = pltpu.bitcast(x, I32)
    return bits ^ ((bits >> 31) & 0x7FFFFFFF)


def _flash_rows(s_ref, bias, m_ref, l_ref, alpha_ref, p_ref, r0, r1):
    c_sz = s_ref.shape[1]
    nq = c_sz // LANES
    parts = []
    for q in range(nq):
        v = s_ref[r0:r1, q * LANES:(q + 1) * LANES]
        parts.append(v if bias is None else v + bias[:, q * LANES:(q + 1) * LANES])
    mx = parts[0]
    for q in range(1, nq):
        mx = jnp.maximum(mx, parts[q])
    m_old = m_ref[r0:r1, :]
    m_new = jnp.maximum(m_old, jnp.max(mx, axis=1, keepdims=True))
    alpha = jnp.exp2(m_old - m_new)
    m_ref[r0:r1, :] = m_new
    alpha_ref[r0:r1, :] = alpha
    lsum = None
    for q in range(nq):
        p = jnp.exp2(parts[q] - m_new)
        p_ref[r0:r1, q * LANES:(q + 1) * LANES] = p.astype(BF16)
        lsum = p if lsum is None else lsum + p
    l_ref[r0:r1, :] = alpha * l_ref[r0:r1, :] + lsum


def _flash_loop(n, qa_ref, kt_ref, v_ref, bias_fn, nh, tq, s_scr, p_scr, alpha_scr, m_scr, l_scr, acc_scr):
    c_sz = kt_ref.shape[2]
    rows = nh * tq
    m_scr[...] = jnp.full((rows, LANES), NEG, F32)
    l_scr[...] = jnp.zeros((rows, LANES), F32)
    acc_scr[...] = jnp.zeros((rows, BRANCH_W), F32)

    def scores(c, slot):
        s_scr[slot] = jnp.dot(qa_ref[...], kt_ref[c], preferred_element_type=F32)

    def softmax(c, slot):
        bias = bias_fn(c)
        for h in range(nh):
            _flash_rows(s_scr.at[slot], bias, m_scr, l_scr, alpha_scr.at[slot], p_scr.at[slot],
                        h * tq, (h + 1) * tq)

    def values(c, slot):
        start = pl.multiple_of(c * c_sz, c_sz)
        pv = jnp.dot(p_scr[slot], v_ref[pl.ds(start, c_sz), :], preferred_element_type=F32)
        alpha = alpha_scr[slot]
        acc_scr[...] = jnp.concatenate([alpha, alpha], axis=1) * acc_scr[...] + pv

    scores(0, 0)
    scores(jnp.minimum(1, n - 1), 1)
    softmax(0, 0)

    def body(j, carry):
        c = 2 * j + 1
        scores(jnp.minimum(c + 1, n - 1), 0)
        values(c - 1, 0)
        softmax(c, 1)
        scores(jnp.minimum(c + 2, n - 1), 1)
        values(c, 1)
        softmax(c + 1, 0)
        return carry

    lax.fori_loop(0, (n - 1) // 2, body, 0)

    @pl.when((n - 1) % 2 == 1)
    def _():
        values(n - 2, 0)
        softmax(n - 1, 1)

    values(n - 1, (n - 1) & 1)


def _diff_lambda(lam_ref, lam_init):
    lp = lam_ref[...]
    s1 = jnp.sum(lp[0:1, :] * lp[1:2, :], axis=1, keepdims=True)
    s2 = jnp.sum(lp[2:3, :] * lp[3:4, :], axis=1, keepdims=True)
    return jnp.exp(s1) - jnp.exp(s2) + lam_init


def _diff_finish(acc_ref, l_ref, rows, lam, lam_init, gsub):
    grp = _group_of_lane((rows, BRANCH_W), 6)
    out = jnp.zeros((rows, BRANCH_W), F32)
    for j in range(DIFF_HEADS):
        r1, r2 = 2 * j * rows, (2 * j + 1) * rows
        a1 = acc_ref[r1:r1 + rows, :] / jnp.sum(l_ref[r1:r1 + rows, :], axis=1, keepdims=True)
        a2 = acc_ref[r2:r2 + rows, :] / jnp.sum(l_ref[r2:r2 + rows, :], axis=1, keepdims=True)
        o = a1 - lam * a2
        ms = jnp.sum(jnp.where(grp == j, o * o, 0.0), axis=1, keepdims=True) * (1.0 / DIFF_V_DIM)
        out = out + jnp.where(grp == j, o * lax.rsqrt(ms + EPS), 0.0)
    return out * gsub * (1.0 - lam_init)


def _dsa_prompt_kernel(iq_ref, misc_ref, dq_ref, ikn_ref, dk_ref, dv_ref, tril_ref, o_ref,
                       key_scr, half_scr, st_scr, qit_scr, qa_scr, m_scr, l_scr, alpha_scr, acc_scr, p_scr, s_scr,
                       *, k_top):
    tq = iq_ref.shape[0]
    c_sz = dk_ref.shape[2]
    seq = dk_ref.shape[0] * c_sz
    i = pl.program_id(1)
    nch = ((i + 1) * tq + c_sz - 1) // c_sz
    qpos = i * tq + lax.broadcasted_iota(I32, (1, tq), 1)
    sub_c = lax.broadcasted_iota(I32, (c_sz, tq), 0)

    dq = dq_ref[...]
    grp = _group_of_lane((tq, BRANCH_W), 6)
    for h in range(DSA_HEADS):
        qa_scr[h * tq:(h + 1) * tq, :] = jnp.where(grp == h, dq, jnp.zeros_like(dq))
    iq_t = iq_ref[...].astype(F32).T
    qit_scr[IDX_DIM:, :] = jnp.zeros((LANES - IDX_DIM, IDX_HEADS * tq), BF16)
    for h in range(IDX_HEADS):
        qit_scr[0:IDX_DIM, h * tq:(h + 1) * tq] = iq_t[IDX_DIM * h:IDX_DIM * (h + 1), :].astype(BF16)
    w_t = misc_ref[...].T

    def p1(c, carry):
        start = pl.multiple_of(c * c_sz, c_sz)
        st_scr[...] = jnp.dot(ikn_ref[pl.ds(start, c_sz), :], qit_scr[...], preferred_element_type=F32)
        acc = None
        for h in range(IDX_HEADS):
            t = w_t[IDX_DIM + h:IDX_DIM + h + 1, :] * jnp.maximum(st_scr[:, h * tq:(h + 1) * tq], 0.0)
            acc = t if acc is None else acc + t
        acc = jnp.where(acc == 0.0, 0.0, acc)
        key = jnp.where(c * c_sz + sub_c <= qpos, _sortable(acc), INT_MIN)
        key_scr[c] = key
        half_scr[c] = key >> 1
        return carry

    lax.fori_loop(0, nch, p1, 0)

    fold = 4 * SUBLANES

    def count(pred):
        def body(c, cnt):
            m = pred(key_scr[c], c)
            return cnt + jnp.sum(m.reshape(c_sz // fold, fold, tq), axis=0)
        cnt = lax.fori_loop(0, nch, body, jnp.zeros((fold, tq), F32))
        return jnp.sum(cnt, axis=0, keepdims=True)

    kf = float(k_top)
    short = qpos + 1 < k_top
    c0 = count(lambda k, c: jnp.where(k >= 0, 1.0, 0.0))
    cand0 = jnp.where(c0 >= kf, 0, INT_MIN).astype(I32)
    cnt0 = jnp.where(c0 >= kf, c0, 2.0 ** 30)

    def count_even(trial):
        th = trial >> 1

        def body(c, acc):
            below = (half_scr[c] - th) >> 31
            return acc + jnp.sum(below.reshape(c_sz // fold, fold, tq), axis=0)

        neg = lax.fori_loop(0, nch, body, jnp.zeros((fold, tq), I32))
        return (nch * c_sz).astype(F32) + jnp.sum(neg.astype(F32), axis=0, keepdims=True)

    def refine(shift, cand, cnt_c, even=False):
        trial = cand | jnp.left_shift(jnp.int32(1), shift)
        cnt = count_even(trial) if even else count(lambda k, c: jnp.where(k >= trial, 1.0, 0.0))
        return jnp.where(cnt >= kf, trial, cand), jnp.where(cnt >= kf, cnt, cnt_c)

    cand0, cnt0 = refine(30, cand0, cnt0)
    probe = cand0 + 1
    tied = jnp.where(count(lambda k, c: jnp.where(k >= probe, 1.0, 0.0)) < kf, 1.0, 0.0)
    settled = jnp.where(short, 1.0, tied)

    def unresolved(cnt_c):
        return (jnp.max(jnp.where(settled > 0.0, 0.0, jnp.where(cnt_c == kf, 0.0, 1.0))) > 0.0).astype(I32)

    def bits_cond(st):
        return jnp.logical_and(st[0] < 14, st[3] > 0)

    def bits_body(st):
        j, cand, cnt_c, _ = st
        cand, cnt_c = refine(29 - 2 * j, cand, cnt_c, even=True)
        cand, cnt_c = refine(28 - 2 * j, cand, cnt_c, even=True)
        return j + 1, cand, cnt_c, unresolved(cnt_c)

    def last_bits(st):
        cand, cnt_c = refine(1, st[0], st[1], even=True)
        return refine(0, cand, cnt_c)

    _, thr, cnt_ge, go = lax.while_loop(bits_cond, bits_body, (jnp.int32(0), cand0, cnt0, unresolved(cnt0)))
    thr, cnt_ge = lax.cond(go > 0, last_bits, lambda st: st, (thr, cnt_ge))
    need = jnp.where(short, 0.0, jnp.where(cnt_ge > kf, 1.0, 0.0))

    @pl.when(jnp.max(need) > 0.0)
    def _():
        rem = kf - count(lambda k, c: jnp.where(k > thr, 1.0, 0.0))
        rem = jnp.where(need > 0.0, rem, 2.0 ** 30)

        def demote(c, base):
            k = key_scr[c]
            tie = jnp.where(k == thr, 1.0, 0.0)
            rank = base + jnp.dot(tril_ref[...], tie.astype(BF16), preferred_element_type=F32)
            key_scr[c] = jnp.where(k == thr, jnp.where(rank > rem, INT_MIN, k), k)
            return rank[c_sz - 1:c_sz, :]

        lax.fori_loop(0, nch, demote, jnp.zeros((1, tq), F32))

    thr_sel = jnp.where(short, INT_MIN + 1, thr)

    def sel_bias(c):
        return jnp.where(key_scr[c] >= thr_sel, 0.0, NEG).T

    _flash_loop(nch, qa_scr, dk_ref, dv_ref, sel_bias, DSA_HEADS, tq, s_scr, p_scr, alpha_scr, m_scr, l_scr, acc_scr)
    o_ref[...] = _dsa_finish(acc_scr, l_scr, tq)


def _dsa_prompt(p, bp, seq, tq):
    c_sz = p["dkT"].shape[2]
    nq = seq // tq
    k_top = min(IDX_TOPK_MAX, seq // 4)
    qrow = lambda b, i: (b * nq + i, 0)
    kv = lambda b, i: (b, 0)
    kvt = lambda b, i: (b, 0, 0)
    rows = DSA_HEADS * tq
    return pl.pallas_call(
        functools.partial(_dsa_prompt_kernel, k_top=k_top),
        grid=(bp, nq),
        in_specs=[pl.BlockSpec((tq, 256), qrow), pl.BlockSpec((tq, 128), qrow), pl.BlockSpec((tq, 256), qrow),
                  pl.BlockSpec((seq, LANES), kv), pl.BlockSpec((seq // c_sz, 256, c_sz), kvt),
                  pl.BlockSpec((seq, 256), kv), pl.BlockSpec((c_sz, c_sz), lambda b, i: (0, 0))],
        out_specs=pl.BlockSpec((tq, 256), qrow),
        out_shape=jax.ShapeDtypeStruct((bp * seq, BRANCH_W), F32),
        scratch_shapes=[pltpu.VMEM((seq // c_sz, c_sz, tq), I32), pltpu.VMEM((seq // c_sz, c_sz, tq), I32),
                        pltpu.VMEM((c_sz, IDX_HEADS * tq), F32),
                        pltpu.VMEM((LANES, IDX_HEADS * tq), BF16),
                        pltpu.VMEM((rows, 256), BF16), pltpu.VMEM((rows, LANES), F32),
                        pltpu.VMEM((rows, LANES), F32), pltpu.VMEM((2, rows, LANES), F32),
                        pltpu.VMEM((rows, 256), F32), pltpu.VMEM((2, rows, c_sz), BF16),
                        pltpu.VMEM((2, rows, c_sz), F32)],
        compiler_params=_cparams(2),
        name="dsa_prompt",
    )(p["iq"], p["misc"], p["dq"], p["miscb"], p["dkT"], p["dvb"], jnp.tril(jnp.ones((c_sz, c_sz), BF16)))


def _diff_prompt_kernel(fq_ref, fk_ref, fv_ref, lam_ref, gsub_ref, o_ref,
                        qa_scr, m_scr, l_scr, alpha_scr, acc_scr, p_scr, s_scr, *, lam_init):
    tq = fq_ref.shape[0]
    c_sz = fk_ref.shape[2]
    nh = 2 * DIFF_HEADS
    i = pl.program_id(1)
    n_full = (i * tq + 1) // c_sz
    qpos = i * tq + lax.broadcasted_iota(I32, (tq, 1), 0)
    lane_c = lax.broadcasted_iota(I32, (tq, c_sz), 1)
    fq = fq_ref[...]
    grp = _group_of_lane((tq, BRANCH_W), 5)
    for h in range(nh):
        qa_scr[h * tq:(h + 1) * tq, :] = jnp.where(grp == h, fq, jnp.zeros_like(fq))

    def causal_bias(c):
        return jnp.where(c * c_sz + lane_c <= qpos, 0.0, NEG)

    _flash_loop(n_full + 1, qa_scr, fk_ref, fv_ref, causal_bias, nh, tq,
                s_scr, p_scr, alpha_scr, m_scr, l_scr, acc_scr)
    lam = _diff_lambda(lam_ref, lam_init)
    o_ref[...] = _diff_finish(acc_scr, l_scr, tq, lam, lam_init, gsub_ref[...])


def _diff_prompt(p, lam_p, gsub, lam_init, bp, seq, tq):
    c_sz = p["fkT"].shape[2]
    nq = seq // tq
    qrow = lambda b, i: (b * nq + i, 0)
    kv = lambda b, i: (b, 0)
    const = lambda b, i: (0, 0)
    rows = 2 * DIFF_HEADS * tq
    return pl.pallas_call(
        functools.partial(_diff_prompt_kernel, lam_init=lam_init),
        grid=(bp, nq),
        in_specs=[pl.BlockSpec((tq, 256), qrow),
                  pl.BlockSpec((seq // c_sz, 256, c_sz), lambda b, i: (b, 0, 0)), pl.BlockSpec((seq, 256), kv),
                  pl.BlockSpec((4, DIFF_HEAD_DIM), const), pl.BlockSpec((1, 256), const)],
        out_specs=pl.BlockSpec((tq, 256), qrow),
        out_shape=jax.ShapeDtypeStruct((bp * seq, BRANCH_W), F32),
        scratch_shapes=[pltpu.VMEM((rows, 256), BF16), pltpu.VMEM((rows, LANES), F32),
                        pltpu.VMEM((rows, LANES), F32), pltpu.VMEM((2, rows, LANES), F32),
                        pltpu.VMEM((rows, 256), F32), pltpu.VMEM((2, rows, c_sz), BF16),
                        pltpu.VMEM((2, rows, c_sz), F32)],
        compiler_params=_cparams(2),
        name="diff_prompt",
    )(p["fq"], p["fkT"], p["fvb"], lam_p, gsub)


def _ssd_kernel(par_ref, xbc_ref, dts_ref, z_ref, cprev_ref, hprev_ref, cw_ref, cb_ref, ng_ref,
                y_ref, hout_ref, cout_ref, e_scr, h_scr):
    qin = xbc_ref.shape[0]
    q = SSD_CHUNK
    c = pl.program_id(1)
    last = pl.num_programs(1) - 1

    @pl.when(c == 0)
    def _():
        e_scr[0:8, :] = cprev_ref[...]
        h_scr[...] = hprev_ref[...]
        if qin < q:
            e_scr[8 + qin:8 + q, :] = jnp.zeros((q - qin, SSD_CONV_DIM), F32)

    e_scr[8:8 + qin, :] = xbc_ref[...]
    conv = cb_ref[...] + jnp.zeros((q, SSD_CONV_DIM), F32)
    for k in range(SSD_CONV):
        conv = conv + e_scr[5 + k:5 + k + q, :] * cw_ref[k:k + 1, :]
    new_tail = e_scr[qin:qin + 8, :]
    u = _silu(conv)
    xs = u[:, :BRANCH_W]
    b_all = u[:, BRANCH_W:BRANCH_W + SSD_GROUPS * SSD_STATE]
    bm_t = b_all.T.astype(BF16)
    bm = b_all.astype(BF16)
    cm = u[:, BRANCH_W + SSD_GROUPS * SSD_STATE:].astype(BF16)

    dts = dts_ref[...]
    if qin < q:
        dts = jnp.concatenate([dts, jnp.zeros((q - qin, LANES), F32)], axis=0)
    rowi = lax.broadcasted_iota(I32, (q, LANES), 0)
    lanei = lax.broadcasted_iota(I32, (q, LANES), 1)
    bias_l = jnp.zeros((q, LANES), F32)
    alog_l = jnp.zeros((q, LANES), F32)
    for h in range(SSD_HEADS):
        bias_l = jnp.where(lanei == h, par_ref[h], bias_l)
        alog_l = jnp.where(lanei == h, par_ref[SSD_HEADS + h], alog_l)
    pre = dts + bias_l
    dt = jnp.maximum(pre, 0.0) + jnp.log1p(jnp.exp(-jnp.abs(pre)))
    dt = jnp.where(rowi < qin, jnp.where(lanei < SSD_HEADS, dt, 0.0), 0.0)
    adt = dt * (-jnp.exp(alog_l))
    r_qq = lax.broadcasted_iota(I32, (q, q), 0)
    c_qq = lax.broadcasted_iota(I32, (q, q), 1)
    causal = r_qq >= c_qq
    acs_col = jnp.dot(jnp.where(causal, 1.0, 0.0), adt, preferred_element_type=F32, precision=HIGHEST)
    acs_row = jnp.dot(adt.T[0:8, :], jnp.where(r_qq <= c_qq, 1.0, 0.0), preferred_element_type=F32,
                      precision=HIGHEST)

    grp = _group_of_lane((q, BRANCH_W), 6)
    grp1 = _group_of_lane((1, BRANCH_W), 6)
    dt_b = jnp.zeros((q, BRANCH_W), F32)
    a_b = jnp.zeros((q, BRANCH_W), F32)
    d_b = jnp.zeros((1, BRANCH_W), F32)
    for h in range(SSD_HEADS):
        dt_b = jnp.where(grp == h, dt[:, h:h + 1], dt_b)
        a_b = jnp.where(grp == h, acs_col[:, h:h + 1], a_b)
        d_b = jnp.where(grp1 == h, par_ref[2 * SSD_HEADS + h], d_b)
    xdt = xs * dt_b
    xdt_b = xdt.astype(BF16)
    a_last = a_b[q - 1:q, :]

    y_diag = jnp.zeros((q, BRANCH_W), F32)
    hpg = SSD_HEADS // SSD_GROUPS
    cbs = [_nk_dot(cm[:, 64 * g:64 * (g + 1)], bm[:, 64 * g:64 * (g + 1)]) for g in range(SSD_GROUPS)]
    for h in range(SSD_HEADS):
        lmat = jnp.exp(jnp.where(causal, acs_col[:, h:h + 1] - acs_row[h:h + 1, :], -jnp.inf))
        y_diag = y_diag + jnp.dot((cbs[h // hpg] * lmat).astype(BF16),
                                  jnp.where(grp == h, xdt_b, jnp.zeros_like(xdt_b)), preferred_element_type=F32)
    ht = h_scr[...]
    htb = ht.astype(BF16)
    xdec = (xdt * jnp.exp(a_last - a_b)).astype(BF16)
    st = [jnp.dot(bm_t[64 * g:64 * (g + 1), :], xdec, preferred_element_type=F32) for g in range(SSD_GROUPS)]
    yo = [jnp.dot(cm[:, 64 * g:64 * (g + 1)], htb, preferred_element_type=F32) for g in range(SSD_GROUPS)]
    grp_n = _group_of_lane((SSD_STATE, BRANCH_W), 6)
    h_scr[...] = ht * jnp.exp(a_last) + jnp.where(grp_n < hpg, st[0], st[1])
    y = y_diag + jnp.where(grp < hpg, yo[0], yo[1]) * jnp.exp(a_b) + xs * d_b
    gte = y * _silu(z_ref[...]) if qin == q else y[0:qin, :] * _silu(z_ref[...])
    half = BRANCH_W // SSD_GROUPS
    outs = []
    for g in range(SSD_GROUPS):
        gg = gte[:, half * g:half * (g + 1)]
        outs.append(gg * lax.rsqrt(jnp.mean(gg * gg, axis=-1, keepdims=True) + EPS))
    y_ref[...] = jnp.concatenate(outs, axis=1) * ng_ref[...]
    e_scr[0:8, :] = new_tail

    @pl.when(c == last)
    def _():
        hout_ref[...] = h_scr[...]
        cout_ref[...] = new_tail


def _ssd(p, par, cprev8, hprev, cw, cb, ng, nb, rows_per_b, qin):
    nc = rows_per_b // qin
    row = lambda b, c: (b * nc + c, 0)
    const = lambda b, c: (0, 0)
    y, hout, cout = pl.pallas_call(
        _ssd_kernel,
        grid=(nb, nc),
        in_specs=[pl.BlockSpec(memory_space=pltpu.SMEM),
                  pl.BlockSpec((qin, SSD_CONV_DIM), row), pl.BlockSpec((qin, LANES), row),
                  pl.BlockSpec((qin, 256), lambda b, c: (b * nc + c, 2)),
                  pl.BlockSpec((None, 8, SSD_CONV_DIM), lambda b, c: (b, 0, 0)),
                  pl.BlockSpec((None, SSD_STATE, BRANCH_W), lambda b, c: (b, 0, 0)),
                  pl.BlockSpec((SSD_CONV, SSD_CONV_DIM), const), pl.BlockSpec((1, SSD_CONV_DIM), const),
                  pl.BlockSpec((1, BRANCH_W), const)],
        out_specs=[pl.BlockSpec((qin, BRANCH_W), row),
                   pl.BlockSpec((None, SSD_STATE, BRANCH_W), lambda b, c: (b, 0, 0)),
                   pl.BlockSpec((None, 8, SSD_CONV_DIM), lambda b, c: (b, 0, 0))],
        out_shape=[jax.ShapeDtypeStruct((nb * rows_per_b, BRANCH_W), F32),
                   jax.ShapeDtypeStruct((nb, SSD_STATE, BRANCH_W), F32),
                   jax.ShapeDtypeStruct((nb, 8, SSD_CONV_DIM), F32)],
        scratch_shapes=[pltpu.VMEM((8 + SSD_CHUNK, SSD_CONV_DIM), F32),
                        pltpu.VMEM((SSD_STATE, BRANCH_W), F32)],
        compiler_params=_cparams(2),
        name="ssd",
    )(par, p["sxbc"], p["dts"], p["gates"], cprev8, _state_to_lanes(hprev), cw, cb, ng)
    return y, _state_from_lanes(hout), cout[:, 8 - (SSD_CONV - 1):, :]


def _state_to_lanes(h):
    b = h.shape[0]
    return h.reshape(b, SSD_HEADS * SSD_HEAD_DIM, SSD_STATE).transpose(0, 2, 1)


def _state_from_lanes(ht):
    b = ht.shape[0]
    return ht.transpose(0, 2, 1).reshape(b, SSD_HEADS, SSD_HEAD_DIM, SSD_STATE)


def _pool_kernel(u_ref, halo_ref, w_ref, scale_ref, o_ref, hout_ref, e_scr, *, start_pos):
    t = u_ref.shape[0]
    c = pl.program_id(1)
    last = pl.num_programs(1) - 1

    @pl.when(c == 0)
    def _():
        e_scr[0:8, :] = jnp.zeros((8, BRANCH_W), F32)
        e_scr[8:24, :] = halo_ref[...]

    u = u_ref[...]
    e_scr[24:24 + t, :] = u
    new_halo = e_scr[8 + t:24 + t, :]
    n = 16 + t
    cur = e_scr[8:8 + n, :]
    stages = []
    for k in (1, 2, 4, 8):
        cur = cur + e_scr[8 - k:8 - k + n, :]
        stages.append(cur[16:, :])
        e_scr[8:8 + n, :] = cur
    grp = _group_of_lane((t, BRANCH_W), 6)
    win = jnp.where(grp == 0, stages[0], jnp.where(grp == 1, stages[1], jnp.where(grp == 2, stages[2], stages[3])))
    wlen = jnp.where(grp == 0, 2.0, jnp.where(grp == 1, 4.0, jnp.where(grp == 2, 8.0, 16.0)))
    n_avail = (start_pos + c * t + 1 + lax.broadcasted_iota(I32, (t, BRANCH_W), 0)).astype(F32)
    d = win / jnp.minimum(wlen, n_avail) - u
    o_ref[...] = jnp.dot(d.astype(BF16), w_ref[...], preferred_element_type=F32) * scale_ref[...]
    e_scr[8:24, :] = new_halo

    @pl.when(c == last)
    def _():
        hout_ref[...] = new_halo


def _pool(u, halo16, wbd, scale, nb, rows_per_b, t, start_pos):
    nc = rows_per_b // t
    row = lambda b, c: (b * nc + c, 0)
    const = lambda b, c: (0, 0)
    o, hout = pl.pallas_call(
        functools.partial(_pool_kernel, start_pos=start_pos),
        grid=(nb, nc),
        in_specs=[pl.BlockSpec((t, BRANCH_W), row), pl.BlockSpec((None, 16, BRANCH_W), lambda b, c: (b, 0, 0)),
                  pl.BlockSpec((BRANCH_W, BRANCH_W), const), pl.BlockSpec((1, BRANCH_W), const)],
        out_specs=[pl.BlockSpec((t, BRANCH_W), row), pl.BlockSpec((None, 16, BRANCH_W), lambda b, c: (b, 0, 0))],
        out_shape=[jax.ShapeDtypeStruct((nb * rows_per_b, BRANCH_W), F32),
                   jax.ShapeDtypeStruct((nb, 16, BRANCH_W), F32)],
        scratch_shapes=[pltpu.VMEM((24 + t, BRANCH_W), F32)],
        compiler_params=_cparams(2),
        name="pool",
    )(u, halo16, wbd, scale)
    return o, hout[:, 1:, :]


def _merge_kernel(x_ref, ng_ref, wmg_ref, dsa_ref, diff_ref, ssd_ref, pool_ref, gates_ref, wbr_ref, wout_ref, y_ref):
    x = x_ref[...]
    h = x * lax.rsqrt(jnp.mean(x * x, axis=-1, keepdims=True) + EPS) * ng_ref[...]
    hb = h.astype(BF16)
    d = x.shape[1]
    br = (dsa_ref[...] * _silu(gates_ref[:, 0:256]),
          diff_ref[...] * _silu(gates_ref[:, 256:512]),
          ssd_ref[...],
          pool_ref[...] * _silu(gates_ref[:, 768:1024]))
    m = jnp.zeros(x.shape, F32)
    for n in range(N_BRANCH):
        mg = jnp.dot(hb, wmg_ref[:, n * d:(n + 1) * d], preferred_element_type=F32)
        up = jnp.dot(br[n].astype(BF16), wbr_ref[n], preferred_element_type=F32)
        m = m + _sigmoid(mg) * up
    y_ref[...] = x + jnp.dot(m.astype(BF16), wout_ref[...], preferred_element_type=F32)


def _merge(x, ng, wmg, dsa_o, diff_o, ssd_o, pool_o, gates, wbr, wout, tm):
    n, d = x.shape
    row = lambda i: (i, 0)
    const = lambda i: (0, 0)
    return pl.pallas_call(
        _merge_kernel,
        grid=(n // tm,),
        in_specs=[pl.BlockSpec((tm, d), row), pl.BlockSpec((1, d), const), pl.BlockSpec(wmg.shape, const),
                  pl.BlockSpec((tm, 256), row), pl.BlockSpec((tm, 256), row), pl.BlockSpec((tm, 256), row),
                  pl.BlockSpec((tm, 256), row), pl.BlockSpec((tm, 1024), row),
                  pl.BlockSpec(wbr.shape, lambda i: (0, 0, 0)), pl.BlockSpec(wout.shape, const)],
        out_specs=pl.BlockSpec((tm, d), row),
        out_shape=jax.ShapeDtypeStruct((n, d), F32),
        compiler_params=_cparams(1),
        name="merge",
    )(x, ng, wmg, dsa_o, diff_o, ssd_o, pool_o, gates, wbr, wout)


_PAGES_PER_STEP = 32
_KEY_CHUNK = 512
_Q_BLOCK = 128
_MERGE_ROWS = 256


def _page_specs(layer, features, npp):
    def spec(k):
        return pl.BlockSpec((None, None, features, PAGE_SIZE),
                            lambda b, g, pt, k=k: (layer, pt[b, g * npp + k], 0, 0))
    return [spec(k) for k in range(npp)]


def _pages_view(cache):
    depth, n_pool, page = cache.shape[:3]
    c = cache.reshape(depth, n_pool, page, -1)
    return jnp.swapaxes(c, 2, 3)


def _dsa_sidx_kernel(pt_ref, iq_ref, misc_ref, *rest, npp, n_pages, k_top):
    pages = rest[:npp]
    bias_ref = rest[npp]
    key_scr, qi_scr, jst_scr = rest[npp + 1:]
    t = iq_ref.shape[0]
    g = pl.program_id(1)
    last = pl.num_programs(1) - 1
    wts = misc_ref[:, 64:64 + IDX_HEADS]

    def scores(kt):
        s = jnp.dot(qi_scr[...], kt, preferred_element_type=F32)
        acc = None
        for h in range(IDX_HEADS):
            v = wts[:, h:h + 1] * jnp.maximum(s[h * t:(h + 1) * t, :], 0.0)
            acc = v if acc is None else acc + v
        return jnp.where(acc == 0.0, 0.0, acc)

    @pl.when(g == 0)
    def _():
        iq = iq_ref[...]
        for h in range(IDX_HEADS):
            qi_scr[h * t:(h + 1) * t, :] = iq[:, IDX_DIM * h:IDX_DIM * (h + 1)]
        knew = jnp.concatenate([misc_ref[...], jnp.zeros((PAGE_SIZE - t, LANES), F32)], axis=0)
        sc = scores(knew.T[0:IDX_DIM, :].astype(BF16))
        causal = lax.broadcasted_iota(I32, (t, LANES), 1) <= lax.broadcasted_iota(I32, (t, LANES), 0)
        key_scr[n_pages] = jnp.where(causal, _sortable(sc), INT_MIN)

    sc = scores(jnp.concatenate([pg[...].astype(BF16) for pg in pages], axis=1))
    for k in range(npp):
        key_scr[g * npp + k] = _sortable(sc[:, k * PAGE_SIZE:(k + 1) * PAGE_SIZE])

    @pl.when(g == last)
    def _():
        keys = key_scr[...]
        kpos = (lax.broadcasted_iota(I32, keys.shape, 0) * PAGE_SIZE + lax.broadcasted_iota(I32, keys.shape, 2))

        def count(m):
            return jnp.sum(jnp.sum(m, axis=0), axis=1, keepdims=True)

        kf = float(k_top)
        c0 = count(jnp.where(keys >= 0, 1.0, 0.0))
        cand0 = jnp.where(c0 >= kf, 0, INT_MIN).astype(I32)

        def bit_body(b, cand):
            trial = cand | jnp.left_shift(jnp.int32(1), 30 - b)
            cnt = count(jnp.where(keys >= trial[None], 1.0, 0.0))
            return jnp.where(cnt >= kf, trial, cand)

        thr = lax.fori_loop(0, 31, bit_body, cand0)
        cnt_gt = count(jnp.where(keys > thr[None], 1.0, 0.0))
        cnt_ge = count(jnp.where(keys >= thr[None], 1.0, 0.0))
        need = jnp.where(cnt_ge > kf, jnp.where(thr > INT_MIN, 1.0, 0.0), 0.0)
        rem = kf - cnt_gt
        big = jnp.int32(2 ** 30)
        jst_scr[...] = jnp.full((t, 1), big, I32)

        @pl.when(jnp.max(need) > 0.0)
        def _():
            nbits = ((n_pages + 1) * PAGE_SIZE - 1).bit_length()

            def jb(b, pos):
                trial = pos | jnp.left_shift(jnp.int32(1), nbits - 1 - b)
                cnt = count(jnp.where(keys == thr[None], jnp.where(kpos < trial[None], 1.0, 0.0), 0.0))
                return jnp.where(cnt < rem, trial, pos)

            pos = lax.fori_loop(0, nbits, jb, jnp.zeros((t, 1), I32))
            jst_scr[...] = jnp.where(need > 0.0, pos, big)

        jst = jst_scr[...]
        sel = jnp.where(keys > thr[None], 0.0,
                        jnp.where(keys == thr[None], jnp.where(kpos <= jst[None], 0.0, NEG), NEG))
        newc = lax.broadcasted_iota(I32, keys.shape, 0) == n_pages
        causal = lax.broadcasted_iota(I32, keys.shape, 2) <= lax.broadcasted_iota(I32, keys.shape, 1)
        bias_ref[...] = jnp.where(newc, jnp.where(causal, sel, NEG), sel)


def _dsa_sidx(ps, cache_ik, layer, page_table, bs, t):
    n_pages = page_table.shape[1]
    npp = math.gcd(n_pages, _PAGES_PER_STEP)
    k_top = min(IDX_TOPK_MAX, (n_pages * PAGE_SIZE + t) // 4)
    row = lambda b, g, pt: (b, 0)
    grid_spec = pltpu.PrefetchScalarGridSpec(
        num_scalar_prefetch=1, grid=(bs, n_pages // npp),
        in_specs=[pl.BlockSpec((t, 256), row), pl.BlockSpec((t, 128), row)] + _page_specs(layer, IDX_DIM, npp),
        out_specs=pl.BlockSpec((None, n_pages + 1, t, LANES), lambda b, g, pt: (b, 0, 0, 0)),
        scratch_shapes=[pltpu.VMEM((n_pages + 1, t, LANES), I32), pltpu.VMEM((IDX_HEADS * t, IDX_DIM), BF16),
                        pltpu.VMEM((t, 1), I32)])
    return pl.pallas_call(
        functools.partial(_dsa_sidx_kernel, npp=npp, n_pages=n_pages, k_top=k_top),
        grid_spec=grid_spec,
        out_shape=jax.ShapeDtypeStruct((bs, n_pages + 1, t, LANES), F32),
        compiler_params=_cparams(2),
        name="dsa_sample_index",
    )(page_table, ps["iq"], ps["misc"], *([cache_ik] * npp))


def _paged_attn_kernel(pt_ref, q_ref, kn_ref, vn_ref, *rest, npp, nheads, log2_hd, use_bias, finish):
    idx = 0
    if use_bias:
        bias_new_ref, bias_pg_ref = rest[0], rest[1]
        idx = 2
    kpages = rest[idx:idx + npp]
    vpages = rest[idx + npp:idx + 2 * npp]
    n_scr = 7
    extra = rest[idx + 2 * npp:-(n_scr + 1)]
    o_ref = rest[-(n_scr + 1)]
    qa_scr, m_scr, l_scr, alpha_scr, acc_scr, p_scr, s_scr = rest[-n_scr:]
    t = q_ref.shape[0]
    rows = nheads * t
    g = pl.program_id(1)
    last = pl.num_programs(1) - 1

    def tile_rows(b):
        return jnp.concatenate([b] * nheads, axis=0)

    def accumulate(pv):
        alpha = alpha_scr[...]
        acc_scr[...] = jnp.concatenate([alpha, alpha], axis=1) * acc_scr[...] + pv

    @pl.when(g == 0)
    def _():
        qv = q_ref[...]
        grp = _group_of_lane((t, BRANCH_W), log2_hd)
        for h in range(nheads):
            qa_scr[h * t:(h + 1) * t, :] = jnp.where(grp == h, qv, jnp.zeros_like(qv))
        m_scr[...] = jnp.full((rows, LANES), NEG, F32)
        l_scr[...] = jnp.zeros((rows, LANES), F32)
        acc_scr[...] = jnp.zeros((rows, BRANCH_W), F32)
        kn = jnp.concatenate([kn_ref[...], jnp.zeros((PAGE_SIZE - t, BRANCH_W), F32)], axis=0)
        vn = jnp.concatenate([vn_ref[...], jnp.zeros((PAGE_SIZE - t, BRANCH_W), BF16)], axis=0)
        if use_bias:
            bias = bias_new_ref[0]
        else:
            causal = lax.broadcasted_iota(I32, (t, LANES), 1) <= lax.broadcasted_iota(I32, (t, LANES), 0)
            bias = jnp.where(causal, 0.0, NEG)
        s_new = s_scr.at[:, 0:PAGE_SIZE]
        p_new = p_scr.at[:, 0:PAGE_SIZE]
        s_new[...] = jnp.dot(qa_scr[...], kn.T.astype(BF16), preferred_element_type=F32)
        _flash_rows(s_new, tile_rows(bias), m_scr, l_scr, alpha_scr, p_new, 0, rows)
        accumulate(jnp.dot(p_new[...], vn, preferred_element_type=F32))

    kt = jnp.concatenate([kp[...].astype(BF16) for kp in kpages], axis=1)
    vt = jnp.concatenate([vp[...].astype(BF16) for vp in vpages], axis=1)
    s_scr[...] = jnp.dot(qa_scr[...], kt, preferred_element_type=F32)
    bias = None
    if use_bias:
        bias = tile_rows(jnp.concatenate([bias_pg_ref[k] for k in range(npp)], axis=1))
    _flash_rows(s_scr, bias, m_scr, l_scr, alpha_scr, p_scr, 0, rows)
    accumulate(_nk_dot(p_scr[...], vt))

    @pl.when(g == last)
    def _():
        o_ref[...] = finish(acc_scr, l_scr, t, *extra)


def _dsa_finish(acc_ref, l_ref, t):
    grp = _group_of_lane((t, BRANCH_W), 6)
    out = jnp.zeros((t, BRANCH_W), F32)
    for h in range(DSA_HEADS):
        l_row = jnp.sum(l_ref[h * t:(h + 1) * t, :], axis=1, keepdims=True)
        out = out + jnp.where(grp == h, acc_ref[h * t:(h + 1) * t, :] / l_row, 0.0)
    return out


def _diff_sample_finish(acc_ref, l_ref, t, lam_ref, gsub_ref, *, lam_init):
    return _diff_finish(acc_ref, l_ref, t, _diff_lambda(lam_ref, lam_init), lam_init, gsub_ref[...])


def _paged_attn(q, kn, vn, cache_k, cache_v, layer, page_table, bs, t, *, nheads, log2_hd, bias=None,
                extra=(), finish, name):
    n_pages = page_table.shape[1]
    npp = math.gcd(n_pages, _PAGES_PER_STEP)
    row = lambda b, g, pt: (b, 0)
    const = lambda b, g, pt: (0, 0)
    in_specs = [pl.BlockSpec((t, 256), row), pl.BlockSpec((t, 256), row), pl.BlockSpec((t, 256), row)]
    args = [q, kn, vn]
    if bias is not None:
        in_specs += [pl.BlockSpec((None, 1, t, LANES), lambda b, g, pt: (b, n_pages, 0, 0)),
                     pl.BlockSpec((None, npp, t, LANES), lambda b, g, pt: (b, g, 0, 0))]
        args += [bias, bias]
    in_specs += _page_specs(layer, 256, npp) + _page_specs(layer, 256, npp)
    args += [cache_k] * npp + [cache_v] * npp
    for e in extra:
        in_specs.append(pl.BlockSpec(e.shape, const))
        args.append(e)
    rows = nheads * t
    grid_spec = pltpu.PrefetchScalarGridSpec(
        num_scalar_prefetch=1, grid=(bs, n_pages // npp), in_specs=in_specs,
        out_specs=pl.BlockSpec((t, 256), row),
        scratch_shapes=[pltpu.VMEM((rows, 256), BF16), pltpu.VMEM((rows, LANES), F32),
                        pltpu.VMEM((rows, LANES), F32), pltpu.VMEM((rows, LANES), F32),
                        pltpu.VMEM((rows, 256), F32), pltpu.VMEM((rows, npp * PAGE_SIZE), BF16),
                        pltpu.VMEM((rows, npp * PAGE_SIZE), F32)])
    return pl.pallas_call(
        functools.partial(_paged_attn_kernel, npp=npp, nheads=nheads, log2_hd=log2_hd,
                          use_bias=bias is not None, finish=finish),
        grid_spec=grid_spec,
        out_shape=jax.ShapeDtypeStruct((bs * t, BRANCH_W), F32),
        compiler_params=_cparams(2),
        name=name,
    )(page_table, *args)


def _rope_table(pos, half, reps, extra_cos=None):
    inv = ROPE_THETA ** (-jnp.arange(half, dtype=F32) / half)
    ang = pos.astype(F32)[:, None] * inv[None, :]
    cos, sin = jnp.cos(ang), jnp.sin(ang)
    cos_h = jnp.concatenate([cos, cos], axis=1)
    sin_h = jnp.concatenate([-sin, sin], axis=1)
    cos_t, sin_t = jnp.tile(cos_h, (1, reps)), jnp.tile(sin_h, (1, reps))
    if extra_cos is not None:
        n = pos.shape[0]
        cos_t = jnp.concatenate([cos_t, jnp.broadcast_to(extra_cos[None, :], (n, extra_cos.shape[0]))], axis=1)
        sin_t = jnp.concatenate([sin_t, jnp.zeros((n, extra_cos.shape[0]), F32)], axis=1)
    return jnp.concatenate([cos_t, sin_t], axis=1)


def _tables(pos):
    misc_scale = jnp.concatenate([jnp.full((IDX_HEADS,), IDX_HEADS ** -0.5, F32),
                                  jnp.ones((LANES - IDX_DIM - IDX_HEADS,), F32)])
    return (_rope_table(pos, DSA_HEAD_DIM // 2, 4), _rope_table(pos, DIFF_HEAD_DIM // 2, 8),
            _rope_table(pos, IDX_DIM // 2, 1, misc_scale))


def _pack_w_in(w):
    sizes = (256, 256, 256, 256, 256, 64, 4, 256, 256, 256, 256, 256, 512, 4, 256, 256, 4096)
    offs = [0]
    for s in sizes:
        offs.append(offs[-1] + s)
    (dq, dk, dv, dg, iq, ik, iw, fq, fk, fv, fg, sz, sxbc, sdt, pu, pg, mg) = [
        w[:, offs[i]:offs[i + 1]] for i in range(len(sizes))]
    d = w.shape[0]
    misc = jnp.concatenate([ik, iw, jnp.zeros((d, LANES - IDX_DIM - IDX_HEADS), w.dtype)], axis=1)
    dts = jnp.concatenate([sdt, jnp.zeros((d, LANES - SSD_HEADS), w.dtype)], axis=1)
    packed = jnp.concatenate([dq, dk, dv, iq, fq, fk, fv, dg, fg, sz, pg, sxbc, pu, misc, dts], axis=1)
    return packed.astype(BF16), mg.astype(BF16)


def _block_diag(w):
    g, n, _ = w.shape
    out = jnp.zeros((g * n, g * n), w.dtype)
    for i in range(g):
        out = out.at[i * n:(i + 1) * n, i * n:(i + 1) * n].set(w[i])
    return out


def kernel(x_prompt, x_sample, cache_dsa_k, cache_dsa_v, cache_idx_k, cache_diff_k, cache_diff_v, state_ssm,
           state_conv, state_pool, page_table, norm_g, w_in, dsa_qk_g, diff_qk_g, diff_lam, diff_subln,
           ssd_conv_w, ssd_conv_b, ssd_dt_bias, ssd_a_log, ssd_d, ssd_norm, pool_w, pool_scale, w_branch, w_out):
    bp, seq, d = x_prompt.shape
    bs, t, _ = x_sample.shape
    depth = norm_g.shape[0]
    n_pages = page_table.shape[1]
    past = n_pages * PAGE_SIZE
    n_pool = cache_dsa_k.shape[1]

    tab_p = _tables(jnp.arange(seq, dtype=I32))
    tab_s = tuple(jnp.tile(a, (bs, 1)) for a in _tables(past + jnp.arange(t, dtype=I32)))
    ck, cv, cik = _pages_view(cache_dsa_k), _pages_view(cache_dsa_v), _pages_view(cache_idx_k)
    cfk, cfv = _pages_view(cache_diff_k), _pages_view(cache_diff_v)

    tm = min(_KEY_CHUNK, seq)
    tq = min(_Q_BLOCK, seq)
    tm_s = min(256, bs * t)
    pool_t = min(512, seq)
    xp = x_prompt.reshape(bp * seq, d)
    xs = x_sample.reshape(bs * t, d)
    acc_p = {n: [] for n in ("dk", "dv", "ik", "fk", "fv", "ssm", "conv", "pool")}
    acc_s = {n: [] for n in acc_p}

    for l in range(depth):
        lam_init = 0.8 - 0.6 * math.exp(-0.3 * l)
        wp, wmg = _pack_w_in(w_in[l])
        ng = norm_g[l].reshape(1, d)
        gains = jnp.stack([jnp.tile(dsa_qk_g[l, 0], 4), jnp.tile(dsa_qk_g[l, 1], 4),
                           jnp.tile(diff_qk_g[l, 0], 8), jnp.tile(diff_qk_g[l, 1], 8)])
        gsub = jnp.tile(diff_subln[l], 4).reshape(1, BRANCH_W)
        lam_p = diff_lam[l]
        ssd_par = jnp.concatenate([ssd_dt_bias[l], ssd_a_log[l], ssd_d[l]]).astype(F32)
        cw, cb = ssd_conv_w[l], ssd_conv_b[l].reshape(1, SSD_CONV_DIM)
        sng = ssd_norm[l].reshape(1, BRANCH_W)
        wbd = _block_diag(pool_w[l]).astype(BF16)
        pscale = pool_scale[l].reshape(1, BRANCH_W)
        wbr = w_branch[l].astype(BF16)
        wout = w_out[l].astype(BF16)

        pp = _proj(xp, ng, wp, *tab_p, gains, tm, seq // tm)
        dsa_o = _dsa_prompt(pp, bp, seq, tq)
        diff_o = _diff_prompt(pp, lam_p, gsub, lam_init, bp, seq, tq)
        ssd_o, ssm_new, conv_new = _ssd(pp, ssd_par, jnp.zeros((bp, 8, SSD_CONV_DIM), F32),
                                        jnp.zeros((bp, SSD_HEADS, SSD_HEAD_DIM, SSD_STATE), F32), cw, cb, sng,
                                        bp, seq, SSD_CHUNK)
        pool_o, pool_new = _pool(pp["pu"], jnp.zeros((bp, 16, BRANCH_W), F32), wbd, pscale, bp, seq, pool_t, 0)
        xp = _merge(xp, ng, wmg, dsa_o, diff_o, ssd_o, pool_o, pp["gates"], wbr, wout, min(_MERGE_ROWS, seq))
        def heads_out(a, nh, hd):
            return a.reshape(bp, nh, hd, seq).transpose(0, 3, 1, 2)

        for n, v in zip(acc_p, (heads_out(pp["dkTf"], DSA_HEADS, DSA_HEAD_DIM),
                                heads_out(pp["dvTf"], DSA_HEADS, DSA_HEAD_DIM),
                                pp["ikTf"].transpose(0, 2, 1),
                                heads_out(pp["fkTf"], 2 * DIFF_HEADS, DIFF_HEAD_DIM),
                                heads_out(pp["fvTf"], DIFF_HEADS, DIFF_V_DIM),
                                ssm_new, conv_new, pool_new)):
            acc_p[n].append(v)

        ps = _proj(xs, ng, wp, *tab_s, gains, tm_s, (bs * t) // tm_s)
        bias = _dsa_sidx(ps, cik, l, page_table, bs, t)
        dsa_o = _paged_attn(ps["dq"], ps["dk"], ps["dvb"], ck, cv, l, page_table, bs, t,
                            nheads=DSA_HEADS, log2_hd=6, bias=bias, finish=_dsa_finish, name="dsa_sample_attn")
        diff_o = _paged_attn(ps["fq"], ps["fk"], ps["fvb"], cfk, cfv, l, page_table, bs, t,
                             nheads=2 * DIFF_HEADS, log2_hd=5, extra=(lam_p, gsub),
                             finish=functools.partial(_diff_sample_finish, lam_init=lam_init),
                             name="diff_sample_attn")
        cprev8 = jnp.concatenate([jnp.zeros((bs, 8 - (SSD_CONV - 1), SSD_CONV_DIM), F32), state_conv[l]], axis=1)
        ssd_o, ssm_new, conv_new = _ssd(ps, ssd_par, cprev8, state_ssm[l], cw, cb, sng, bs, t, t)
        halo = jnp.concatenate([jnp.zeros((bs, 1, BRANCH_W), F32), state_pool[l]], axis=1)
        pool_o, pool_new = _pool(ps["pu"], halo, wbd, pscale, bs, t, t, past)
        xs = _merge(xs, ng, wmg, dsa_o, diff_o, ssd_o, pool_o, ps["gates"], wbr, wout, tm_s)
        for n, v in zip(acc_s, (ps["dk"].reshape(bs, t, DSA_HEADS, DSA_HEAD_DIM),
                                ps["dv"].reshape(bs, t, DSA_HEADS, DSA_HEAD_DIM),
                                ps["misc"][:, :IDX_DIM].reshape(bs, t, IDX_DIM),
                                ps["fk"].reshape(bs, t, 2 * DIFF_HEADS, DIFF_HEAD_DIM),
                                ps["fv"].reshape(bs, t, DIFF_HEADS, DIFF_V_DIM),
                                ssm_new, conv_new, pool_new)):
            acc_s[n].append(v)

    names = ("dk", "dv", "ik", "fk", "fv", "ssm", "conv", "pool")
    return ((xp.reshape(bp, seq, d), xs.reshape(bs, t, d))
            + tuple(jnp.stack(acc_p[n]) for n in names) + tuple(jnp.stack(acc_s[n]) for n in names))
```
